```python
import math
import jax, jax.numpy as jnp
from jax import lax
import numpy as np

D_MODEL = 1024
BATCH = 16
SEQ = 2048
DEPTH = 2

CHUNK = 64
HEAD_DIM = 64
N_MAIN_HEADS = 12
N_MEM_HEADS = 4
MAIN_WIDTH = N_MAIN_HEADS * HEAD_DIM
MEM_WIDTH = N_MEM_HEADS * HEAD_DIM
MIX_WIDTH = MAIN_WIDTH + MEM_WIDTH
N_MEM = 256
D_FF = 2816
CONV_WIDTH = 3
Q_BLOCK = 128
N_A_LAYERS = DEPTH // 2
N_B_LAYERS = DEPTH - N_A_LAYERS
A_IN_WIDTH = 3 * MAIN_WIDTH + N_MAIN_HEADS + MEM_WIDTH
B_IN_WIDTH = MAIN_WIDTH + MEM_WIDTH
FORGET_BIAS_INIT = 3.0
FORGET_W_SCALE = 0.1
EPS = 1e-6

kernel_name = "yoco_fox_stickbreak_memory_convffn"


def rmsnorm(x, g):
    xf = x.astype(jnp.float32)
    y = xf * lax.rsqrt(jnp.mean(xf * xf, axis=-1, keepdims=True) + EPS)
    return (y * g.astype(jnp.float32)).astype(x.dtype)


def split_heads(t, n_heads):
    b, s, _ = t.shape
    return t.reshape(b, s, n_heads, HEAD_DIM).transpose(0, 2, 1, 3)


def merge_heads(t):
    b, n, s, d = t.shape
    return t.transpose(0, 2, 1, 3).reshape(b, s, n * d)


def forgetting_attention(q, k, v, log_f):
    seq = q.shape[2]
    c = jnp.cumsum(log_f, axis=-1)
    scale = HEAD_DIM ** -0.5
    outs = []
    for i in range(seq // Q_BLOCK):
        q0, q1 = i * Q_BLOCK, (i + 1) * Q_BLOCK
        kb, vb = k[:, :, :q1], v[:, :, :q1]
        logits = jnp.einsum('bhqd,bhkd->bhqk', q[:, :, q0:q1], kb).astype(jnp.float32) * scale
        logits = logits + c[:, :, q0:q1, None] - c[:, :, None, :q1]
        t_idx = jnp.arange(q0, q1)[:, None]
        s_idx = jnp.arange(q1)[None, :]
        logits = jnp.where(s_idx <= t_idx, logits, -jnp.inf)
        p = jax.nn.softmax(logits, axis=-1)
        outs.append(jnp.einsum('bhqk,bhkd->bhqd', p.astype(vb.dtype), vb))
    return jnp.concatenate(outs, axis=2)


def stick_breaking_attention(q, k, v):
    seq = q.shape[2]
    scale = HEAD_DIM ** -0.5
    outs = []
    for i in range(seq // Q_BLOCK):
        q0, q1 = i * Q_BLOCK, (i + 1) * Q_BLOCK
        kb, vb = k[:, :, :q1], v[:, :, :q1]
        z = jnp.einsum('bhqd,bhkd->bhqk', q[:, :, q0:q1], kb).astype(jnp.float32) * scale
        t_idx = jnp.arange(q0, q1)[:, None]
        s_idx = jnp.arange(q1)[None, :]
        causal = s_idx < t_idx
        log_1m_beta = jnp.where(causal, jax.nn.log_sigmoid(-z), 0.0)
        cum = jnp.cumsum(log_1m_beta, axis=-1)
        log_a = jax.nn.log_sigmoid(z) + cum[..., -1:] - cum
        a = jnp.where(causal, jnp.exp(log_a), 0.0)
        outs.append(jnp.einsum('bhqk,bhkd->bhqd', a.astype(vb.dtype), vb))
    return jnp.concatenate(outs, axis=2)


def memory_attention(q, mem_k, mem_v):
    logits = jnp.einsum('bhqd,bhmd->bhqm', q, mem_k).astype(jnp.float32) * (HEAD_DIM ** -0.5)
    p = jax.nn.softmax(logits, axis=-1)
    return jnp.einsum('bhqm,bhmd->bhqd', p.astype(mem_v.dtype), mem_v)


def conv_ffn(h, w_up, conv_w, conv_b, w_down):
    u = h @ w_up
    s = u.shape[1]
    up = jnp.pad(u, ((0, 0), (CONV_WIDTH - 1, 0), (0, 0)))
    uc = conv_b
    for j in range(CONV_WIDTH):
        uc = uc + conv_w[j] * up[:, j:j + s]
    gate, val = jnp.split(uc, 2, axis=-1)
    return (jax.nn.silu(gate) * val) @ w_down


def setup_inputs(seed: int = 0) -> dict:
    key = jax.random.key(seed)
    ks = jax.random.split(key, 17)
    f32 = jnp.float32

    def nrm(k, shape, scale):
        return jax.random.normal(k, shape, f32) * scale

    def gain(k, shape):
        return 1.0 + 0.05 * jax.random.normal(k, shape, f32)

    x = nrm(ks[0], (BATCH, SEQ, D_MODEL), 1.0)
    mem = nrm(ks[1], (BATCH, N_MEM, D_MODEL), 1.0)
    ln_mix_g = gain(ks[2], (DEPTH, D_MODEL))
    w_in_a = nrm(ks[3], (N_A_LAYERS, D_MODEL, A_IN_WIDTH), D_MODEL ** -0.5)
    w_in_a = w_in_a.at[:, :, 3 * MAIN_WIDTH:3 * MAIN_WIDTH + N_MAIN_HEADS].multiply(FORGET_W_SCALE)
    b_f_a = FORGET_BIAS_INIT + 0.5 * jax.random.normal(ks[4], (N_A_LAYERS, N_MAIN_HEADS), f32)
    w_in_b = nrm(ks[5], (N_B_LAYERS, D_MODEL, B_IN_WIDTH), D_MODEL ** -0.5)
    ln_kv_g = gain(ks[6], (D_MODEL,))
    w_kv = nrm(ks[7], (D_MODEL, 2 * MAIN_WIDTH), D_MODEL ** -0.5)
    ln_mem_g = gain(ks[8], (DEPTH, D_MODEL))
    w_memkv = nrm(ks[9], (DEPTH, D_MODEL, 2 * MEM_WIDTH), D_MODEL ** -0.5)
    w_out = nrm(ks[10], (DEPTH, MIX_WIDTH, D_MODEL), MIX_WIDTH ** -0.5)
    ln_ffn_g = gain(ks[11], (DEPTH, D_MODEL))
    w_up = nrm(ks[12], (DEPTH, D_MODEL, 2 * D_FF), D_MODEL ** -0.5)
    conv_w = nrm(ks[13], (DEPTH, CONV_WIDTH, 2 * D_FF), CONV_WIDTH ** -0.5)
    conv_b = nrm(ks[14], (DEPTH, 2 * D_FF), 0.02)
    w_down = nrm(ks[15], (DEPTH, D_FF, D_MODEL), D_FF ** -0.5)
    final_g = gain(ks[16], (D_MODEL,))
    return {'x': x, 'mem': mem, 'ln_mix_g': ln_mix_g, 'w_in_a': w_in_a, 'b_f_a': b_f_a,
            'w_in_b': w_in_b, 'ln_kv_g': ln_kv_g, 'w_kv': w_kv, 'ln_mem_g': ln_mem_g,
            'w_memkv': w_memkv, 'w_out': w_out, 'ln_ffn_g': ln_ffn_g, 'w_up': w_up,
            'conv_w': conv_w, 'conv_b': conv_b, 'w_down': w_down, 'final_g': final_g}


def reference(x, mem, ln_mix_g, w_in_a, b_f_a, w_in_b, ln_kv_g, w_kv, ln_mem_g,
              w_memkv, w_out, ln_ffn_g, w_up, conv_w, conv_b, w_down, final_g):
    k_sh = None
    v_sh = None
    for layer in range(DEPTH):
        h = rmsnorm(x, ln_mix_g[layer])
        if layer < N_A_LAYERS:
            proj = h @ w_in_a[layer]
            q, k, v, f_logit, q_mem = jnp.split(
                proj, [MAIN_WIDTH, 2 * MAIN_WIDTH, 3 * MAIN_WIDTH, 3 * MAIN_WIDTH + N_MAIN_HEADS], axis=-1)
            log_f = jax.nn.log_sigmoid((f_logit + b_f_a[layer]).astype(jnp.float32))
            o_main = forgetting_attention(split_heads(q, N_MAIN_HEADS), split_heads(k, N_MAIN_HEADS),
                                          split_heads(v, N_MAIN_HEADS), log_f.transpose(0, 2, 1))
        else:
            if layer == N_A_LAYERS:
                kv = rmsnorm(x, ln_kv_g) @ w_kv
                k_s, v_s = jnp.split(kv, 2, axis=-1)
                k_sh = split_heads(k_s, N_MAIN_HEADS)
                v_sh = split_heads(v_s, N_MAIN_HEADS)
            proj = h @ w_in_b[layer - N_A_LAYERS]
            q, q_mem = jnp.split(proj, [MAIN_WIDTH], axis=-1)
            o_main = stick_breaking_attention(split_heads(q, N_MAIN_HEADS), k_sh, v_sh)
        mem_kv = rmsnorm(mem, ln_mem_g[layer]) @ w_memkv[layer]
        mk, mv = jnp.split(mem_kv, 2, axis=-1)
        o_mem = memory_attention(split_heads(q_mem, N_MEM_HEADS), split_heads(mk, N_MEM_HEADS),
                                 split_heads(mv, N_MEM_HEADS))
        o = jnp.concatenate([merge_heads(o_main), merge_heads(o_mem)], axis=-1) @ w_out[layer]
        x = x + o
        x = x + conv_ffn(rmsnorm(x, ln_ffn_g[layer]), w_up[layer], conv_w[layer], conv_b[layer], w_down[layer])
    return rmsnorm(x, final_g)
```

```python
import functools

import jax
import jax.numpy as jnp
from jax import lax
from jax.experimental import pallas as pl
from jax.experimental.pallas import tpu as pltpu

HEAD_DIM = 64
N_MAIN_HEADS = 12
N_MEM_HEADS = 4
MAIN_WIDTH = N_MAIN_HEADS * HEAD_DIM
MEM_WIDTH = N_MEM_HEADS * HEAD_DIM
CONV_WIDTH = 3
EPS = 1e-6
Q_SCALE = HEAD_DIM ** -0.5

LANES = 128
SUBLANES = 8
PAIR_WIDTH = 2 * HEAD_DIM
VMEM_LIMIT = 56 * 1024 * 1024

F32 = jnp.float32
BF16 = jnp.bfloat16

_NT = (((1,), (1,)), ((), ()))


def _params(*sem):
    return pltpu.CompilerParams(dimension_semantics=sem, vmem_limit_bytes=VMEM_LIMIT)


def _dot(a, b):
    return jnp.dot(a, b, preferred_element_type=F32)


def _rms_normalize(x):
    return x * lax.rsqrt(jnp.mean(x * x, axis=-1, keepdims=True) + EPS)


def _log_sigmoid(x):
    return jnp.minimum(x, 0.0) - jnp.log1p(jnp.exp(-jnp.abs(x)))


def _low_head_mask():
    return lax.broadcasted_iota(jnp.int32, (1, PAIR_WIDTH), 1) < HEAD_DIM


def _store_chunks(h, w_ref, w_lo, out_ref, width, scale, chunk=256):
    for c in range(0, width, chunk):
        r = _dot(h, w_ref[:, w_lo + c:w_lo + c + chunk])
        if scale is not None:
            r = r * scale
        out_ref[:, c:c + chunk] = r.astype(out_ref.dtype)


def _proj_a_kernel(x_ref, g_ref, wqkv_ref, wqm_ref, wf_ref, bf_ref,
                   q_ref, k_ref, v_ref, qm_ref, ct_ref, carry_ref, *, tiles_per_seq, tk):
    tm = x_ref.shape[0]

    @pl.when(pl.program_id(0) % tiles_per_seq == 0)
    def _():
        carry_ref[...] = jnp.zeros_like(carry_ref)

    h = (_rms_normalize(x_ref[...]) * g_ref[...]).astype(BF16)
    _store_chunks(h, wqkv_ref, 0, q_ref, MAIN_WIDTH, Q_SCALE)
    _store_chunks(h, wqkv_ref, MAIN_WIDTH, k_ref, MAIN_WIDTH, None)
    _store_chunks(h, wqkv_ref, 2 * MAIN_WIDTH, v_ref, MAIN_WIDTH, None)
    _store_chunks(h, wqm_ref, 0, qm_ref, MEM_WIDTH, Q_SCALE)

    log_f = _log_sigmoid(_dot(h, wf_ref[...]) + bf_ref[...])
    hi = log_f.astype(BF16)
    r1 = log_f - hi.astype(F32)
    mid = r1.astype(BF16)
    lo = (r1 - mid.astype(F32)).astype(BF16)
    row = lax.broadcasted_iota(jnp.int32, (tm, tm), 0)
    col = lax.broadcasted_iota(jnp.int32, (tm, tm), 1)
    tri = jnp.where(row >= col, 1.0, 0.0).astype(BF16)
    c = (_dot(tri, hi) + _dot(tri, mid)) + _dot(tri, lo) + carry_ref[0:1, :]
    carry_ref[0:1, :] = c[tm - 1:tm, :]
    for r in range(tm // tk):
        ct_ref[0, r] = c[r * tk:(r + 1) * tk, :].T[:ct_ref.shape[2], :]


def _proj_b_kernel(x_ref, gmix_ref, gkv_ref, win_ref, wkv_ref, q_ref, qm_ref, k_ref, v_ref):
    xn = _rms_normalize(x_ref[...])
    h = (xn * gmix_ref[...]).astype(BF16)
    _store_chunks(h, win_ref, 0, q_ref, MAIN_WIDTH, Q_SCALE)
    _store_chunks(h, win_ref, MAIN_WIDTH, qm_ref, MEM_WIDTH, Q_SCALE)
    hk = (xn * gkv_ref[...]).astype(BF16)
    _store_chunks(hk, wkv_ref, 0, k_ref, MAIN_WIDTH, None)
    _store_chunks(hk, wkv_ref, MAIN_WIDTH, v_ref, MAIN_WIDTH, None)


def _norm_proj_kernel(x_ref, g_ref, w_ref, out_ref):
    h = (_rms_normalize(x_ref[...]) * g_ref[...]).astype(BF16)
    _store_chunks(h, w_ref, 0, out_ref, out_ref.shape[1], None)


def _row_spec(tm, width):
    return pl.BlockSpec((tm, width), lambda i: (i, 0))


def _full_spec(shape):
    return pl.BlockSpec(shape, lambda *_: (0,) * len(shape))


def _proj_a(x2, g, w_in, b_f, seq, tm, tk):
    t, d = x2.shape
    n_f = N_MAIN_HEADS
    wqkv = w_in[:, :3 * MAIN_WIDTH].astype(BF16)
    wf = jnp.pad(w_in[:, 3 * MAIN_WIDTH:3 * MAIN_WIDTH + n_f], ((0, 0), (0, LANES - n_f))).astype(BF16)
    wqm = w_in[:, 3 * MAIN_WIDTH + n_f:].astype(BF16)
    bf = jnp.pad(b_f, (0, LANES - n_f)).reshape(1, LANES)
    ct_rows = 2 * SUBLANES
    kernel = functools.partial(_proj_a_kernel, tiles_per_seq=seq // tm, tk=tk)
    return pl.pallas_call(
        kernel,
        grid=(t // tm,),
        in_specs=[_row_spec(tm, d), _full_spec((1, d)), _full_spec(wqkv.shape), _full_spec(wqm.shape),
                  _full_spec(wf.shape), _full_spec((1, LANES))],
        out_specs=[_row_spec(tm, MAIN_WIDTH), _row_spec(tm, MAIN_WIDTH), _row_spec(tm, MAIN_WIDTH),
                   _row_spec(tm, MEM_WIDTH),
                   pl.BlockSpec((1, tm // tk, ct_rows, tk), lambda i: (i, 0, 0, 0))],
        out_shape=[jax.ShapeDtypeStruct((t, MAIN_WIDTH), BF16)] * 3
        + [jax.ShapeDtypeStruct((t, MEM_WIDTH), BF16),
           jax.ShapeDtypeStruct((t // tm, tm // tk, ct_rows, tk), F32)],
        scratch_shapes=[pltpu.VMEM((SUBLANES, LANES), F32)],
        compiler_params=_params("arbitrary"),
        name="proj_a",
    )(x2, g.reshape(1, d), wqkv, wqm, wf, bf)


def _proj_b(x2, g_mix, g_kv, w_in, w_kv, tm):
    t, d = x2.shape
    win = w_in.astype(BF16)
    wkv = w_kv.astype(BF16)
    return pl.pallas_call(
        _proj_b_kernel,
        grid=(t // tm,),
        in_specs=[_row_spec(tm, d), _full_spec((1, d)), _full_spec((1, d)), _full_spec(win.shape),
                  _full_spec(wkv.shape)],
        out_specs=[_row_spec(tm, MAIN_WIDTH), _row_spec(tm, MEM_WIDTH), _row_spec(tm, MAIN_WIDTH),
                   _row_spec(tm, MAIN_WIDTH)],
        out_shape=[jax.ShapeDtypeStruct((t, MAIN_WIDTH), BF16), jax.ShapeDtypeStruct((t, MEM_WIDTH), BF16),
                   jax.ShapeDtypeStruct((t, MAIN_WIDTH), BF16), jax.ShapeDtypeStruct((t, MAIN_WIDTH), BF16)],
        compiler_params=_params("arbitrary"),
        name="proj_b",
    )(x2, g_mix.reshape(1, d), g_kv.reshape(1, d), win, wkv)


def _norm_proj(x2, g, w, tm):
    t, d = x2.shape
    wb = w.astype(BF16)
    n = wb.shape[1]
    return pl.pallas_call(
        _norm_proj_kernel,
        grid=(t // tm,),
        in_specs=[_row_spec(tm, d), _full_spec((1, d)), _full_spec(wb.shape)],
        out_specs=_row_spec(tm, n),
        out_shape=jax.ShapeDtypeStruct((t, n), BF16),
        compiler_params=_params("arbitrary"),
        name="mem_proj",
    )(x2, g.reshape(1, d), wb)


def _split_pair(x2, low):
    zero = jnp.zeros_like(x2)
    return jnp.where(low, x2, zero), jnp.where(low, zero, x2)


def _softmax_block(s, m, l, acc, v2):
    m_new = jnp.maximum(m, jnp.max(s, axis=-1, keepdims=True))
    p = jnp.exp(s - m_new)
    alpha = jnp.exp(m - m_new)
    l = alpha * l + jnp.sum(p, axis=-1, keepdims=True)
    acc = alpha * acc + _dot(p.astype(BF16), v2)
    return m_new, l, acc


def _memory_heads(qm_ref, mk_ref, mv_ref, o_ref, low):
    tq = qm_ref.shape[0]
    for hp in range(N_MEM_HEADS // 2):
        cols = slice(hp * PAIR_WIDTH, (hp + 1) * PAIR_WIDTH)
        k2 = mk_ref[:, cols]
        v2 = mv_ref[:, cols]
        outs = []
        for qh in _split_pair(qm_ref[:, cols], low):
            s = lax.dot_general(qh, k2, _NT, preferred_element_type=F32)
            init = (jnp.full((tq, 1), -jnp.inf, F32), jnp.zeros((tq, 1), F32), jnp.zeros((tq, PAIR_WIDTH), F32))
            _, l, acc = _softmax_block(s, *init, v2)
            outs.append(acc / l)
        o_ref[:, MAIN_WIDTH + hp * PAIR_WIDTH:MAIN_WIDTH + (hp + 1) * PAIR_WIDTH] = (
            jnp.where(low, outs[0], outs[1]).astype(o_ref.dtype))


def _fox_attn_kernel(q_ref, k_ref, v_ref, ct_ref, qm_ref, mk_ref, mv_ref, o_ref, *, tq):
    qi = pl.program_id(1)
    tk = tq
    low = _low_head_mask()
    row = lax.broadcasted_iota(jnp.int32, (tq, tk), 0)
    col = lax.broadcasted_iota(jnp.int32, (tq, tk), 1)
    causal = col <= row

    for hp in range(N_MAIN_HEADS // 2):
        cols = slice(hp * PAIR_WIDTH, (hp + 1) * PAIR_WIDTH)
        outs = []
        for sub, qh in enumerate(_split_pair(q_ref[:, cols], low)):
            head = 2 * hp + sub

            def scores(kb, qh=qh, head=head, cols=cols):
                ks = pl.multiple_of(kb * tk, tk)
                k2 = k_ref[pl.ds(ks, tk), cols]
                v2 = v_ref[pl.ds(ks, tk), cols]
                s = lax.dot_general(qh, k2, _NT, preferred_element_type=F32)
                return s - ct_ref[0, kb, head:head + 1, :], v2

            def body(kb, carry, scores=scores):
                s, v2 = scores(kb)
                return _softmax_block(s, *carry, v2)

            init = (jnp.full((tq, 1), -jnp.inf, F32), jnp.zeros((tq, 1), F32), jnp.zeros((tq, PAIR_WIDTH), F32))
            carry = lax.fori_loop(0, qi, body, init)
            s, v2 = scores(qi)
            _, l, acc = _softmax_block(jnp.where(causal, s, -jnp.inf), *carry, v2)
            outs.append(acc / l)
        o_ref[:, cols] = jnp.where(low, outs[0], outs[1]).astype(o_ref.dtype)

    _memory_heads(qm_ref, mk_ref, mv_ref, o_ref, low)


def _sb_attn_kernel(q_ref, k_ref, v_ref, qm_ref, mk_ref, mv_ref, o_ref, *, tq, tk):
    qi = pl.program_id(1)
    n_diag = tq // tk
    low = _low_head_mask()
    row = lax.broadcasted_iota(jnp.int32, (tq, tk), 0)
    col = lax.broadcasted_iota(jnp.int32, (tq, tk), 1)
    jj = lax.broadcasted_iota(jnp.int32, (tk, tk + LANES), 0)
    ss = lax.broadcasted_iota(jnp.int32, (tk, tk + LANES), 1)
    suffix = jnp.where((jj > ss) | (ss >= tk), 1.0, 0.0).astype(BF16)

    for hp in range(N_MAIN_HEADS // 2):
        cols = slice(hp * PAIR_WIDTH, (hp + 1) * PAIR_WIDTH)
        outs = []
        for qh in _split_pair(q_ref[:, cols], low):

            def block(kb, carry, causal, qh=qh, cols=cols):
                run, acc = carry
                ks = pl.multiple_of(kb * tk, tk)
                k2 = k_ref[pl.ds(ks, tk), cols]
                v2 = v_ref[pl.ds(ks, tk), cols]
                z = lax.dot_general(qh, k2, _NT, preferred_element_type=F32)
                lp = jnp.log1p(jnp.exp(-jnp.abs(z)))
                log_1m_beta = jnp.minimum(-z, 0.0) - lp
                log_beta = jnp.minimum(z, 0.0) - lp
                if causal is not None:
                    log_1m_beta = jnp.where(causal, log_1m_beta, 0.0)
                hi = log_1m_beta.astype(BF16)
                lo = (log_1m_beta - hi.astype(F32)).astype(BF16)
                r = _dot(jnp.concatenate([hi, lo], axis=0), suffix)
                r = r[:tq] + r[tq:]
                later = r[:, :tk] + jnp.tile(run, (1, tk // LANES))
                a = jnp.exp(log_beta + later)
                if causal is not None:
                    a = jnp.where(causal, a, 0.0)
                return run + r[:, tk:], acc + _dot(a.astype(BF16), v2)

            carry = (jnp.zeros((tq, LANES), F32), jnp.zeros((tq, PAIR_WIDTH), F32))
            for d in reversed(range(n_diag)):
                carry = block(qi * n_diag + d, carry, (col + d * tk) < row)
            carry = lax.fori_loop(
                0, qi * n_diag,
                lambda it, c, block=block: block(qi * n_diag - 1 - it, c, None), carry)
            outs.append(carry[1])
        o_ref[:, cols] = jnp.where(low, outs[0], outs[1]).astype(o_ref.dtype)

    _memory_heads(qm_ref, mk_ref, mv_ref, o_ref, low)


def _attention(kind, q, k, v, ct, qm, memkv, batch, seq, n_mem, tq, tk):
    t = q.shape[0]
    nq = seq // tq
    q_spec = pl.BlockSpec((tq, MAIN_WIDTH), lambda b, i: (b * nq + i, 0))
    kv_spec = pl.BlockSpec((seq, MAIN_WIDTH), lambda b, i: (b, 0))
    qm_spec = pl.BlockSpec((tq, MEM_WIDTH), lambda b, i: (b * nq + i, 0))
    mk_spec = pl.BlockSpec((n_mem, MEM_WIDTH), lambda b, i: (b, 0))
    mv_spec = pl.BlockSpec((n_mem, MEM_WIDTH), lambda b, i: (b, 1))
    o_spec = pl.BlockSpec((tq, MAIN_WIDTH + MEM_WIDTH), lambda b, i: (b * nq + i, 0))
    if kind == "fox":
        tiles = seq // ct.shape[1] // ct.shape[3]
        ct = ct.reshape(batch, tiles * ct.shape[1], ct.shape[2], ct.shape[3])
        ct_spec = pl.BlockSpec((1,) + ct.shape[1:], lambda b, i: (b, 0, 0, 0))
        kernel = functools.partial(_fox_attn_kernel, tq=tq)
        in_specs = [q_spec, kv_spec, kv_spec, ct_spec, qm_spec, mk_spec, mv_spec]
        args = (q, k, v, ct, qm, memkv, memkv)
    else:
        kernel = functools.partial(_sb_attn_kernel, tq=tq, tk=tk)
        in_specs = [q_spec, kv_spec, kv_spec, qm_spec, mk_spec, mv_spec]
        args = (q, k, v, qm, memkv, memkv)
    return pl.pallas_call(
        kernel,
        grid=(batch, nq),
        in_specs=in_specs,
        out_specs=o_spec,
        out_shape=jax.ShapeDtypeStruct((t, MAIN_WIDTH + MEM_WIDTH), BF16),
        compiler_params=_params("arbitrary", "arbitrary"),
        name=kind + "_attn",
    )(*args)


def _causal_conv(u, prev, cw_ref, cb_ref):
    tm = u.shape[0]
    r8 = lax.broadcasted_iota(jnp.int32, (SUBLANES, 1), 0)
    out = cb_ref[...]
    for j in range(CONV_WIDTH - 1):
        shift = CONV_WIDTH - 1 - j
        rolled = pltpu.roll(u, shift, axis=0)
        top = jnp.where(r8 < shift, pltpu.roll(prev, shift, axis=0), rolled[:SUBLANES])
        shifted = jnp.concatenate([top, rolled[SUBLANES:]], axis=0)
        out = out + cw_ref[j:j + 1, :] * shifted
    return out + cw_ref[CONV_WIDTH - 1:CONV_WIDTH, :] * u


def _out_ffn_kernel(x_ref, o_ref, wout_ref, g_ref, wg_ref, wv_ref, cwg_ref, cwv_ref, cbg_ref, cbv_ref,
                    wd_ref, fg_ref, out_ref, h_ref, carry_ref, *, tiles_per_seq, final_norm):
    i = pl.program_id(0)
    j = pl.program_id(1)
    tm = x_ref.shape[0]

    @pl.when(j == 0)
    def _():
        x1 = x_ref[...] + _dot(o_ref[...], wout_ref[...])
        out_ref[...] = x1
        h_ref[...] = (_rms_normalize(x1) * g_ref[...]).astype(BF16)

    h = h_ref[...]
    ug = _dot(h, wg_ref[...])
    uv = _dot(h, wv_ref[...])
    @pl.when(i % tiles_per_seq == 0)
    def _():
        carry_ref[j] = jnp.zeros_like(carry_ref[j])

    prev_g = carry_ref[j, 0]
    prev_v = carry_ref[j, 1]
    carry_ref[j, 0] = ug[tm - SUBLANES:]
    carry_ref[j, 1] = uv[tm - SUBLANES:]
    gate = _causal_conv(ug, prev_g, cwg_ref, cbg_ref)
    val = _causal_conv(uv, prev_v, cwv_ref, cbv_ref)
    act = gate * (1.0 / (1.0 + jnp.exp(-gate))) * val
    out_ref[...] += _dot(act.astype(BF16), wd_ref[...])

    if final_norm:
        @pl.when(j == pl.num_programs(1) - 1)
        def _():
            out_ref[...] = _rms_normalize(out_ref[...]) * fg_ref[...]


def _out_ffn(x2, o, w_out, g, w_up, conv_w, conv_b, w_down, final_g, seq, tm, tf, final_norm):
    t, d = x2.shape
    d_ff = w_down.shape[0]
    nf = d_ff // tf
    wout = w_out.astype(BF16)
    wup = w_up.astype(BF16)
    wdn = w_down.astype(BF16)
    cb = conv_b.reshape(1, 2 * d_ff)
    kernel = functools.partial(_out_ffn_kernel, tiles_per_seq=seq // tm, final_norm=final_norm)
    return pl.pallas_call(
        kernel,
        grid=(t // tm, nf),
        in_specs=[pl.BlockSpec((tm, d), lambda i, j: (i, 0)),
                  pl.BlockSpec((tm, o.shape[1]), lambda i, j: (i, 0)),
                  _full_spec(wout.shape), _full_spec((1, d)),
                  pl.BlockSpec((d, tf), lambda i, j: (0, j)),
                  pl.BlockSpec((d, tf), lambda i, j: (0, nf + j)),
                  pl.BlockSpec((CONV_WIDTH, tf), lambda i, j: (0, j)),
                  pl.BlockSpec((CONV_WIDTH, tf), lambda i, j: (0, nf + j)),
                  pl.BlockSpec((1, tf), lambda i, j: (0, j)),
                  pl.BlockSpec((1, tf), lambda i, j: (0, nf + j)),
                  pl.BlockSpec((tf, d), lambda i, j: (j, 0)),
                  _full_spec((1, d))],
        out_specs=pl.BlockSpec((tm, d), lambda i, j: (i, 0)),
        out_shape=jax.ShapeDtypeStruct((t, d), F32),
        scratch_shapes=[pltpu.VMEM((tm, d), BF16), pltpu.VMEM((nf, 2, SUBLANES, tf), F32)],
        compiler_params=_params("arbitrary", "arbitrary"),
        name="out_ffn",
    )(x2, o, wout, g.reshape(1, d), wup, wup, conv_w, conv_w, cb, cb, wdn, final_g.reshape(1, d))


def kernel(x, mem, ln_mix_g, w_in_a, b_f_a, w_in_b, ln_kv_g, w_kv, ln_mem_g, w_memkv, w_out, ln_ffn_g,
           w_up, conv_w, conv_b, w_down, final_g):
    batch, seq, d = x.shape
    n_mem = mem.shape[1]
    depth = ln_mix_g.shape[0]
    n_a = w_in_a.shape[0]
    tm_proj = min(512, seq)
    tq_fox = min(256, seq)
    tq_sb, tk_sb = min(256, seq), 128
    tm_ffn, tf = min(512, seq), 256

    x2 = x.reshape(batch * seq, d)
    mem2 = mem.reshape(batch * n_mem, d)
    k_sh = v_sh = None
    for layer in range(depth):
        memkv = _norm_proj(mem2, ln_mem_g[layer], w_memkv[layer], min(512, batch * n_mem))
        if layer < n_a:
            q, k, v, qm, ct = _proj_a(x2, ln_mix_g[layer], w_in_a[layer], b_f_a[layer], seq, tm_proj, tq_fox)
            o = _attention("fox", q, k, v, ct, qm, memkv, batch, seq, n_mem, tq_fox, tq_fox)
        else:
            lb = layer - n_a
            if lb == 0:
                q, qm, k_sh, v_sh = _proj_b(x2, ln_mix_g[layer], ln_kv_g, w_in_b[lb], w_kv, tm_proj)
            else:
                qqm = _norm_proj(x2, ln_mix_g[layer], w_in_b[lb], tm_proj)
                q = (qqm[:, :MAIN_WIDTH].astype(F32) * Q_SCALE).astype(BF16)
                qm = (qqm[:, MAIN_WIDTH:].astype(F32) * Q_SCALE).astype(BF16)
            o = _attention("sb", q, k_sh, v_sh, None, qm, memkv, batch, seq, n_mem, tq_sb, tk_sb)
        x2 = _out_ffn(x2, o, w_out[layer], ln_ffn_g[layer], w_up[layer], conv_w[layer], conv_b[layer],
                      w_down[layer], final_g, seq, tm_ffn, tf, final_norm=(layer == depth - 1))
    return x2.reshape(batch, seq, d)
```

```python
import functools

import jax
import jax.numpy as jnp
from jax import lax
from jax.experimental import pallas as pl
from jax.experimental.pallas import tpu as pltpu

HEAD_DIM = 64
N_MAIN_HEADS = 12
N_MEM_HEADS = 4
MAIN_WIDTH = N_MAIN_HEADS * HEAD_DIM
MEM_WIDTH = N_MEM_HEADS * HEAD_DIM
CONV_WIDTH = 3
EPS = 1e-6
Q_SCALE = HEAD_DIM ** -0.5

LANES = 128
SUBLANES = 8
PAIR_WIDTH = 2 * HEAD_DIM
VMEM_LIMIT = 56 * 1024 * 1024

F32 = jnp.float32
BF16 = jnp.bfloat16

_NT = (((1,), (1,)), ((), ()))


def _params(*sem):
    return pltpu.CompilerParams(dimension_semantics=sem, vmem_limit_bytes=VMEM_LIMIT)


def _dot(a, b):
    return jnp.dot(a, b, preferred_element_type=F32)


def _rms_normalize(x):
    return x * lax.rsqrt(jnp.mean(x * x, axis=-1, keepdims=True) + EPS)


def _log_sigmoid(x):
    return jnp.minimum(x, 0.0) - jnp.log1p(jnp.exp(-jnp.abs(x)))


def _low_head_mask():
    return lax.broadcasted_iota(jnp.int32, (1, PAIR_WIDTH), 1) < HEAD_DIM


def _store_chunks(h, w_ref, w_lo, out_ref, width, scale, chunk=256):
    for c in range(0, width, chunk):
        r = _dot(h, w_ref[:, w_lo + c:w_lo + c + chunk])
        if scale is not None:
            r = r * scale
        out_ref[:, c:c + chunk] = r.astype(out_ref.dtype)


def _proj_a_kernel(x_ref, g_ref, wqkv_ref, wqm_ref, wf_ref, bf_ref,
                   q_ref, k_ref, v_ref, qm_ref, ct_ref, carry_ref, *, tiles_per_seq, tk):
    tm = x_ref.shape[0]

    @pl.when(pl.program_id(0) % tiles_per_seq == 0)
    def _():
        carry_ref[...] = jnp.zeros_like(carry_ref)

    h = (_rms_normalize(x_ref[...]) * g_ref[...]).astype(BF16)
    _store_chunks(h, wqkv_ref, 0, q_ref, MAIN_WIDTH, Q_SCALE)
    _store_chunks(h, wqkv_ref, MAIN_WIDTH, k_ref, MAIN_WIDTH, None)
    _store_chunks(h, wqkv_ref, 2 * MAIN_WIDTH, v_ref, MAIN_WIDTH, None)
    _store_chunks(h, wqm_ref, 0, qm_ref, MEM_WIDTH, Q_SCALE)

    log_f = _log_sigmoid(_dot(h, wf_ref[...]) + bf_ref[...])
    hi = log_f.astype(BF16)
    r1 = log_f - hi.astype(F32)
    mid = r1.astype(BF16)
    lo = (r1 - mid.astype(F32)).astype(BF16)
    row = lax.broadcasted_iota(jnp.int32, (tm, tm), 0)
    col = lax.broadcasted_iota(jnp.int32, (tm, tm), 1)
    tri = jnp.where(row >= col, 1.0, 0.0).astype(BF16)
    c = (_dot(tri, hi) + _dot(tri, mid)) + _dot(tri, lo) + carry_ref[0:1, :]
    carry_ref[0:1, :] = c[tm - 1:tm, :]
    for r in range(tm // tk):
        ct_ref[0, r] = c[r * tk:(r + 1) * tk, :].T[:ct_ref.shape[2], :]


def _proj_b_kernel(x_ref, gmix_ref, gkv_ref, win_ref, wkv_ref, q_ref, qm_ref, k_ref, v_ref):
    xn = _rms_normalize(x_ref[...])
    h = (xn * gmix_ref[...]).astype(BF16)
    _store_chunks(h, win_ref, 0, q_ref, MAIN_WIDTH, Q_SCALE)
    _store_chunks(h, win_ref, MAIN_WIDTH, qm_ref, MEM_WIDTH, Q_SCALE)
    hk = (xn * gkv_ref[...]).astype(BF16)
    _store_chunks(hk, wkv_ref, 0, k_ref, MAIN_WIDTH, None)
    _store_chunks(hk, wkv_ref, MAIN_WIDTH, v_ref, MAIN_WIDTH, None)


def _norm_proj_kernel(x_ref, g_ref, w_ref, out_ref):
    h = (_rms_normalize(x_ref[...]) * g_ref[...]).astype(BF16)
    _store_chunks(h, w_ref, 0, out_ref, out_ref.shape[1], None)


def _row_spec(tm, width):
    return pl.BlockSpec((tm, width), lambda i: (i, 0))


def _full_spec(shape):
    return pl.BlockSpec(shape, lambda *_: (0,) * len(shape))


def _proj_a(x2, g, w_in, b_f, seq, tm, tk):
    t, d = x2.shape
    n_f = N_MAIN_HEADS
    wqkv = w_in[:, :3 * MAIN_WIDTH].astype(BF16)
    wf = jnp.pad(w_in[:, 3 * MAIN_WIDTH:3 * MAIN_WIDTH + n_f], ((0, 0), (0, LANES - n_f))).astype(BF16)
    wqm = w_in[:, 3 * MAIN_WIDTH + n_f:].astype(BF16)
    bf = jnp.pad(b_f, (0, LANES - n_f)).reshape(1, LANES)
    ct_rows = 2 * SUBLANES
    kernel = functools.partial(_proj_a_kernel, tiles_per_seq=seq // tm, tk=tk)
    return pl.pallas_call(
        kernel,
        grid=(t // tm,),
        in_specs=[_row_spec(tm, d), _full_spec((1, d)), _full_spec(wqkv.shape), _full_spec(wqm.shape),
                  _full_spec(wf.shape), _full_spec((1, LANES))],
        out_specs=[_row_spec(tm, MAIN_WIDTH), _row_spec(tm, MAIN_WIDTH), _row_spec(tm, MAIN_WIDTH),
                   _row_spec(tm, MEM_WIDTH),
                   pl.BlockSpec((1, tm // tk, ct_rows, tk), lambda i: (i, 0, 0, 0))],
        out_shape=[jax.ShapeDtypeStruct((t, MAIN_WIDTH), BF16)] * 3
        + [jax.ShapeDtypeStruct((t, MEM_WIDTH), BF16),
           jax.ShapeDtypeStruct((t // tm, tm // tk, ct_rows, tk), F32)],
        scratch_shapes=[pltpu.VMEM((SUBLANES, LANES), F32)],
        compiler_params=_params("arbitrary"),
        name="proj_a",
    )(x2, g.reshape(1, d), wqkv, wqm, wf, bf)


def _proj_b(x2, g_mix, g_kv, w_in, w_kv, tm):
    t, d = x2.shape
    win = w_in.astype(BF16)
    wkv = w_kv.astype(BF16)
    return pl.pallas_call(
        _proj_b_kernel,
        grid=(t // tm,),
        in_specs=[_row_spec(tm, d), _full_spec((1, d)), _full_spec((1, d)), _full_spec(win.shape),
                  _full_spec(wkv.shape)],
        out_specs=[_row_spec(tm, MAIN_WIDTH), _row_spec(tm, MEM_WIDTH), _row_spec(tm, MAIN_WIDTH),
                   _row_spec(tm, MAIN_WIDTH)],
        out_shape=[jax.ShapeDtypeStruct((t, MAIN_WIDTH), BF16), jax.ShapeDtypeStruct((t, MEM_WIDTH), BF16),
                   jax.ShapeDtypeStruct((t, MAIN_WIDTH), BF16), jax.ShapeDtypeStruct((t, MAIN_WIDTH), BF16)],
        compiler_params=_params("arbitrary"),
        name="proj_b",
    )(x2, g_mix.reshape(1, d), g_kv.reshape(1, d), win, wkv)


def _norm_proj(x2, g, w, tm):
    t, d = x2.shape
    wb = w.astype(BF16)
    n = wb.shape[1]
    return pl.pallas_call(
        _norm_proj_kernel,
        grid=(t // tm,),
        in_specs=[_row_spec(tm, d), _full_spec((1, d)), _full_spec(wb.shape)],
        out_specs=_row_spec(tm, n),
        out_shape=jax.ShapeDtypeStruct((t, n), BF16),
        compiler_params=_params("arbitrary"),
        name="mem_proj",
    )(x2, g.reshape(1, d), wb)


N_HEADS = N_MAIN_HEADS + N_MEM_HEADS


def _split_heads_into(q_ref, dst_ref, first, n_heads, low):
    for hp in range(n_heads // 2):
        q2 = q_ref[:, hp * PAIR_WIDTH:(hp + 1) * PAIR_WIDTH]
        zero = jnp.zeros_like(q2)
        dst_ref[first + 2 * hp] = jnp.where(low, q2, zero)
        dst_ref[first + 2 * hp + 1] = jnp.where(low, zero, q2)


def _values_with_ones(v2, sub, low):
    one = jnp.ones_like(v2)
    return jnp.where(low, v2, one) if sub == 0 else jnp.where(low, one, v2)


def _chunks(s):
    return [s[:, c:c + LANES] for c in range(0, s.shape[1], LANES)]


def _softmax_update(s_chunks, head, vh, m_ref, acc_ref):
    m_prev = m_ref[head]
    smax = functools.reduce(jnp.maximum, s_chunks)
    m_new = jnp.maximum(m_prev, jnp.max(smax, axis=-1, keepdims=True))
    p = jnp.concatenate([jnp.exp(s - m_new) for s in s_chunks], axis=-1).astype(BF16)
    acc_ref[head] = jnp.exp(m_prev - m_new) * acc_ref[head] + _dot(p, vh)
    m_ref[head] = m_new


def _softmax_finish(acc_ref, first, n_heads, o_ref, col0, low):
    for hp in range(n_heads // 2):
        outs = []
        for sub in range(2):
            acc = acc_ref[first + 2 * hp + sub]
            outs.append(acc / pltpu.roll(acc, HEAD_DIM, axis=1))
        o_ref[:, col0 + hp * PAIR_WIDTH:col0 + (hp + 1) * PAIR_WIDTH] = (
            jnp.where(low, outs[0], outs[1]).astype(o_ref.dtype))


def _memory_heads(qs_ref, mk_ref, mv_ref, m_ref, acc_ref, o_ref, low):
    for hp in range(N_MEM_HEADS // 2):
        cols = slice(hp * PAIR_WIDTH, (hp + 1) * PAIR_WIDTH)
        k2 = mk_ref[:, cols]
        v2 = mv_ref[:, cols]
        for sub in range(2):
            head = N_MAIN_HEADS + 2 * hp + sub
            s = lax.dot_general(qs_ref[head], k2, _NT, preferred_element_type=F32)
            _softmax_update(_chunks(s), head, _values_with_ones(v2, sub, low), m_ref, acc_ref)
    _softmax_finish(acc_ref, N_MAIN_HEADS, N_MEM_HEADS, o_ref, MAIN_WIDTH, low)


def _fox_attn_kernel(q_ref, k_ref, v_ref, ct_ref, qm_ref, mk_ref, mv_ref, o_ref, qs_ref, m_ref, acc_ref, *, tq):
    qi = pl.program_id(1)
    tk = tq
    low = _low_head_mask()
    _split_heads_into(q_ref, qs_ref, 0, N_MAIN_HEADS, low)
    _split_heads_into(qm_ref, qs_ref, N_MAIN_HEADS, N_MEM_HEADS, low)
    m_ref[...] = jnp.full_like(m_ref, -jnp.inf)
    acc_ref[...] = jnp.zeros_like(acc_ref)

    def block(kb, causal):
        ks = pl.multiple_of(kb * tk, tk)
        for hp in range(N_MAIN_HEADS // 2):
            cols = slice(hp * PAIR_WIDTH, (hp + 1) * PAIR_WIDTH)
            k2 = k_ref[pl.ds(ks, tk), cols]
            v2 = v_ref[pl.ds(ks, tk), cols]
            for sub in range(2):
                head = 2 * hp + sub
                s = lax.dot_general(qs_ref[head], k2, _NT, preferred_element_type=F32)
                s = s - ct_ref[0, kb, head:head + 1, :]
                if causal is not None:
                    s = jnp.where(causal, s, -jnp.inf)
                _softmax_update(_chunks(s), head, _values_with_ones(v2, sub, low), m_ref, acc_ref)

    def body(kb, carry):
        block(kb, None)
        return carry

    lax.fori_loop(0, qi, body, 0)
    row = lax.broadcasted_iota(jnp.int32, (tq, tk), 0)
    col = lax.broadcasted_iota(jnp.int32, (tq, tk), 1)
    block(qi, col <= row)
    _softmax_finish(acc_ref, 0, N_MAIN_HEADS, o_ref, 0, low)
    _memory_heads(qs_ref, mk_ref, mv_ref, m_ref, acc_ref, o_ref, low)


def _sb_attn_kernel(q_ref, k_ref, v_ref, qm_ref, mk_ref, mv_ref, o_ref, qs_ref, m_ref, acc_ref, run_ref, *, tq):
    qi = pl.program_id(1)
    tk = tq
    low = _low_head_mask()
    _split_heads_into(q_ref, qs_ref, 0, N_MAIN_HEADS, low)
    _split_heads_into(qm_ref, qs_ref, N_MAIN_HEADS, N_MEM_HEADS, low)
    m_ref[...] = jnp.full_like(m_ref, -jnp.inf)
    acc_ref[...] = jnp.zeros_like(acc_ref)
    run_ref[...] = jnp.zeros_like(run_ref)

    jj = lax.broadcasted_iota(jnp.int32, (2 * LANES, 2 * LANES), 0) % LANES
    ss = lax.broadcasted_iota(jnp.int32, (2 * LANES, 2 * LANES), 1)
    suffix = jnp.where((jj > ss) | (ss >= LANES), 1.0, 0.0).astype(BF16)

    def block(kb, causal):
        ks = pl.multiple_of(kb * tk, tk)
        for hp in range(N_MAIN_HEADS // 2):
            cols = slice(hp * PAIR_WIDTH, (hp + 1) * PAIR_WIDTH)
            k2 = k_ref[pl.ds(ks, tk), cols]
            v2 = v_ref[pl.ds(ks, tk), cols]
            for sub in range(2):
                head = 2 * hp + sub
                z = lax.dot_general(qs_ref[head], k2, _NT, preferred_element_type=F32)
                run = run_ref[head]
                a_chunks = [None] * (tk // LANES)
                for c in reversed(range(tk // LANES)):
                    zc = z[:, c * LANES:(c + 1) * LANES]
                    nl = jnp.maximum(zc, 0.0) + jnp.log1p(jnp.exp(-jnp.abs(zc)))
                    if causal is not None:
                        nl = jnp.where(causal[c], nl, 0.0)
                    hi = nl.astype(BF16)
                    lo = (nl - hi.astype(F32)).astype(BF16)
                    r = _dot(jnp.concatenate([hi, lo], axis=1), suffix)
                    a = jnp.exp((zc - nl) - (r[:, :LANES] + run))
                    if causal is not None:
                        a = jnp.where(causal[c], a, 0.0)
                    a_chunks[c] = a.astype(BF16)
                    run = run + r[:, LANES:]
                run_ref[head] = run
                acc_ref[head] += _dot(jnp.concatenate(a_chunks, axis=1), v2)

    row = lax.broadcasted_iota(jnp.int32, (tq, LANES), 0)
    col = lax.broadcasted_iota(jnp.int32, (tq, LANES), 1)
    block(qi, [(col + c * LANES) < row for c in range(tk // LANES)])

    def body(it, carry):
        block(qi - 1 - it, None)
        return carry

    lax.fori_loop(0, qi, body, 0)
    for hp in range(N_MAIN_HEADS // 2):
        o_ref[:, hp * PAIR_WIDTH:(hp + 1) * PAIR_WIDTH] = (
            jnp.where(low, acc_ref[2 * hp], acc_ref[2 * hp + 1]).astype(o_ref.dtype))
    _memory_heads(qs_ref, mk_ref, mv_ref, m_ref, acc_ref, o_ref, low)


def _attention(kind, q, k, v, ct, qm, memkv, batch, seq, n_mem, tq):
    t = q.shape[0]
    nq = seq // tq
    q_spec = pl.BlockSpec((tq, MAIN_WIDTH), lambda b, i: (b * nq + i, 0))
    kv_spec = pl.BlockSpec((seq, MAIN_WIDTH), lambda b, i: (b, 0))
    qm_spec = pl.BlockSpec((tq, MEM_WIDTH), lambda b, i: (b * nq + i, 0))
    mk_spec = pl.BlockSpec((n_mem, MEM_WIDTH), lambda b, i: (b, 0))
    mv_spec = pl.BlockSpec((n_mem, MEM_WIDTH), lambda b, i: (b, 1))
    o_spec = pl.BlockSpec((tq, MAIN_WIDTH + MEM_WIDTH), lambda b, i: (b * nq + i, 0))
    scratch = [pltpu.VMEM((N_HEADS, tq, PAIR_WIDTH), BF16),
               pltpu.VMEM((N_HEADS, tq, LANES), F32),
               pltpu.VMEM((N_HEADS, tq, PAIR_WIDTH), F32)]
    if kind == "fox":
        tiles = seq // ct.shape[1] // ct.shape[3]
        ct = ct.reshape(batch, tiles * ct.shape[1], ct.shape[2], ct.shape[3])
        ct_spec = pl.BlockSpec((1,) + ct.shape[1:], lambda b, i: (b, 0, 0, 0))
        kernel = functools.partial(_fox_attn_kernel, tq=tq)
        in_specs = [q_spec, kv_spec, kv_spec, ct_spec, qm_spec, mk_spec, mv_spec]
        args = (q, k, v, ct, qm, memkv, memkv)
    else:
        kernel = functools.partial(_sb_attn_kernel, tq=tq)
        in_specs = [q_spec, kv_spec, kv_spec, qm_spec, mk_spec, mv_spec]
        args = (q, k, v, qm, memkv, memkv)
        scratch.append(pltpu.VMEM((N_MAIN_HEADS, tq, LANES), F32))
    return pl.pallas_call(
        kernel,
        grid=(batch, nq),
        in_specs=in_specs,
        out_specs=o_spec,
        out_shape=jax.ShapeDtypeStruct((t, MAIN_WIDTH + MEM_WIDTH), BF16),
        scratch_shapes=scratch,
        compiler_params=_params("arbitrary", "arbitrary"),
        name=kind + "_attn",
    )(*args)


def _causal_conv(u, prev, cw_ref, cb_ref):
    tm = u.shape[0]
    r8 = lax.broadcasted_iota(jnp.int32, (SUBLANES, 1), 0)
    out = cb_ref[...]
    for j in range(CONV_WIDTH - 1):
        shift = CONV_WIDTH - 1 - j
        rolled = pltpu.roll(u, shift, axis=0)
        top = jnp.where(r8 < shift, pltpu.roll(prev, shift, axis=0), rolled[:SUBLANES])
        shifted = jnp.concatenate([top, rolled[SUBLANES:]], axis=0)
        out = out + cw_ref[j:j + 1, :] * shifted
    return out + cw_ref[CONV_WIDTH - 1:CONV_WIDTH, :] * u


def _out_ffn_kernel(x_ref, o_ref, wout_ref, g_ref, wg_ref, wv_ref, cwg_ref, cwv_ref, cbg_ref, cbv_ref,
                    wd_ref, fg_ref, out_ref, h_ref, carry_ref, *, tiles_per_seq, final_norm):
    i = pl.program_id(0)
    j = pl.program_id(1)
    tm = x_ref.shape[0]

    @pl.when(j == 0)
    def _():
        x1 = x_ref[...] + _dot(o_ref[...], wout_ref[...])
        out_ref[...] = x1
        h_ref[...] = (_rms_normalize(x1) * g_ref[...]).astype(BF16)

    h = h_ref[...]
    ug = _dot(h, wg_ref[...])
    uv = _dot(h, wv_ref[...])
    @pl.when(i % tiles_per_seq == 0)
    def _():
        carry_ref[j] = jnp.zeros_like(carry_ref[j])

    prev_g = carry_ref[j, 0]
    prev_v = carry_ref[j, 1]
    carry_ref[j, 0] = ug[tm - SUBLANES:]
    carry_ref[j, 1] = uv[tm - SUBLANES:]
    gate = _causal_conv(ug, prev_g, cwg_ref, cbg_ref)
    val = _causal_conv(uv, prev_v, cwv_ref, cbv_ref)
    act = gate * (1.0 / (1.0 + jnp.exp(-gate))) * val
    out_ref[...] += _dot(act.astype(BF16), wd_ref[...])

    if final_norm:
        @pl.when(j == pl.num_programs(1) - 1)
        def _():
            out_ref[...] = _rms_normalize(out_ref[...]) * fg_ref[...]


def _out_ffn(x2, o, w_out, g, w_up, conv_w, conv_b, w_down, final_g, seq, tm, tf, final_norm):
    t, d = x2.shape
    d_ff = w_down.shape[0]
    nf = d_ff // tf
    wout = w_out.astype(BF16)
    wup = w_up.astype(BF16)
    wdn = w_down.astype(BF16)
    cb = conv_b.reshape(1, 2 * d_ff)
    kernel = functools.partial(_out_ffn_kernel, tiles_per_seq=seq // tm, final_norm=final_norm)
    return pl.pallas_call(
        kernel,
        grid=(t // tm, nf),
        in_specs=[pl.BlockSpec((tm, d), lambda i, j: (i, 0)),
                  pl.BlockSpec((tm, o.shape[1]), lambda i, j: (i, 0)),
                  _full_spec(wout.shape), _full_spec((1, d)),
                  pl.BlockSpec((d, tf), lambda i, j: (0, j)),
                  pl.BlockSpec((d, tf), lambda i, j: (0, nf + j)),
                  pl.BlockSpec((CONV_WIDTH, tf), lambda i, j: (0, j)),
                  pl.BlockSpec((CONV_WIDTH, tf), lambda i, j: (0, nf + j)),
                  pl.BlockSpec((1, tf), lambda i, j: (0, j)),
                  pl.BlockSpec((1, tf), lambda i, j: (0, nf + j)),
                  pl.BlockSpec((tf, d), lambda i, j: (j, 0)),
                  _full_spec((1, d))],
        out_specs=pl.BlockSpec((tm, d), lambda i, j: (i, 0)),
        out_shape=jax.ShapeDtypeStruct((t, d), F32),
        scratch_shapes=[pltpu.VMEM((tm, d), BF16), pltpu.VMEM((nf, 2, SUBLANES, tf), F32)],
        compiler_params=_params("arbitrary", "arbitrary"),
        name="out_ffn",
    )(x2, o, wout, g.reshape(1, d), wup, wup, conv_w, conv_w, cb, cb, wdn, final_g.reshape(1, d))


def kernel(x, mem, ln_mix_g, w_in_a, b_f_a, w_in_b, ln_kv_g, w_kv, ln_mem_g, w_memkv, w_out, ln_ffn_g,
           w_up, conv_w, conv_b, w_down, final_g):
    batch, seq, d = x.shape
    n_mem = mem.shape[1]
    depth = ln_mix_g.shape[0]
    n_a = w_in_a.shape[0]
    assert w_in_b.shape[0] == 1, "one stick-breaking layer reads the shared K/V"
    tm_proj = min(512, seq)
    tq_attn = min(256, seq)
    tm_ffn, tf = min(512, seq), 256

    x2 = x.reshape(batch * seq, d)
    mem2 = mem.reshape(batch * n_mem, d)
    for layer in range(depth):
        memkv = _norm_proj(mem2, ln_mem_g[layer], w_memkv[layer], min(512, batch * n_mem))
        if layer < n_a:
            q, k, v, qm, ct = _proj_a(x2, ln_mix_g[layer], w_in_a[layer], b_f_a[layer], seq, tm_proj, tq_attn)
            o = _attention("fox", q, k, v, ct, qm, memkv, batch, seq, n_mem, tq_attn)
        else:
            q, qm, k_sh, v_sh = _proj_b(x2, ln_mix_g[layer], ln_kv_g, w_in_b[0], w_kv, tm_proj)
            o = _attention("sb", q, k_sh, v_sh, None, qm, memkv, batch, seq, n_mem, tq_attn)
        x2 = _out_ffn(x2, o, w_out[layer], ln_ffn_g[layer], w_up[layer], conv_w[layer], conv_b[layer],
                      w_down[layer], final_g, seq, tm_ffn, tf, final_norm=(layer == depth - 1))
    return x2.reshape(batch, seq, d)
```

```python
import functools

import jax
import jax.numpy as jnp
from jax import lax
from jax.experimental import pallas as pl
from jax.experimental.pallas import tpu as pltpu

HEAD_DIM = 64
N_MAIN_HEADS = 12
N_MEM_HEADS = 4
MAIN_WIDTH = N_MAIN_HEADS * HEAD_DIM
MEM_WIDTH = N_MEM_HEADS * HEAD_DIM
CONV_WIDTH = 3
EPS = 1e-6
Q_SCALE = HEAD_DIM ** -0.5

LANES = 128
SUBLANES = 8
PAIR_WIDTH = 2 * HEAD_DIM
VMEM_LIMIT = 56 * 1024 * 1024

F32 = jnp.float32
BF16 = jnp.bfloat16

_NT = (((1,), (1,)), ((), ()))


def _params(*sem):
    return pltpu.CompilerParams(dimension_semantics=sem, vmem_limit_bytes=VMEM_LIMIT)


def _dot(a, b):
    return jnp.dot(a, b, preferred_element_type=F32)


def _rms_normalize(x):
    return x * lax.rsqrt(jnp.mean(x * x, axis=-1, keepdims=True) + EPS)


def _log_sigmoid(x):
    return jnp.minimum(x, 0.0) - jnp.log1p(jnp.exp(-jnp.abs(x)))


def _low_head_mask():
    return lax.broadcasted_iota(jnp.int32, (1, PAIR_WIDTH), 1) < HEAD_DIM


def _store_chunks(h, w_ref, w_lo, out_ref, width, scale, chunk=256):
    for c in range(0, width, chunk):
        r = _dot(h, w_ref[:, w_lo + c:w_lo + c + chunk])
        if scale is not None:
            r = r * scale
        out_ref[:, c:c + chunk] = r.astype(out_ref.dtype)


def _proj_a_kernel(x_ref, g_ref, wqkv_ref, wqm_ref, wf_ref, bf_ref,
                   q_ref, k_ref, v_ref, qm_ref, ct_ref, carry_ref, *, tiles_per_seq, tk):
    tm = x_ref.shape[0]

    @pl.when(pl.program_id(0) % tiles_per_seq == 0)
    def _():
        carry_ref[...] = jnp.zeros_like(carry_ref)

    h = (_rms_normalize(x_ref[...]) * g_ref[...]).astype(BF16)
    _store_chunks(h, wqkv_ref, 0, q_ref, MAIN_WIDTH, Q_SCALE)
    _store_chunks(h, wqkv_ref, MAIN_WIDTH, k_ref, MAIN_WIDTH, None)
    _store_chunks(h, wqkv_ref, 2 * MAIN_WIDTH, v_ref, MAIN_WIDTH, None)
    _store_chunks(h, wqm_ref, 0, qm_ref, MEM_WIDTH, Q_SCALE)

    log_f = _log_sigmoid(_dot(h, wf_ref[...]) + bf_ref[...])
    hi = log_f.astype(BF16)
    r1 = log_f - hi.astype(F32)
    mid = r1.astype(BF16)
    lo = (r1 - mid.astype(F32)).astype(BF16)
    row = lax.broadcasted_iota(jnp.int32, (tm, tm), 0)
    col = lax.broadcasted_iota(jnp.int32, (tm, tm), 1)
    tri = jnp.where(row >= col, 1.0, 0.0).astype(BF16)
    c = (_dot(tri, hi) + _dot(tri, mid)) + _dot(tri, lo) + carry_ref[0:1, :]
    carry_ref[0:1, :] = c[tm - 1:tm, :]
    for r in range(tm // tk):
        ct_ref[0, r] = c[r * tk:(r + 1) * tk, :].T[:ct_ref.shape[2], :]


def _proj_b_kernel(x_ref, gmix_ref, gkv_ref, win_ref, wkv_ref, q_ref, qm_ref, k_ref, v_ref):
    xn = _rms_normalize(x_ref[...])
    h = (xn * gmix_ref[...]).astype(BF16)
    _store_chunks(h, win_ref, 0, q_ref, MAIN_WIDTH, Q_SCALE)
    _store_chunks(h, win_ref, MAIN_WIDTH, qm_ref, MEM_WIDTH, Q_SCALE)
    hk = (xn * gkv_ref[...]).astype(BF16)
    _store_chunks(hk, wkv_ref, 0, k_ref, MAIN_WIDTH, None)
    _store_chunks(hk, wkv_ref, MAIN_WIDTH, v_ref, MAIN_WIDTH, None)


def _norm_proj_kernel(x_ref, g_ref, w_ref, out_ref):
    h = (_rms_normalize(x_ref[...]) * g_ref[...]).astype(BF16)
    _store_chunks(h, w_ref, 0, out_ref, out_ref.shape[1], None)


def _row_spec(tm, width):
    return pl.BlockSpec((tm, width), lambda i: (i, 0))


def _full_spec(shape):
    return pl.BlockSpec(shape, lambda *_: (0,) * len(shape))


def _proj_a(x2, g, w_in, b_f, seq, tm, tk):
    t, d = x2.shape
    n_f = N_MAIN_HEADS
    wqkv = w_in[:, :3 * MAIN_WIDTH].astype(BF16)
    wf = jnp.pad(w_in[:, 3 * MAIN_WIDTH:3 * MAIN_WIDTH + n_f], ((0, 0), (0, LANES - n_f))).astype(BF16)
    wqm = w_in[:, 3 * MAIN_WIDTH + n_f:].astype(BF16)
    bf = jnp.pad(b_f, (0, LANES - n_f)).reshape(1, LANES)
    ct_rows = 2 * SUBLANES
    kernel = functools.partial(_proj_a_kernel, tiles_per_seq=seq // tm, tk=tk)
    return pl.pallas_call(
        kernel,
        grid=(t // tm,),
        in_specs=[_row_spec(tm, d), _full_spec((1, d)), _full_spec(wqkv.shape), _full_spec(wqm.shape),
                  _full_spec(wf.shape), _full_spec((1, LANES))],
        out_specs=[_row_spec(tm, MAIN_WIDTH), _row_spec(tm, MAIN_WIDTH), _row_spec(tm, MAIN_WIDTH),
                   _row_spec(tm, MEM_WIDTH),
                   pl.BlockSpec((1, tm // tk, ct_rows, tk), lambda i: (i, 0, 0, 0))],
        out_shape=[jax.ShapeDtypeStruct((t, MAIN_WIDTH), BF16)] * 3
        + [jax.ShapeDtypeStruct((t, MEM_WIDTH), BF16),
           jax.ShapeDtypeStruct((t // tm, tm // tk, ct_rows, tk), F32)],
        scratch_shapes=[pltpu.VMEM((SUBLANES, LANES), F32)],
        compiler_params=_params("arbitrary"),
        name="proj_a",
    )(x2, g.reshape(1, d), wqkv, wqm, wf, bf)


def _proj_b(x2, g_mix, g_kv, w_in, w_kv, tm):
    t, d = x2.shape
    win = w_in.astype(BF16)
    wkv = w_kv.astype(BF16)
    return pl.pallas_call(
        _proj_b_kernel,
        grid=(t // tm,),
        in_specs=[_row_spec(tm, d), _full_spec((1, d)), _full_spec((1, d)), _full_spec(win.shape),
                  _full_spec(wkv.shape)],
        out_specs=[_row_spec(tm, MAIN_WIDTH), _row_spec(tm, MEM_WIDTH), _row_spec(tm, MAIN_WIDTH),
                   _row_spec(tm, MAIN_WIDTH)],
        out_shape=[jax.ShapeDtypeStruct((t, MAIN_WIDTH), BF16), jax.ShapeDtypeStruct((t, MEM_WIDTH), BF16),
                   jax.ShapeDtypeStruct((t, MAIN_WIDTH), BF16), jax.ShapeDtypeStruct((t, MAIN_WIDTH), BF16)],
        compiler_params=_params("arbitrary"),
        name="proj_b",
    )(x2, g_mix.reshape(1, d), g_kv.reshape(1, d), win, wkv)


def _norm_proj(x2, g, w, tm):
    t, d = x2.shape
    wb = w.astype(BF16)
    n = wb.shape[1]
    return pl.pallas_call(
        _norm_proj_kernel,
        grid=(t // tm,),
        in_specs=[_row_spec(tm, d), _full_spec((1, d)), _full_spec(wb.shape)],
        out_specs=_row_spec(tm, n),
        out_shape=jax.ShapeDtypeStruct((t, n), BF16),
        compiler_params=_params("arbitrary"),
        name="mem_proj",
    )(x2, g.reshape(1, d), wb)


N_HEADS = N_MAIN_HEADS + N_MEM_HEADS


def _split_heads_into(q_ref, dst_ref, first, n_heads, low):
    for hp in range(n_heads // 2):
        q2 = q_ref[:, hp * PAIR_WIDTH:(hp + 1) * PAIR_WIDTH]
        zero = jnp.zeros_like(q2)
        dst_ref[first + 2 * hp] = jnp.where(low, q2, zero)
        dst_ref[first + 2 * hp + 1] = jnp.where(low, zero, q2)


def _values_with_ones(v2, sub, low):
    one = jnp.ones_like(v2)
    return jnp.where(low, v2, one) if sub == 0 else jnp.where(low, one, v2)


def _chunks(s):
    return [s[:, c:c + LANES] for c in range(0, s.shape[1], LANES)]


def _softmax_update(s_chunks, head, vh, m_ref, acc_ref):
    m_prev = m_ref[head]
    smax = functools.reduce(jnp.maximum, s_chunks)
    m_new = jnp.maximum(m_prev, jnp.max(smax, axis=-1, keepdims=True))
    p = jnp.concatenate([jnp.exp(s - m_new) for s in s_chunks], axis=-1).astype(BF16)
    acc_ref[head] = jnp.exp(m_prev - m_new) * acc_ref[head] + _dot(p, vh)
    m_ref[head] = m_new


def _softmax_finish(acc_ref, first, n_heads, o_ref, col0, low):
    for hp in range(n_heads // 2):
        outs = []
        for sub in range(2):
            acc = acc_ref[first + 2 * hp + sub]
            outs.append(acc / pltpu.roll(acc, HEAD_DIM, axis=1))
        o_ref[:, col0 + hp * PAIR_WIDTH:col0 + (hp + 1) * PAIR_WIDTH] = (
            jnp.where(low, outs[0], outs[1]).astype(o_ref.dtype))


def _memory_heads(qs_ref, mk_ref, mv_ref, m_ref, acc_ref, o_ref, low):
    for hp in range(N_MEM_HEADS // 2):
        cols = slice(hp * PAIR_WIDTH, (hp + 1) * PAIR_WIDTH)
        k2 = mk_ref[:, cols]
        v2 = mv_ref[:, cols]
        for sub in range(2):
            head = N_MAIN_HEADS + 2 * hp + sub
            s = lax.dot_general(qs_ref[head], k2, _NT, preferred_element_type=F32)
            _softmax_update(_chunks(s), head, _values_with_ones(v2, sub, low), m_ref, acc_ref)
    _softmax_finish(acc_ref, N_MAIN_HEADS, N_MEM_HEADS, o_ref, MAIN_WIDTH, low)


def _fox_attn_kernel(q_ref, k_ref, v_ref, ct_ref, qm_ref, mk_ref, mv_ref, o_ref, qs_ref, m_ref, acc_ref, *, tq):
    qi = pl.program_id(1)
    tk = tq
    low = _low_head_mask()
    _split_heads_into(q_ref, qs_ref, 0, N_MAIN_HEADS, low)
    _split_heads_into(qm_ref, qs_ref, N_MAIN_HEADS, N_MEM_HEADS, low)
    m_ref[...] = jnp.full_like(m_ref, -jnp.inf)
    acc_ref[...] = jnp.zeros_like(acc_ref)

    def block(kb, causal):
        ks = pl.multiple_of(kb * tk, tk)
        for hp in range(N_MAIN_HEADS // 2):
            cols = slice(hp * PAIR_WIDTH, (hp + 1) * PAIR_WIDTH)
            k2 = k_ref[pl.ds(ks, tk), cols]
            v2 = v_ref[pl.ds(ks, tk), cols]
            for sub in range(2):
                head = 2 * hp + sub
                s = lax.dot_general(qs_ref[head], k2, _NT, preferred_element_type=F32)
                s = s - ct_ref[0, kb, head:head + 1, :]
                if causal is not None:
                    s = jnp.where(causal, s, -jnp.inf)
                _softmax_update(_chunks(s), head, _values_with_ones(v2, sub, low), m_ref, acc_ref)

    def body(kb, carry):
        block(kb, None)
        return carry

    lax.fori_loop(0, qi, body, 0)
    row = lax.broadcasted_iota(jnp.int32, (tq, tk), 0)
    col = lax.broadcasted_iota(jnp.int32, (tq, tk), 1)
    block(qi, col <= row)
    _softmax_finish(acc_ref, 0, N_MAIN_HEADS, o_ref, 0, low)
    _memory_heads(qs_ref, mk_ref, mv_ref, m_ref, acc_ref, o_ref, low)


def _sb_attn_kernel(q_ref, k_ref, v_ref, qm_ref, mk_ref, mv_ref, o_ref, qs_ref, m_ref, acc_ref, run_ref, *, tq):
    qi = pl.program_id(1)
    tk = tq
    low = _low_head_mask()
    _split_heads_into(q_ref, qs_ref, 0, N_MAIN_HEADS, low)
    _split_heads_into(qm_ref, qs_ref, N_MAIN_HEADS, N_MEM_HEADS, low)
    m_ref[...] = jnp.full_like(m_ref, -jnp.inf)
    acc_ref[...] = jnp.zeros_like(acc_ref)
    run_ref[...] = jnp.zeros_like(run_ref)

    jj = lax.broadcasted_iota(jnp.int32, (2 * LANES, 2 * LANES), 0) % LANES
    ss = lax.broadcasted_iota(jnp.int32, (2 * LANES, 2 * LANES), 1)
    suffix = jnp.where((jj > ss) | (ss >= LANES), 1.0, 0.0).astype(BF16)

    def block(kb, causal):
        ks = pl.multiple_of(kb * tk, tk)
        for hp in range(N_MAIN_HEADS // 2):
            cols = slice(hp * PAIR_WIDTH, (hp + 1) * PAIR_WIDTH)
            k2 = k_ref[pl.ds(ks, tk), cols]
            v2 = v_ref[pl.ds(ks, tk), cols]
            for sub in range(2):
                head = 2 * hp + sub
                z = lax.dot_general(qs_ref[head], k2, _NT, preferred_element_type=F32)
                run = run_ref[head]
                a_chunks = [None] * (tk // LANES)
                for c in reversed(range(tk // LANES)):
                    zc = z[:, c * LANES:(c + 1) * LANES]
                    nl = jnp.maximum(zc, 0.0) + jnp.log(1.0 + jnp.exp(-jnp.abs(zc)))
                    if causal is not None:
                        nl = jnp.where(causal[c], nl, 0.0)
                    hi = nl.astype(BF16)
                    lo = (nl - hi.astype(F32)).astype(BF16)
                    r = _dot(jnp.concatenate([hi, lo], axis=1), suffix)
                    a = jnp.exp((zc - nl) - (r[:, :LANES] + run))
                    if causal is not None:
                        a = jnp.where(causal[c], a, 0.0)
                    a_chunks[c] = a.astype(BF16)
                    run = run + r[:, LANES:]
                run_ref[head] = run
                acc_ref[head] += _dot(jnp.concatenate(a_chunks, axis=1), v2)

    row = lax.broadcasted_iota(jnp.int32, (tq, LANES), 0)
    col = lax.broadcasted_iota(jnp.int32, (tq, LANES), 1)
    block(qi, [(col + c * LANES) < row for c in range(tk // LANES)])

    def body(it, carry):
        block(qi - 1 - it, None)
        return carry

    lax.fori_loop(0, qi, body, 0)
    for hp in range(N_MAIN_HEADS // 2):
        o_ref[:, hp * PAIR_WIDTH:(hp + 1) * PAIR_WIDTH] = (
            jnp.where(low, acc_ref[2 * hp], acc_ref[2 * hp + 1]).astype(o_ref.dtype))
    _memory_heads(qs_ref, mk_ref, mv_ref, m_ref, acc_ref, o_ref, low)


def _attention(kind, q, k, v, ct, qm, memkv, batch, seq, n_mem, tq):
    t = q.shape[0]
    nq = seq // tq
    q_spec = pl.BlockSpec((tq, MAIN_WIDTH), lambda b, i: (b * nq + i, 0))
    kv_spec = pl.BlockSpec((seq, MAIN_WIDTH), lambda b, i: (b, 0))
    qm_spec = pl.BlockSpec((tq, MEM_WIDTH), lambda b, i: (b * nq + i, 0))
    mk_spec = pl.BlockSpec((n_mem, MEM_WIDTH), lambda b, i: (b, 0))
    mv_spec = pl.BlockSpec((n_mem, MEM_WIDTH), lambda b, i: (b, 1))
    o_spec = pl.BlockSpec((tq, MAIN_WIDTH + MEM_WIDTH), lambda b, i: (b * nq + i, 0))
    scratch = [pltpu.VMEM((N_HEADS, tq, PAIR_WIDTH), BF16),
               pltpu.VMEM((N_HEADS, tq, LANES), F32),
               pltpu.VMEM((N_HEADS, tq, PAIR_WIDTH), F32)]
    if kind == "fox":
        tiles = seq // ct.shape[1] // ct.shape[3]
        ct = ct.reshape(batch, tiles * ct.shape[1], ct.shape[2], ct.shape[3])
        ct_spec = pl.BlockSpec((1,) + ct.shape[1:], lambda b, i: (b, 0, 0, 0))
        kernel = functools.partial(_fox_attn_kernel, tq=tq)
        in_specs = [q_spec, kv_spec, kv_spec, ct_spec, qm_spec, mk_spec, mv_spec]
        args = (q, k, v, ct, qm, memkv, memkv)
    else:
        kernel = functools.partial(_sb_attn_kernel, tq=tq)
        in_specs = [q_spec, kv_spec, kv_spec, qm_spec, mk_spec, mv_spec]
        args = (q, k, v, qm, memkv, memkv)
        scratch.append(pltpu.VMEM((N_MAIN_HEADS, tq, LANES), F32))
    return pl.pallas_call(
        kernel,
        grid=(batch, nq),
        in_specs=in_specs,
        out_specs=o_spec,
        out_shape=jax.ShapeDtypeStruct((t, MAIN_WIDTH + MEM_WIDTH), BF16),
        scratch_shapes=scratch,
        compiler_params=_params("arbitrary", "arbitrary"),
        name=kind + "_attn",
    )(*args)


def _causal_conv(u, prev, cw_ref, cb_ref):
    tm = u.shape[0]
    r8 = lax.broadcasted_iota(jnp.int32, (SUBLANES, 1), 0)
    out = cb_ref[...]
    for j in range(CONV_WIDTH - 1):
        shift = CONV_WIDTH - 1 - j
        rolled = pltpu.roll(u, shift, axis=0)
        top = jnp.where(r8 < shift, pltpu.roll(prev, shift, axis=0), rolled[:SUBLANES])
        shifted = jnp.concatenate([top, rolled[SUBLANES:]], axis=0)
        out = out + cw_ref[j:j + 1, :] * shifted
    return out + cw_ref[CONV_WIDTH - 1:CONV_WIDTH, :] * u


def _out_ffn_kernel(x_ref, o_ref, wout_ref, g_ref, wup_ref, cw_ref, cb_ref, wd_ref, fg_ref, out_ref,
                    act_ref, carry_ref, *, tiles_per_seq, tf, final_norm):
    tm = x_ref.shape[0]
    d_ff = wd_ref.shape[0]

    @pl.when(pl.program_id(0) % tiles_per_seq == 0)
    def _():
        carry_ref[...] = jnp.zeros_like(carry_ref)

    x1 = x_ref[...] + _dot(o_ref[...], wout_ref[...])
    h = (_rms_normalize(x1) * g_ref[...]).astype(BF16)
    for c in range(0, d_ff, tf):
        halves = []
        for half in range(2):
            cols = slice(half * d_ff + c, half * d_ff + c + tf)
            u = _dot(h, wup_ref[:, cols])
            prev = carry_ref[:, cols]
            carry_ref[:, cols] = u[tm - SUBLANES:]
            halves.append(_causal_conv(u, prev, cw_ref[:, cols], cb_ref[:, cols]))
        gate, val = halves
        act_ref[:, c:c + tf] = (gate * (1.0 / (1.0 + jnp.exp(-gate))) * val).astype(BF16)
    y = x1 + _dot(act_ref[...], wd_ref[...])
    if final_norm:
        y = _rms_normalize(y) * fg_ref[...]
    out_ref[...] = y


def _resident_spec(shape):
    return pl.BlockSpec(shape, lambda *_: (0,) * len(shape), pipeline_mode=pl.Buffered(1))


def _out_ffn(x2, o, w_out, g, w_up, conv_w, conv_b, w_down, final_g, seq, tm, tf, final_norm):
    t, d = x2.shape
    d_ff = w_down.shape[0]
    wout = w_out.astype(BF16)
    wup = w_up.astype(BF16)
    wdn = w_down.astype(BF16)
    cb = conv_b.reshape(1, 2 * d_ff)
    kernel = functools.partial(_out_ffn_kernel, tiles_per_seq=seq // tm, tf=tf, final_norm=final_norm)
    return pl.pallas_call(
        kernel,
        grid=(t // tm,),
        in_specs=[_row_spec(tm, d), _row_spec(tm, o.shape[1]),
                  _resident_spec(wout.shape), _resident_spec((1, d)), _resident_spec(wup.shape),
                  _resident_spec(conv_w.shape), _resident_spec(cb.shape), _resident_spec(wdn.shape),
                  _resident_spec((1, d))],
        out_specs=_row_spec(tm, d),
        out_shape=jax.ShapeDtypeStruct((t, d), F32),
        scratch_shapes=[pltpu.VMEM((tm, d_ff), BF16), pltpu.VMEM((SUBLANES, 2 * d_ff), F32)],
        compiler_params=_params("arbitrary"),
        name="out_ffn",
    )(x2, o, wout, g.reshape(1, d), wup, conv_w, cb, wdn, final_g.reshape(1, d))


def kernel(x, mem, ln_mix_g, w_in_a, b_f_a, w_in_b, ln_kv_g, w_kv, ln_mem_g, w_memkv, w_out, ln_ffn_g,
           w_up, conv_w, conv_b, w_down, final_g):
    batch, seq, d = x.shape
    n_mem = mem.shape[1]
    depth = ln_mix_g.shape[0]
    n_a = w_in_a.shape[0]
    assert w_in_b.shape[0] == 1, "one stick-breaking layer reads the shared K/V"
    tm_proj = min(512, seq)
    tq_attn = min(256, seq)
    tm_ffn, tf = min(512, seq), 256

    x2 = x.reshape(batch * seq, d)
    mem2 = mem.reshape(batch * n_mem, d)
    for layer in range(depth):
        memkv = _norm_proj(mem2, ln_mem_g[layer], w_memkv[layer], min(512, batch * n_mem))
        if layer < n_a:
            q, k, v, qm, ct = _proj_a(x2, ln_mix_g[layer], w_in_a[layer], b_f_a[layer], seq, tm_proj, tq_attn)
            o = _attention("fox", q, k, v, ct, qm, memkv, batch, seq, n_mem, tq_attn)
        else:
            q, qm, k_sh, v_sh = _proj_b(x2, ln_mix_g[layer], ln_kv_g, w_in_b[0], w_kv, tm_proj)
            o = _attention("sb", q, k_sh, v_sh, None, qm, memkv, batch, seq, n_mem, tq_attn)
        x2 = _out_ffn(x2, o, w_out[layer], ln_ffn_g[layer], w_up[layer], conv_w[layer], conv_b[layer],
                      w_down[layer], final_g, seq, tm_ffn, tf, final_norm=(layer == depth - 1))
    return x2.reshape(batch, seq, d)
```

```python
import functools

import jax
import jax.numpy as jnp
from jax import lax
from jax.experimental import pallas as pl
from jax.experimental.pallas import tpu as pltpu

HEAD_DIM = 64
N_MAIN_HEADS = 12
N_MEM_HEADS = 4
N_HEADS = N_MAIN_HEADS + N_MEM_HEADS
MAIN_WIDTH = N_MAIN_HEADS * HEAD_DIM
MEM_WIDTH = N_MEM_HEADS * HEAD_DIM
CONV_WIDTH = 3
EPS = 1e-6
Q_SCALE = HEAD_DIM ** -0.5
LOG2_E = 1.4426950408889634

LANES = 128
SUBLANES = 8
BF16_ROWS = 16
PAIR_WIDTH = 2 * HEAD_DIM
GATE_TERMS = 3
VMEM_LIMIT = 56 * 1024 * 1024

F32 = jnp.float32
BF16 = jnp.bfloat16

_NT = (((1,), (1,)), ((), ()))


def _params(*sem, flags=None):
    return pltpu.CompilerParams(dimension_semantics=sem, vmem_limit_bytes=VMEM_LIMIT, flags=flags)


def _dot(a, b):
    return jnp.dot(a, b, preferred_element_type=F32)


def _dot_nt(a, b):
    return lax.dot_general(a, b, _NT, preferred_element_type=F32)


def _rms_normalize(x):
    return x * lax.rsqrt(jnp.mean(x * x, axis=-1, keepdims=True) + EPS)


def _log_sigmoid(x):
    return jnp.minimum(x, 0.0) - jnp.log1p(jnp.exp(-jnp.abs(x)))


def _split3(x):
    hi = x.astype(BF16)
    r1 = x - hi.astype(F32)
    mid = r1.astype(BF16)
    return hi, mid, (r1 - mid.astype(F32)).astype(BF16)


def _store_chunks(h, w_ref, w_lo, out_ref, width, scale, chunk=256):
    for c in range(0, width, chunk):
        r = _dot(h, w_ref[:, w_lo + c:w_lo + c + chunk])
        if scale is not None:
            r = r * scale
        out_ref[:, c:c + chunk] = r.astype(out_ref.dtype)


def _store_transposed(h, w_ref, w_lo, out_ref, width, tk, chunk=256):
    tm = h.shape[0]
    for c in range(0, width, chunk):
        v = _dot(h, w_ref[:, w_lo + c:w_lo + c + chunk])
        for r in range(tm // tk):
            out_ref[0, r, c:c + chunk, :] = v[r * tk:(r + 1) * tk, :].T.astype(out_ref.dtype)


def _proj_a_kernel(x_ref, g_ref, wqkv_ref, wqm_ref, wf_ref, bf_ref,
                   q_ref, k_ref, vt_ref, qm_ref, cp_ref, carry_ref, *, tiles_per_seq, tk):
    tm = x_ref.shape[0]

    @pl.when(pl.program_id(0) % tiles_per_seq == 0)
    def _():
        carry_ref[...] = jnp.zeros_like(carry_ref)

    h = (_rms_normalize(x_ref[...]) * g_ref[...]).astype(BF16)
    _store_chunks(h, wqkv_ref, 0, q_ref, MAIN_WIDTH, Q_SCALE)
    _store_chunks(h, wqkv_ref, MAIN_WIDTH, k_ref, MAIN_WIDTH, None)
    _store_transposed(h, wqkv_ref, 2 * MAIN_WIDTH, vt_ref, MAIN_WIDTH, tk)
    _store_chunks(h, wqm_ref, 0, qm_ref, MEM_WIDTH, Q_SCALE)

    log_f = _log_sigmoid(_dot(h, wf_ref[...]) + bf_ref[...])
    hi, mid, lo = _split3(log_f)
    row = lax.broadcasted_iota(jnp.int32, (tm, tm), 0)
    col = lax.broadcasted_iota(jnp.int32, (tm, tm), 1)
    tri = jnp.where(row >= col, 1.0, 0.0).astype(BF16)
    c = (_dot(tri, hi) + _dot(tri, mid)) + _dot(tri, lo) + carry_ref[0:1, :]
    carry_ref[0:1, :] = c[tm - 1:tm, :]
    rr = lax.broadcasted_iota(jnp.int32, (GATE_TERMS * LANES, LANES), 0)
    cc = lax.broadcasted_iota(jnp.int32, (GATE_TERMS * LANES, LANES), 1)
    head, term = rr & (LANES - 1), rr >> 7
    place = jnp.where((cc == GATE_TERMS * head + term) & (head < N_MAIN_HEADS), 1.0, 0.0).astype(BF16)
    cp_ref[...] = _dot(jnp.concatenate(_split3(-c), axis=1), place).astype(cp_ref.dtype)


def _proj_b_kernel(x_ref, gmix_ref, gkv_ref, win_ref, wkv_ref, q_ref, qm_ref, k_ref, vt_ref, *, tk):
    xn = _rms_normalize(x_ref[...])
    h = (xn * gmix_ref[...]).astype(BF16)
    _store_chunks(h, win_ref, 0, q_ref, MAIN_WIDTH, Q_SCALE)
    _store_chunks(h, win_ref, MAIN_WIDTH, qm_ref, MEM_WIDTH, Q_SCALE)
    hk = (xn * gkv_ref[...]).astype(BF16)
    _store_chunks(hk, wkv_ref, 0, k_ref, MAIN_WIDTH, None)
    _store_transposed(hk, wkv_ref, MAIN_WIDTH, vt_ref, MAIN_WIDTH, tk)


def _mem_proj_kernel(x_ref, g_ref, w_ref, mk_ref, mvt_ref):
    n_mem = x_ref.shape[0]
    h = (_rms_normalize(x_ref[...]) * g_ref[...]).astype(BF16)
    _store_chunks(h, w_ref, 0, mk_ref, MEM_WIDTH, None)
    _store_transposed(h, w_ref, MEM_WIDTH, mvt_ref, MEM_WIDTH, n_mem)


def _row_spec(tm, width):
    return pl.BlockSpec((tm, width), lambda i: (i, 0))


def _full_spec(shape):
    return pl.BlockSpec(shape, lambda *_: (0,) * len(shape))


def _vt_out(t, tm, tk):
    shape = (t // tm, tm // tk, MAIN_WIDTH, tk)
    return pl.BlockSpec((1,) + shape[1:], lambda i: (i, 0, 0, 0)), jax.ShapeDtypeStruct(shape, BF16)


def _proj_a(x2, g, w_in, b_f, seq, tm, tk):
    t, d = x2.shape
    n_f = N_MAIN_HEADS
    wqkv = w_in[:, :3 * MAIN_WIDTH].astype(BF16)
    wf = jnp.pad(w_in[:, 3 * MAIN_WIDTH:3 * MAIN_WIDTH + n_f], ((0, 0), (0, LANES - n_f))).astype(BF16)
    wqm = w_in[:, 3 * MAIN_WIDTH + n_f:].astype(BF16)
    bf = jnp.pad(b_f, (0, LANES - n_f)).reshape(1, LANES)
    vt_spec, vt_shape = _vt_out(t, tm, tk)
    kernel = functools.partial(_proj_a_kernel, tiles_per_seq=seq // tm, tk=tk)
    return pl.pallas_call(
        kernel,
        grid=(t // tm,),
        in_specs=[_row_spec(tm, d), _full_spec((1, d)), _full_spec(wqkv.shape), _full_spec(wqm.shape),
                  _full_spec(wf.shape), _full_spec((1, LANES))],
        out_specs=[_row_spec(tm, MAIN_WIDTH), _row_spec(tm, MAIN_WIDTH), vt_spec,
                   _row_spec(tm, MEM_WIDTH), _row_spec(tm, LANES)],
        out_shape=[jax.ShapeDtypeStruct((t, MAIN_WIDTH), BF16), jax.ShapeDtypeStruct((t, MAIN_WIDTH), BF16),
                   vt_shape, jax.ShapeDtypeStruct((t, MEM_WIDTH), BF16), jax.ShapeDtypeStruct((t, LANES), BF16)],
        scratch_shapes=[pltpu.VMEM((SUBLANES, LANES), F32)],
        compiler_params=_params("arbitrary"),
        name="proj_a",
    )(x2, g.reshape(1, d), wqkv, wqm, wf, bf)


def _proj_b(x2, g_mix, g_kv, w_in, w_kv, tm, tk):
    t, d = x2.shape
    win = w_in.astype(BF16)
    wkv = w_kv.astype(BF16)
    vt_spec, vt_shape = _vt_out(t, tm, tk)
    return pl.pallas_call(
        functools.partial(_proj_b_kernel, tk=tk),
        grid=(t // tm,),
        in_specs=[_row_spec(tm, d), _full_spec((1, d)), _full_spec((1, d)), _full_spec(win.shape),
                  _full_spec(wkv.shape)],
        out_specs=[_row_spec(tm, MAIN_WIDTH), _row_spec(tm, MEM_WIDTH), _row_spec(tm, MAIN_WIDTH), vt_spec],
        out_shape=[jax.ShapeDtypeStruct((t, MAIN_WIDTH), BF16), jax.ShapeDtypeStruct((t, MEM_WIDTH), BF16),
                   jax.ShapeDtypeStruct((t, MAIN_WIDTH), BF16), vt_shape],
        compiler_params=_params("arbitrary"),
        name="proj_b",
    )(x2, g_mix.reshape(1, d), g_kv.reshape(1, d), win, wkv)


def _mem_proj(mem2, g, w, batch, n_mem):
    d = mem2.shape[1]
    wb = w.astype(BF16)
    return pl.pallas_call(
        _mem_proj_kernel,
        grid=(batch,),
        in_specs=[_row_spec(n_mem, d), _full_spec((1, d)), _full_spec(wb.shape)],
        out_specs=[_row_spec(n_mem, MEM_WIDTH), pl.BlockSpec((1, 1, MEM_WIDTH, n_mem), lambda i: (i, 0, 0, 0))],
        out_shape=[jax.ShapeDtypeStruct((batch * n_mem, MEM_WIDTH), BF16),
                   jax.ShapeDtypeStruct((batch, 1, MEM_WIDTH, n_mem), BF16)],
        compiler_params=_params("arbitrary"),
        name="mem_proj",
    )(mem2, g.reshape(1, d), wb)


ACC_ROWS = HEAD_DIM + BF16_ROWS
ATTN_FLAGS = None
LOOKAHEAD = 6
SCORE_AHEAD = 3
SUFFIX_AHEAD = 3


def _split_heads_into(q_ref, dst_ref, first, n_heads, gate_rows):
    tq = q_ref.shape[0]
    row = lax.broadcasted_iota(jnp.int32, (PAIR_WIDTH, 1), 0)
    low = row < HEAD_DIM
    for hp in range(n_heads // 2):
        qt = q_ref[:, hp * PAIR_WIDTH:(hp + 1) * PAIR_WIDTH].astype(F32).T
        for sub in range(2):
            head = first + 2 * hp + sub
            keep = low if sub == 0 else jnp.logical_not(low)
            dst_ref[head, 0:PAIR_WIDTH, :] = jnp.where(keep, qt, 0.0).astype(dst_ref.dtype)
            if gate_rows:
                pick = (row >= GATE_TERMS * head) & (row < GATE_TERMS * (head + 1))
                dst_ref[head, PAIR_WIDTH:, :] = jnp.broadcast_to(
                    jnp.where(pick, 1.0, 0.0).astype(dst_ref.dtype), (PAIR_WIDTH, tq))


def _values_with_ones(vt2, sub):
    vt = vt2[sub * HEAD_DIM:(sub + 1) * HEAD_DIM]
    return jnp.concatenate([vt, jnp.ones((BF16_ROWS, vt.shape[1]), vt.dtype)], axis=0)


def _softmax_update(s, head, vt_h, m_ref, acc_ref):
    m_prev = m_ref[head][0:1]
    m_new = jnp.maximum(m_prev, jnp.max(s, axis=0, keepdims=True))
    p = jnp.exp(s - m_new).astype(BF16)
    acc_ref[head] = jnp.exp(m_prev - m_new) * acc_ref[head] + _dot(vt_h, p)
    m_ref[head] = jnp.broadcast_to(m_new, m_ref.shape[1:])


def _softmax_finish(acc_ref, first, n_heads, o_ref, col0):
    for hp in range(n_heads // 2):
        outs = []
        for sub in range(2):
            acc = acc_ref[first + 2 * hp + sub]
            outs.append(acc[:HEAD_DIM] / acc[HEAD_DIM:HEAD_DIM + 1])
        o_ref[:, col0 + hp * PAIR_WIDTH:col0 + (hp + 1) * PAIR_WIDTH] = (
            jnp.concatenate(outs, axis=0).T.astype(o_ref.dtype))


def _init_softmax_state(m_ref, acc_ref):
    m_ref[...] = jnp.full_like(m_ref, -jnp.inf)
    acc_ref[...] = jnp.zeros_like(acc_ref)


def _memory_heads(qs_ref, mk_ref, mvt_ref, m_ref, acc_ref, o_ref):
    def pair_cols(i):
        return slice((i // 2) * PAIR_WIDTH, (i // 2 + 1) * PAIR_WIDTH)

    scores = [_dot(mk_ref[:, pair_cols(i)], qs_ref[N_MAIN_HEADS + i, 0:PAIR_WIDTH, :])
              for i in range(N_MEM_HEADS)]
    for i, s in enumerate(scores):
        vt_h = _values_with_ones(mvt_ref[0, 0, pair_cols(i), :], i % 2)
        _softmax_update(s, N_MAIN_HEADS + i, vt_h, m_ref, acc_ref)
    _softmax_finish(acc_ref, N_MAIN_HEADS, N_MEM_HEADS, o_ref, MAIN_WIDTH)


def _fox_attn_kernel(q_ref, k_ref, cp_ref, vt_ref, qm_ref, mk_ref, mvt_ref, o_ref, qs_ref, m_ref, acc_ref, *, tq):
    qi = pl.program_id(1)
    tk = tq
    _split_heads_into(q_ref, qs_ref, 0, N_MAIN_HEADS, True)
    _split_heads_into(qm_ref, qs_ref, N_MAIN_HEADS, N_MEM_HEADS, False)
    _init_softmax_state(m_ref, acc_ref)

    def block(kb, causal):
        ks = pl.multiple_of(kb * tk, tk)
        cp = cp_ref[pl.ds(ks, tk), :]

        def scores(head):
            cols = slice((head // 2) * PAIR_WIDTH, (head // 2 + 1) * PAIR_WIDTH)
            kk = jnp.concatenate([k_ref[pl.ds(ks, tk), cols], cp], axis=1)
            s = _dot(kk, qs_ref[head])
            return s if causal is None else jnp.where(causal, s, -jnp.inf)

        pending = [scores(h) for h in range(LOOKAHEAD)]
        for head in range(N_MAIN_HEADS):
            s = pending.pop(0)
            if head + LOOKAHEAD < N_MAIN_HEADS:
                pending.append(scores(head + LOOKAHEAD))
            cols = slice((head // 2) * PAIR_WIDTH, (head // 2 + 1) * PAIR_WIDTH)
            _softmax_update(s, head, _values_with_ones(vt_ref[0, kb, cols, :], head % 2), m_ref, acc_ref)

    def body(kb, carry):
        block(kb, None)
        return carry

    lax.fori_loop(0, qi, body, 0)
    key = lax.broadcasted_iota(jnp.int32, (tk, tq), 0)
    qry = lax.broadcasted_iota(jnp.int32, (tk, tq), 1)
    block(qi, key <= qry)
    _softmax_finish(acc_ref, 0, N_MAIN_HEADS, o_ref, 0)
    _memory_heads(qs_ref, mk_ref, mvt_ref, m_ref, acc_ref, o_ref)


def _sb_attn_kernel(q_ref, k_ref, vt_ref, qm_ref, mk_ref, mvt_ref, o_ref, qs_ref, m_ref, acc_ref, run_ref, *, tq):
    qi = pl.program_id(1)
    tk = tq
    _split_heads_into(q_ref, qs_ref, 0, N_MAIN_HEADS, False)
    _split_heads_into(qm_ref, qs_ref, N_MAIN_HEADS, N_MEM_HEADS, False)
    _init_softmax_state(m_ref, acc_ref)
    run_ref[...] = jnp.zeros_like(run_ref)

    ss = lax.broadcasted_iota(jnp.int32, (LANES + BF16_ROWS, 2 * LANES), 0)
    jj = lax.broadcasted_iota(jnp.int32, (LANES + BF16_ROWS, 2 * LANES), 1) & (LANES - 1)
    suffix = jnp.where((jj > ss) | (ss >= LANES), 1.0, 0.0).astype(BF16)

    def block(kb, causal):
        ks = pl.multiple_of(kb * tk, tk)
        n_chunks = tk // LANES

        def pair_cols(head):
            return slice((head // 2) * PAIR_WIDTH, (head // 2 + 1) * PAIR_WIDTH)

        def score_stage(head):
            z = _dot(k_ref[pl.ds(ks, tk), pair_cols(head)], qs_ref[head])
            log_beta, split = [], []
            for c in range(n_chunks):
                zc = z[c * LANES:(c + 1) * LANES]
                nl = jnp.maximum(zc, 0.0) + jnp.log(1.0 + jnp.exp2(jnp.abs(zc) * -LOG2_E))
                log_beta.append(zc - nl)
                if causal is not None:
                    nl = jnp.where(causal[c], nl, 0.0)
                hi = nl.astype(BF16)
                split.append(jnp.concatenate([hi, (nl - hi.astype(F32)).astype(BF16)], axis=0))
            return log_beta, split

        def suffix_stage(state):
            log_beta, split = state
            return log_beta, [_dot(suffix, sp) for sp in split]

        def value_stage(head, state):
            log_beta, sums = state
            run = run_ref[head][0:1]
            a_chunks = [None] * n_chunks
            for c in reversed(range(n_chunks)):
                a = jnp.exp(log_beta[c] - (sums[c][:LANES] + run))
                if causal is not None:
                    a = jnp.where(causal[c], a, 0.0)
                a_chunks[c] = a.astype(BF16)
                run = run + sums[c][LANES:LANES + 1]
            run_ref[head] = jnp.broadcast_to(run, run_ref.shape[1:])
            vt = vt_ref[0, kb, pair_cols(head), :][(head % 2) * HEAD_DIM:(head % 2 + 1) * HEAD_DIM]
            acc_ref[head, 0:HEAD_DIM] += _dot(vt, jnp.concatenate(a_chunks, axis=0))

        scored, summed = {}, {}
        for step in range(N_MAIN_HEADS + SCORE_AHEAD + SUFFIX_AHEAD):
            if step < N_MAIN_HEADS:
                scored[step] = score_stage(step)
            h2 = step - SCORE_AHEAD
            if 0 <= h2 < N_MAIN_HEADS:
                summed[h2] = suffix_stage(scored.pop(h2))
            h3 = h2 - SUFFIX_AHEAD
            if 0 <= h3 < N_MAIN_HEADS:
                value_stage(h3, summed.pop(h3))

    key = lax.broadcasted_iota(jnp.int32, (LANES, tq), 0)
    qry = lax.broadcasted_iota(jnp.int32, (LANES, tq), 1)
    block(qi, [(key + c * LANES) < qry for c in range(tk // LANES)])

    def body(it, carry):
        block(qi - 1 - it, None)
        return carry

    lax.fori_loop(0, qi, body, 0)
    for hp in range(N_MAIN_HEADS // 2):
        pair = jnp.concatenate([acc_ref[2 * hp, 0:HEAD_DIM], acc_ref[2 * hp + 1, 0:HEAD_DIM]], axis=0)
        o_ref[:, hp * PAIR_WIDTH:(hp + 1) * PAIR_WIDTH] = pair.T.astype(o_ref.dtype)
    _memory_heads(qs_ref, mk_ref, mvt_ref, m_ref, acc_ref, o_ref)


def _attention(kind, q, k, cp, vt, qm, mk, mvt, batch, seq, n_mem, tq):
    t = q.shape[0]
    nq = seq // tq
    vt = vt.reshape(batch, seq // vt.shape[3], MAIN_WIDTH, vt.shape[3])
    q_spec = pl.BlockSpec((tq, MAIN_WIDTH), lambda b, i: (b * nq + i, 0))
    k_spec = pl.BlockSpec((seq, MAIN_WIDTH), lambda b, i: (b, 0))
    vt_spec = pl.BlockSpec((1,) + vt.shape[1:], lambda b, i: (b, 0, 0, 0))
    qm_spec = pl.BlockSpec((tq, MEM_WIDTH), lambda b, i: (b * nq + i, 0))
    mk_spec = pl.BlockSpec((n_mem, MEM_WIDTH), lambda b, i: (b, 0))
    mvt_spec = pl.BlockSpec((1, 1, MEM_WIDTH, n_mem), lambda b, i: (b, 0, 0, 0))
    o_spec = pl.BlockSpec((tq, MAIN_WIDTH + MEM_WIDTH), lambda b, i: (b * nq + i, 0))
    q_rows = 2 * PAIR_WIDTH if kind == "fox" else PAIR_WIDTH
    scratch = [pltpu.VMEM((N_HEADS, q_rows, tq), BF16),
               pltpu.VMEM((N_HEADS, SUBLANES, tq), F32),
               pltpu.VMEM((N_HEADS, ACC_ROWS, tq), F32)]
    if kind == "fox":
        cp_spec = pl.BlockSpec((seq, LANES), lambda b, i: (b, 0))
        kernel = functools.partial(_fox_attn_kernel, tq=tq)
        in_specs = [q_spec, k_spec, cp_spec, vt_spec, qm_spec, mk_spec, mvt_spec]
        args = (q, k, cp, vt, qm, mk, mvt)
    else:
        kernel = functools.partial(_sb_attn_kernel, tq=tq)
        in_specs = [q_spec, k_spec, vt_spec, qm_spec, mk_spec, mvt_spec]
        args = (q, k, vt, qm, mk, mvt)
        scratch.append(pltpu.VMEM((N_MAIN_HEADS, SUBLANES, tq), F32))
    return pl.pallas_call(
        kernel,
        grid=(batch, nq),
        in_specs=in_specs,
        out_specs=o_spec,
        out_shape=jax.ShapeDtypeStruct((t, MAIN_WIDTH + MEM_WIDTH), BF16),
        scratch_shapes=scratch,
        compiler_params=_params("arbitrary", "arbitrary", flags=ATTN_FLAGS),
        name=kind + "_attn",
    )(*args)


def _causal_conv(u, prev, cw, cb):
    r8 = lax.broadcasted_iota(jnp.int32, (SUBLANES, 1), 0)
    out = cb
    for j in range(CONV_WIDTH - 1):
        shift = CONV_WIDTH - 1 - j
        rolled = pltpu.roll(u, shift, axis=0)
        top = jnp.where(r8 < shift, pltpu.roll(prev, shift, axis=0), rolled[:SUBLANES])
        shifted = jnp.concatenate([top, rolled[SUBLANES:]], axis=0)
        out = out + cw[j:j + 1, :] * shifted
    return out + cw[CONV_WIDTH - 1:CONV_WIDTH, :] * u


def _out_ffn_kernel(x_ref, o_ref, wout_ref, g_ref, wup_ref, cw_ref, cb_ref, wd_ref, fg_ref, out_ref,
                    act_ref, carry_ref, *, tiles_per_seq, tf, final_norm):
    tm = x_ref.shape[0]
    d_ff = wd_ref.shape[0]

    @pl.when(pl.program_id(0) % tiles_per_seq == 0)
    def _():
        carry_ref[...] = jnp.zeros_like(carry_ref)

    x1 = x_ref[...] + _dot(o_ref[...], wout_ref[...])
    h = (_rms_normalize(x1) * g_ref[...]).astype(BF16)
    for c in range(0, d_ff, tf):
        halves = []
        for half in range(2):
            cols = slice(half * d_ff + c, half * d_ff + c + tf)
            u = _dot(h, wup_ref[:, cols])
            prev = carry_ref[:, cols]
            carry_ref[:, cols] = u[tm - SUBLANES:]
            halves.append(_causal_conv(u, prev, cw_ref[:, cols], cb_ref[:, cols]))
        gate, val = halves
        act_ref[:, c:c + tf] = (gate * (1.0 / (1.0 + jnp.exp(-gate))) * val).astype(BF16)
    y = x1 + _dot(act_ref[...], wd_ref[...])
    if final_norm:
        y = _rms_normalize(y) * fg_ref[...]
    out_ref[...] = y


def _resident_spec(shape):
    return pl.BlockSpec(shape, lambda *_: (0,) * len(shape), pipeline_mode=pl.Buffered(1))


def _out_ffn(x2, o, w_out, g, w_up, conv_w, conv_b, w_down, final_g, seq, tm, tf, final_norm):
    t, d = x2.shape
    d_ff = w_down.shape[0]
    wout = w_out.astype(BF16)
    wup = w_up.astype(BF16)
    wdn = w_down.astype(BF16)
    cb = conv_b.reshape(1, 2 * d_ff)
    kernel = functools.partial(_out_ffn_kernel, tiles_per_seq=seq // tm, tf=tf, final_norm=final_norm)
    return pl.pallas_call(
        kernel,
        grid=(t // tm,),
        in_specs=[_row_spec(tm, d), _row_spec(tm, o.shape[1]),
                  _resident_spec(wout.shape), _resident_spec((1, d)), _resident_spec(wup.shape),
                  _resident_spec(conv_w.shape), _resident_spec(cb.shape), _resident_spec(wdn.shape),
                  _resident_spec((1, d))],
        out_specs=_row_spec(tm, d),
        out_shape=jax.ShapeDtypeStruct((t, d), F32),
        scratch_shapes=[pltpu.VMEM((tm, d_ff), BF16), pltpu.VMEM((SUBLANES, 2 * d_ff), F32)],
        compiler_params=_params("arbitrary"),
        name="out_ffn",
    )(x2, o, wout, g.reshape(1, d), wup, conv_w, cb, wdn, final_g.reshape(1, d))


def kernel(x, mem, ln_mix_g, w_in_a, b_f_a, w_in_b, ln_kv_g, w_kv, ln_mem_g, w_memkv, w_out, ln_ffn_g,
           w_up, conv_w, conv_b, w_down, final_g):
    batch, seq, d = x.shape
    n_mem = mem.shape[1]
    depth = ln_mix_g.shape[0]
    n_a = w_in_a.shape[0]
    assert w_in_b.shape[0] == 1, "one stick-breaking layer reads the shared K/V"
    tm_proj = min(512, seq)
    tq_attn = min(256, seq)
    tm_ffn, tf = min(512, seq), 256

    x2 = x.reshape(batch * seq, d)
    mem2 = mem.reshape(batch * n_mem, d)
    for layer in range(depth):
        mk, mvt = _mem_proj(mem2, ln_mem_g[layer], w_memkv[layer], batch, n_mem)
        if layer < n_a:
            q, k, vt, qm, cp = _proj_a(x2, ln_mix_g[layer], w_in_a[layer], b_f_a[layer], seq, tm_proj, tq_attn)
            o = _attention("fox", q, k, cp, vt, qm, mk, mvt, batch, seq, n_mem, tq_attn)
        else:
            q, qm, k_sh, vt_sh = _proj_b(x2, ln_mix_g[layer], ln_kv_g, w_in_b[0], w_kv, tm_proj, tq_attn)
            o = _attention("sb", q, k_sh, None, vt_sh, qm, mk, mvt, batch, seq, n_mem, tq_attn)
        x2 = _out_ffn(x2, o, w_out[layer], ln_ffn_g[layer], w_up[layer], conv_w[layer], conv_b[layer],
                      w_down[layer], final_g, seq, tm_ffn, tf, final_norm=(layer == depth - 1))
    return x2.reshape(batch, seq, d)
```

```python
import functools

import jax
import jax.numpy as jnp
from jax import lax
from jax.experimental import pallas as pl
from jax.experimental.pallas import tpu as pltpu

HEAD_DIM = 64
N_MAIN_HEADS = 12
N_MEM_HEADS = 4
N_HEADS = N_MAIN_HEADS + N_MEM_HEADS
MAIN_WIDTH = N_MAIN_HEADS * HEAD_DIM
MEM_WIDTH = N_MEM_HEADS * HEAD_DIM
CONV_WIDTH = 3
EPS = 1e-6
Q_SCALE = HEAD_DIM ** -0.5
LOG2_E = 1.4426950408889634

LANES = 128
SUBLANES = 8
BF16_ROWS = 16
PAIR_WIDTH = 2 * HEAD_DIM
GATE_TERMS = 3
VMEM_LIMIT = 56 * 1024 * 1024

F32 = jnp.float32
BF16 = jnp.bfloat16

_NT = (((1,), (1,)), ((), ()))


def _params(*sem, flags=None):
    return pltpu.CompilerParams(dimension_semantics=sem, vmem_limit_bytes=VMEM_LIMIT, flags=flags)


def _dot(a, b):
    return jnp.dot(a, b, preferred_element_type=F32)


def _dot_nt(a, b):
    return lax.dot_general(a, b, _NT, preferred_element_type=F32)


def _rms_normalize(x):
    return x * lax.rsqrt(jnp.mean(x * x, axis=-1, keepdims=True) + EPS)


def _log_sigmoid(x):
    return jnp.minimum(x, 0.0) - jnp.log1p(jnp.exp(-jnp.abs(x)))


def _split3(x):
    hi = x.astype(BF16)
    r1 = x - hi.astype(F32)
    mid = r1.astype(BF16)
    return hi, mid, (r1 - mid.astype(F32)).astype(BF16)


def _store_chunks(h, w_ref, w_lo, out_ref, width, scale, chunk=256):
    for c in range(0, width, chunk):
        r = _dot(h, w_ref[:, w_lo + c:w_lo + c + chunk])
        if scale is not None:
            r = r * scale
        out_ref[:, c:c + chunk] = r.astype(out_ref.dtype)


def _store_transposed(h, w_ref, w_lo, out_ref, width, tk, chunk=256):
    tm = h.shape[0]
    for c in range(0, width, chunk):
        v = _dot(h, w_ref[:, w_lo + c:w_lo + c + chunk])
        for r in range(tm // tk):
            out_ref[0, r, c:c + chunk, :] = v[r * tk:(r + 1) * tk, :].T.astype(out_ref.dtype)


def _proj_a_kernel(x_ref, g_ref, wqkv_ref, wqm_ref, wf_ref, bf_ref,
                   q_ref, k_ref, vt_ref, qm_ref, cp_ref, carry_ref, *, tiles_per_seq, tk):
    tm = x_ref.shape[0]

    @pl.when(pl.program_id(0) % tiles_per_seq == 0)
    def _():
        carry_ref[...] = jnp.zeros_like(carry_ref)

    h = (_rms_normalize(x_ref[...]) * g_ref[...]).astype(BF16)
    _store_chunks(h, wqkv_ref, 0, q_ref, MAIN_WIDTH, Q_SCALE)
    _store_chunks(h, wqkv_ref, MAIN_WIDTH, k_ref, MAIN_WIDTH, None)
    _store_transposed(h, wqkv_ref, 2 * MAIN_WIDTH, vt_ref, MAIN_WIDTH, tk)
    _store_chunks(h, wqm_ref, 0, qm_ref, MEM_WIDTH, Q_SCALE)

    log_f = _log_sigmoid(_dot(h, wf_ref[...]) + bf_ref[...])
    hi, mid, lo = _split3(log_f)
    row = lax.broadcasted_iota(jnp.int32, (tm, tm), 0)
    col = lax.broadcasted_iota(jnp.int32, (tm, tm), 1)
    tri = jnp.where(row >= col, 1.0, 0.0).astype(BF16)
    c = (_dot(tri, hi) + _dot(tri, mid)) + _dot(tri, lo) + carry_ref[0:1, :]
    carry_ref[0:1, :] = c[tm - 1:tm, :]
    rr = lax.broadcasted_iota(jnp.int32, (GATE_TERMS * LANES, LANES), 0)
    cc = lax.broadcasted_iota(jnp.int32, (GATE_TERMS * LANES, LANES), 1)
    head, term = rr & (LANES - 1), rr >> 7
    place = jnp.where((cc == GATE_TERMS * head + term) & (head < N_MAIN_HEADS), 1.0, 0.0).astype(BF16)
    cp_ref[...] = _dot(jnp.concatenate(_split3(-c), axis=1), place).astype(cp_ref.dtype)


def _proj_b_kernel(x_ref, gmix_ref, gkv_ref, win_ref, wkv_ref, q_ref, qm_ref, k_ref, vt_ref, *, tk):
    xn = _rms_normalize(x_ref[...])
    h = (xn * gmix_ref[...]).astype(BF16)
    _store_chunks(h, win_ref, 0, q_ref, MAIN_WIDTH, Q_SCALE)
    _store_chunks(h, win_ref, MAIN_WIDTH, qm_ref, MEM_WIDTH, Q_SCALE)
    hk = (xn * gkv_ref[...]).astype(BF16)
    _store_chunks(hk, wkv_ref, 0, k_ref, MAIN_WIDTH, None)
    _store_transposed(hk, wkv_ref, MAIN_WIDTH, vt_ref, MAIN_WIDTH, tk)


def _mem_proj_kernel(x_ref, g_ref, w_ref, mk_ref, mvt_ref):
    n_mem = x_ref.shape[0]
    h = (_rms_normalize(x_ref[...]) * g_ref[...]).astype(BF16)
    _store_chunks(h, w_ref, 0, mk_ref, MEM_WIDTH, None)
    _store_transposed(h, w_ref, MEM_WIDTH, mvt_ref, MEM_WIDTH, n_mem)


def _row_spec(tm, width):
    return pl.BlockSpec((tm, width), lambda i: (i, 0))


def _full_spec(shape):
    return pl.BlockSpec(shape, lambda *_: (0,) * len(shape))


def _vt_out(t, tm, tk):
    shape = (t // tm, tm // tk, MAIN_WIDTH, tk)
    return pl.BlockSpec((1,) + shape[1:], lambda i: (i, 0, 0, 0)), jax.ShapeDtypeStruct(shape, BF16)


def _proj_a(x2, g, w_in, b_f, seq, tm, tk):
    t, d = x2.shape
    n_f = N_MAIN_HEADS
    wqkv = w_in[:, :3 * MAIN_WIDTH].astype(BF16)
    wf = jnp.pad(w_in[:, 3 * MAIN_WIDTH:3 * MAIN_WIDTH + n_f], ((0, 0), (0, LANES - n_f))).astype(BF16)
    wqm = w_in[:, 3 * MAIN_WIDTH + n_f:].astype(BF16)
    bf = jnp.pad(b_f, (0, LANES - n_f)).reshape(1, LANES)
    vt_spec, vt_shape = _vt_out(t, tm, tk)
    kernel = functools.partial(_proj_a_kernel, tiles_per_seq=seq // tm, tk=tk)
    return pl.pallas_call(
        kernel,
        grid=(t // tm,),
        in_specs=[_row_spec(tm, d), _full_spec((1, d)), _full_spec(wqkv.shape), _full_spec(wqm.shape),
                  _full_spec(wf.shape), _full_spec((1, LANES))],
        out_specs=[_row_spec(tm, MAIN_WIDTH), _row_spec(tm, MAIN_WIDTH), vt_spec,
                   _row_spec(tm, MEM_WIDTH), _row_spec(tm, LANES)],
        out_shape=[jax.ShapeDtypeStruct((t, MAIN_WIDTH), BF16), jax.ShapeDtypeStruct((t, MAIN_WIDTH), BF16),
                   vt_shape, jax.ShapeDtypeStruct((t, MEM_WIDTH), BF16), jax.ShapeDtypeStruct((t, LANES), BF16)],
        scratch_shapes=[pltpu.VMEM((SUBLANES, LANES), F32)],
        compiler_params=_params("arbitrary"),
        name="proj_a",
    )(x2, g.reshape(1, d), wqkv, wqm, wf, bf)


def _proj_b(x2, g_mix, g_kv, w_in, w_kv, tm, tk):
    t, d = x2.shape
    win = w_in.astype(BF16)
    wkv = w_kv.astype(BF16)
    vt_spec, vt_shape = _vt_out(t, tm, tk)
    return pl.pallas_call(
        functools.partial(_proj_b_kernel, tk=tk),
        grid=(t // tm,),
        in_specs=[_row_spec(tm, d), _full_spec((1, d)), _full_spec((1, d)), _full_spec(win.shape),
                  _full_spec(wkv.shape)],
        out_specs=[_row_spec(tm, MAIN_WIDTH), _row_spec(tm, MEM_WIDTH), _row_spec(tm, MAIN_WIDTH), vt_spec],
        out_shape=[jax.ShapeDtypeStruct((t, MAIN_WIDTH), BF16), jax.ShapeDtypeStruct((t, MEM_WIDTH), BF16),
                   jax.ShapeDtypeStruct((t, MAIN_WIDTH), BF16), vt_shape],
        compiler_params=_params("arbitrary"),
        name="proj_b",
    )(x2, g_mix.reshape(1, d), g_kv.reshape(1, d), win, wkv)


def _mem_proj(mem2, g, w, batch, n_mem):
    d = mem2.shape[1]
    wb = w.astype(BF16)
    return pl.pallas_call(
        _mem_proj_kernel,
        grid=(batch,),
        in_specs=[_row_spec(n_mem, d), _full_spec((1, d)), _full_spec(wb.shape)],
        out_specs=[_row_spec(n_mem, MEM_WIDTH), pl.BlockSpec((1, 1, MEM_WIDTH, n_mem), lambda i: (i, 0, 0, 0))],
        out_shape=[jax.ShapeDtypeStruct((batch * n_mem, MEM_WIDTH), BF16),
                   jax.ShapeDtypeStruct((batch, 1, MEM_WIDTH, n_mem), BF16)],
        compiler_params=_params("arbitrary"),
        name="mem_proj",
    )(mem2, g.reshape(1, d), wb)


ACC_ROWS = HEAD_DIM + BF16_ROWS
ATTN_FLAGS = None
RUN_DEAD = 105.0
LOOKAHEAD = 6
SCORE_AHEAD = 3
SUFFIX_AHEAD = 3


def _split_heads_into(q_ref, dst_ref, first, n_heads, gate_rows):
    tq = q_ref.shape[0]
    row = lax.broadcasted_iota(jnp.int32, (PAIR_WIDTH, 1), 0)
    low = row < HEAD_DIM
    for hp in range(n_heads // 2):
        qt = q_ref[:, hp * PAIR_WIDTH:(hp + 1) * PAIR_WIDTH].astype(F32).T
        for sub in range(2):
            head = first + 2 * hp + sub
            keep = low if sub == 0 else jnp.logical_not(low)
            dst_ref[head, 0:PAIR_WIDTH, :] = jnp.where(keep, qt, 0.0).astype(dst_ref.dtype)
            if gate_rows:
                pick = (row >= GATE_TERMS * head) & (row < GATE_TERMS * (head + 1))
                dst_ref[head, PAIR_WIDTH:, :] = jnp.broadcast_to(
                    jnp.where(pick, 1.0, 0.0).astype(dst_ref.dtype), (PAIR_WIDTH, tq))


def _values_with_ones(vt2, sub):
    vt = vt2[sub * HEAD_DIM:(sub + 1) * HEAD_DIM]
    return jnp.concatenate([vt, jnp.ones((BF16_ROWS, vt.shape[1]), vt.dtype)], axis=0)


def _softmax_update(s, head, vt_h, m_ref, acc_ref):
    m_prev = m_ref[head][0:1]
    m_new = jnp.maximum(m_prev, jnp.max(s, axis=0, keepdims=True))
    p = jnp.exp(s - m_new).astype(BF16)
    acc_ref[head] = jnp.exp(m_prev - m_new) * acc_ref[head] + _dot(vt_h, p)
    m_ref[head] = jnp.broadcast_to(m_new, m_ref.shape[1:])


def _softmax_finish(acc_ref, first, n_heads, o_ref, col0):
    for hp in range(n_heads // 2):
        outs = []
        for sub in range(2):
            acc = acc_ref[first + 2 * hp + sub]
            outs.append(acc[:HEAD_DIM] / acc[HEAD_DIM:HEAD_DIM + 1])
        o_ref[:, col0 + hp * PAIR_WIDTH:col0 + (hp + 1) * PAIR_WIDTH] = (
            jnp.concatenate(outs, axis=0).T.astype(o_ref.dtype))


def _init_softmax_state(m_ref, acc_ref):
    m_ref[...] = jnp.full_like(m_ref, -jnp.inf)
    acc_ref[...] = jnp.zeros_like(acc_ref)


def _memory_heads(qs_ref, mk_ref, mvt_ref, m_ref, acc_ref, o_ref):
    def pair_cols(i):
        return slice((i // 2) * PAIR_WIDTH, (i // 2 + 1) * PAIR_WIDTH)

    scores = [_dot(mk_ref[:, pair_cols(i)], qs_ref[N_MAIN_HEADS + i, 0:PAIR_WIDTH, :])
              for i in range(N_MEM_HEADS)]
    for i, s in enumerate(scores):
        vt_h = _values_with_ones(mvt_ref[0, 0, pair_cols(i), :], i % 2)
        _softmax_update(s, N_MAIN_HEADS + i, vt_h, m_ref, acc_ref)
    _softmax_finish(acc_ref, N_MAIN_HEADS, N_MEM_HEADS, o_ref, MAIN_WIDTH)


def _fox_attn_kernel(q_ref, k_ref, cp_ref, vt_ref, qm_ref, mk_ref, mvt_ref, o_ref, qs_ref, m_ref, acc_ref, *, tq):
    qi = pl.program_id(1)
    tk = tq
    _split_heads_into(q_ref, qs_ref, 0, N_MAIN_HEADS, True)
    _split_heads_into(qm_ref, qs_ref, N_MAIN_HEADS, N_MEM_HEADS, False)
    _init_softmax_state(m_ref, acc_ref)

    def block(kb, causal):
        ks = pl.multiple_of(kb * tk, tk)
        cp = cp_ref[pl.ds(ks, tk), :]

        def scores(head):
            cols = slice((head // 2) * PAIR_WIDTH, (head // 2 + 1) * PAIR_WIDTH)
            kk = jnp.concatenate([k_ref[pl.ds(ks, tk), cols], cp], axis=1)
            s = _dot(kk, qs_ref[head])
            return s if causal is None else jnp.where(causal, s, -jnp.inf)

        pending = [scores(h) for h in range(LOOKAHEAD)]
        for head in range(N_MAIN_HEADS):
            s = pending.pop(0)
            if head + LOOKAHEAD < N_MAIN_HEADS:
                pending.append(scores(head + LOOKAHEAD))
            cols = slice((head // 2) * PAIR_WIDTH, (head // 2 + 1) * PAIR_WIDTH)
            _softmax_update(s, head, _values_with_ones(vt_ref[0, kb, cols, :], head % 2), m_ref, acc_ref)

    def body(kb, carry):
        block(kb, None)
        return carry

    lax.fori_loop(0, qi, body, 0)
    key = lax.broadcasted_iota(jnp.int32, (tk, tq), 0)
    qry = lax.broadcasted_iota(jnp.int32, (tk, tq), 1)
    block(qi, key <= qry)
    _softmax_finish(acc_ref, 0, N_MAIN_HEADS, o_ref, 0)
    _memory_heads(qs_ref, mk_ref, mvt_ref, m_ref, acc_ref, o_ref)


def _sb_attn_kernel(q_ref, k_ref, vt_ref, qm_ref, mk_ref, mvt_ref, o_ref, qs_ref, m_ref, acc_ref, run_ref, *, tq):
    qi = pl.program_id(1)
    tk = tq
    _split_heads_into(q_ref, qs_ref, 0, N_MAIN_HEADS, False)
    _split_heads_into(qm_ref, qs_ref, N_MAIN_HEADS, N_MEM_HEADS, False)
    _init_softmax_state(m_ref, acc_ref)
    run_ref[...] = jnp.zeros_like(run_ref)

    ss = lax.broadcasted_iota(jnp.int32, (LANES + BF16_ROWS, 2 * LANES), 0)
    jj = lax.broadcasted_iota(jnp.int32, (LANES + BF16_ROWS, 2 * LANES), 1) & (LANES - 1)
    suffix = jnp.where((jj > ss) | (ss >= LANES), 1.0, 0.0).astype(BF16)

    def block(kb, causal):
        ks = pl.multiple_of(kb * tk, tk)
        n_chunks = tk // LANES

        def pair_cols(head):
            return slice((head // 2) * PAIR_WIDTH, (head // 2 + 1) * PAIR_WIDTH)

        def score_stage(head):
            z = _dot(k_ref[pl.ds(ks, tk), pair_cols(head)], qs_ref[head])
            log_beta, split = [], []
            for c in range(n_chunks):
                zc = z[c * LANES:(c + 1) * LANES]
                nl = jnp.maximum(zc, 0.0) + jnp.log(1.0 + jnp.exp2(jnp.abs(zc) * -LOG2_E))
                log_beta.append(zc - nl)
                if causal is not None:
                    nl = jnp.where(causal[c], nl, 0.0)
                hi = nl.astype(BF16)
                split.append(jnp.concatenate([hi, (nl - hi.astype(F32)).astype(BF16)], axis=0))
            return log_beta, split

        def suffix_stage(state):
            log_beta, split = state
            return log_beta, [_dot(suffix, sp) for sp in split]

        def value_stage(head, state):
            log_beta, sums = state
            run = run_ref[head][0:1]
            a_chunks = [None] * n_chunks
            for c in reversed(range(n_chunks)):
                a = jnp.exp(log_beta[c] - (sums[c][:LANES] + run))
                if causal is not None:
                    a = jnp.where(causal[c], a, 0.0)
                a_chunks[c] = a.astype(BF16)
                run = run + sums[c][LANES:LANES + 1]
            run_ref[head] = jnp.broadcast_to(run, run_ref.shape[1:])
            vt = vt_ref[0, kb, pair_cols(head), :][(head % 2) * HEAD_DIM:(head % 2 + 1) * HEAD_DIM]
            acc_ref[head, 0:HEAD_DIM] += _dot(vt, jnp.concatenate(a_chunks, axis=0))

        scored, summed = {}, {}
        for step in range(N_MAIN_HEADS + SCORE_AHEAD + SUFFIX_AHEAD):
            if step < N_MAIN_HEADS:
                scored[step] = score_stage(step)
            h2 = step - SCORE_AHEAD
            if 0 <= h2 < N_MAIN_HEADS:
                summed[h2] = suffix_stage(scored.pop(h2))
            h3 = h2 - SUFFIX_AHEAD
            if 0 <= h3 < N_MAIN_HEADS:
                value_stage(h3, summed.pop(h3))

    key = lax.broadcasted_iota(jnp.int32, (LANES, tq), 0)
    qry = lax.broadcasted_iota(jnp.int32, (LANES, tq), 1)
    block(qi, [(key + c * LANES) < qry for c in range(tk // LANES)])

    def any_weight_left():
        least = functools.reduce(jnp.minimum, [run_ref[h] for h in range(N_MAIN_HEADS)])
        return (jnp.min(least) < RUN_DEAD).astype(jnp.int32)

    def cond(carry):
        it, live = carry
        return (it < qi) & (live > 0)

    def body(carry):
        it, _ = carry
        block(qi - 1 - it, None)
        return it + 1, any_weight_left()

    lax.while_loop(cond, body, (jnp.int32(0), any_weight_left()))
    for hp in range(N_MAIN_HEADS // 2):
        pair =jnp.concatenate([acc_ref[2 * hp, 0:HEAD_DIM], acc_ref[2 * hp + 1, 0:HEAD_DIM]], axis=0)
        o_ref[:, hp * PAIR_WIDTH:(hp + 1) * PAIR_WIDTH] = pair.T.astype(o_ref.dtype)
    _memory_heads(qs_ref, mk_ref, mvt_ref, m_ref, acc_ref, o_ref)


def _attention(kind, q, k, cp, vt, qm, mk, mvt, batch, seq, n_mem, tq):
    t = q.shape[0]
    nq = seq // tq
    vt = vt.reshape(batch, seq // vt.shape[3], MAIN_WIDTH, vt.shape[3])
    q_spec = pl.BlockSpec((tq, MAIN_WIDTH), lambda b, i: (b * nq + i, 0))
    k_spec = pl.BlockSpec((seq, MAIN_WIDTH), lambda b, i: (b, 0))
    vt_spec = pl.BlockSpec((1,) + vt.shape[1:], lambda b, i: (b, 0, 0, 0))
    qm_spec = pl.BlockSpec((tq, MEM_WIDTH), lambda b, i: (b * nq + i, 0))
    mk_spec = pl.BlockSpec((n_mem, MEM_WIDTH), lambda b, i: (b, 0))
    mvt_spec = pl.BlockSpec((1, 1, MEM_WIDTH, n_mem), lambda b, i: (b, 0, 0, 0))
    o_spec = pl.BlockSpec((tq, MAIN_WIDTH + MEM_WIDTH), lambda b, i: (b * nq + i, 0))
    q_rows = 2 * PAIR_WIDTH if kind == "fox" else PAIR_WIDTH
    scratch = [pltpu.VMEM((N_HEADS, q_rows, tq), BF16),
               pltpu.VMEM((N_HEADS, SUBLANES, tq), F32),
               pltpu.VMEM((N_HEADS, ACC_ROWS, tq), F32)]
    if kind == "fox":
        cp_spec = pl.BlockSpec((seq, LANES), lambda b, i: (b, 0))
        kernel = functools.partial(_fox_attn_kernel, tq=tq)
        in_specs = [q_spec, k_spec, cp_spec, vt_spec, qm_spec, mk_spec, mvt_spec]
        args = (q, k, cp, vt, qm, mk, mvt)
    else:
        kernel = functools.partial(_sb_attn_kernel, tq=tq)
        in_specs = [q_spec, k_spec, vt_spec, qm_spec, mk_spec, mvt_spec]
        args = (q, k, vt, qm, mk, mvt)
        scratch.append(pltpu.VMEM((N_MAIN_HEADS, SUBLANES, tq), F32))
    return pl.pallas_call(
        kernel,
        grid=(batch, nq),
        in_specs=in_specs,
        out_specs=o_spec,
        out_shape=jax.ShapeDtypeStruct((t, MAIN_WIDTH + MEM_WIDTH), BF16),
        scratch_shapes=scratch,
        compiler_params=_params("arbitrary", "arbitrary", flags=ATTN_FLAGS),
        name=kind + "_attn",
    )(*args)


def _causal_conv(u, prev, cw, cb):
    r8 = lax.broadcasted_iota(jnp.int32, (SUBLANES, 1), 0)
    out = cb
    for j in range(CONV_WIDTH - 1):
        shift = CONV_WIDTH - 1 - j
        rolled = pltpu.roll(u, shift, axis=0)
        top = jnp.where(r8 < shift, pltpu.roll(prev, shift, axis=0), rolled[:SUBLANES])
        shifted = jnp.concatenate([top, rolled[SUBLANES:]], axis=0)
        out = out + cw[j:j + 1, :] * shifted
    return out + cw[CONV_WIDTH - 1:CONV_WIDTH, :] * u


def _out_ffn_kernel(x_ref, o_ref, wout_ref, g_ref, wup_ref, cw_ref, cb_ref, wd_ref, fg_ref, out_ref,
                    act_ref, carry_ref, *, tiles_per_seq, tf, final_norm):
    tm = x_ref.shape[0]
    d_ff = wd_ref.shape[0]

    @pl.when(pl.program_id(0) % tiles_per_seq == 0)
    def _():
        carry_ref[...] = jnp.zeros_like(carry_ref)

    x1 = x_ref[...] + _dot(o_ref[...], wout_ref[...])
    h = (_rms_normalize(x1) * g_ref[...]).astype(BF16)
    for c in range(0, d_ff, tf):
        halves = []
        for half in range(2):
            cols = slice(half * d_ff + c, half * d_ff + c + tf)
            u = _dot(h, wup_ref[:, cols])
            prev = carry_ref[:, cols]
            carry_ref[:, cols] = u[tm - SUBLANES:]
            halves.append(_causal_conv(u, prev, cw_ref[:, cols], cb_ref[:, cols]))
        gate, val = halves
        act_ref[:, c:c + tf] = (gate * (1.0 / (1.0 + jnp.exp(-gate))) * val).astype(BF16)
    y = x1 + _dot(act_ref[...], wd_ref[...])
    if final_norm:
        y = _rms_normalize(y) * fg_ref[...]
    out_ref[...] = y


def _resident_spec(shape):
    return pl.BlockSpec(shape, lambda *_: (0,) * len(shape), pipeline_mode=pl.Buffered(1))


def _out_ffn(x2, o, w_out, g, w_up, conv_w, conv_b, w_down, final_g, seq, tm, tf, final_norm):
    t, d = x2.shape
    d_ff = w_down.shape[0]
    wout = w_out.astype(BF16)
    wup = w_up.astype(BF16)
    wdn = w_down.astype(BF16)
    cb = conv_b.reshape(1, 2 * d_ff)
    kernel = functools.partial(_out_ffn_kernel, tiles_per_seq=seq // tm, tf=tf, final_norm=final_norm)
    return pl.pallas_call(
        kernel,
        grid=(t // tm,),
        in_specs=[_row_spec(tm, d), _row_spec(tm, o.shape[1]),
                  _resident_spec(wout.shape), _resident_spec((1, d)), _resident_spec(wup.shape),
                  _resident_spec(conv_w.shape), _resident_spec(cb.shape), _resident_spec(wdn.shape),
                  _resident_spec((1, d))],
        out_specs=_row_spec(tm, d),
        out_shape=jax.ShapeDtypeStruct((t, d), F32),
        scratch_shapes=[pltpu.VMEM((tm, d_ff), BF16), pltpu.VMEM((SUBLANES, 2 * d_ff), F32)],
        compiler_params=_params("arbitrary"),
        name="out_ffn",
    )(x2, o, wout, g.reshape(1, d), wup, conv_w, cb, wdn, final_g.reshape(1, d))


def kernel(x, mem, ln_mix_g, w_in_a, b_f_a, w_in_b, ln_kv_g, w_kv, ln_mem_g, w_memkv, w_out, ln_ffn_g,
           w_up, conv_w, conv_b, w_down, final_g):
    batch, seq, d = x.shape
    n_mem = mem.shape[1]
    depth = ln_mix_g.shape[0]
    n_a = w_in_a.shape[0]
    assert w_in_b.shape[0] == 1, "one stick-breaking layer reads the shared K/V"
    tm_proj = min(512, seq)
    tq_attn = min(256, seq)
    tm_ffn, tf = min(512, seq), 256

    x2 = x.reshape(batch * seq, d)
    mem2 = mem.reshape(batch * n_mem, d)
    for layer in range(depth):
        mk, mvt = _mem_proj(mem2, ln_mem_g[layer], w_memkv[layer], batch, n_mem)
        if layer < n_a:
            q, k, vt, qm, cp = _proj_a(x2, ln_mix_g[layer], w_in_a[layer], b_f_a[layer], seq, tm_proj, tq_attn)
            o = _attention("fox", q, k, cp, vt, qm, mk, mvt, batch, seq, n_mem, tq_attn)
        else:
            q, qm, k_sh, vt_sh = _proj_b(x2, ln_mix_g[layer], ln_kv_g, w_in_b[0], w_kv, tm_proj, tq_attn)
            o = _attention("sb", q, k_sh, None, vt_sh, qm, mk, mvt, batch, seq, n_mem, tq_attn)
        x2 = _out_ffn(x2, o, w_out[layer], ln_ffn_g[layer], w_up[layer], conv_w[layer], conv_b[layer],
                      w_down[layer], final_g, seq, tm_ffn, tf, final_norm=(layer == depth - 1))
    return x2.reshape(batch, seq, d)
```

```python
import functools

import jax
import jax.numpy as jnp
from jax import lax
from jax.experimental import pallas as pl
from jax.experimental.pallas import tpu as pltpu

HEAD_DIM = 64
N_MAIN_HEADS = 12
N_MEM_HEADS = 4
N_HEADS = N_MAIN_HEADS + N_MEM_HEADS
MAIN_WIDTH = N_MAIN_HEADS * HEAD_DIM
MEM_WIDTH = N_MEM_HEADS * HEAD_DIM
CONV_WIDTH = 3
EPS = 1e-6
Q_SCALE = HEAD_DIM ** -0.5
LOG2_E = 1.4426950408889634

LANES = 128
SUBLANES = 8
BF16_ROWS = 16
PAIR_WIDTH = 2 * HEAD_DIM
GATE_TERMS = 3
VMEM_LIMIT = 56 * 1024 * 1024

F32 = jnp.float32
BF16 = jnp.bfloat16

_NT = (((1,), (1,)), ((), ()))


def _params(*sem, flags=None):
    return pltpu.CompilerParams(dimension_semantics=sem, vmem_limit_bytes=VMEM_LIMIT, flags=flags)


def _dot(a, b):
    return jnp.dot(a, b, preferred_element_type=F32)


def _dot_nt(a, b):
    return lax.dot_general(a, b, _NT, preferred_element_type=F32)


def _rms_normalize(x):
    return x * lax.rsqrt(jnp.mean(x * x, axis=-1, keepdims=True) + EPS)


def _log_sigmoid(x):
    return jnp.minimum(x, 0.0) - jnp.log1p(jnp.exp(-jnp.abs(x)))


def _split3(x):
    hi = x.astype(BF16)
    r1 = x - hi.astype(F32)
    mid = r1.astype(BF16)
    return hi, mid, (r1 - mid.astype(F32)).astype(BF16)


def _store_chunks(h, w_ref, w_lo, out_ref, width, scale, chunk=256):
    for c in range(0, width, chunk):
        r = _dot(h, w_ref[:, w_lo + c:w_lo + c + chunk])
        if scale is not None:
            r = r * scale
        out_ref[:, c:c + chunk] = r.astype(out_ref.dtype)


def _store_transposed(h, w_ref, w_lo, out_ref, width, tk, chunk=256):
    tm = h.shape[0]
    for c in range(0, width, chunk):
        v = _dot(h, w_ref[:, w_lo + c:w_lo + c + chunk])
        for r in range(tm // tk):
            out_ref[0, r, c:c + chunk, :] = v[r * tk:(r + 1) * tk, :].T.astype(out_ref.dtype)


def _proj_a_kernel(x_ref, g_ref, wqkv_ref, wqm_ref, wf_ref, bf_ref,
                   q_ref, k_ref, vt_ref, qm_ref, cp_ref, carry_ref, *, tiles_per_seq, tk):
    tm = x_ref.shape[0]

    @pl.when(pl.program_id(0) % tiles_per_seq == 0)
    def _():
        carry_ref[...] = jnp.zeros_like(carry_ref)

    h = (_rms_normalize(x_ref[...]) * g_ref[...]).astype(BF16)
    _store_chunks(h, wqkv_ref, 0, q_ref, MAIN_WIDTH, Q_SCALE)
    _store_chunks(h, wqkv_ref, MAIN_WIDTH, k_ref, MAIN_WIDTH, None)
    _store_transposed(h, wqkv_ref, 2 * MAIN_WIDTH, vt_ref, MAIN_WIDTH, tk)
    _store_chunks(h, wqm_ref, 0, qm_ref, MEM_WIDTH, Q_SCALE)

    log_f = _log_sigmoid(_dot(h, wf_ref[...]) + bf_ref[...])
    hi, mid, lo = _split3(log_f)
    row = lax.broadcasted_iota(jnp.int32, (tm, tm), 0)
    col = lax.broadcasted_iota(jnp.int32, (tm, tm), 1)
    tri = jnp.where(row >= col, 1.0, 0.0).astype(BF16)
    c = (_dot(tri, hi) + _dot(tri, mid)) + _dot(tri, lo) + carry_ref[0:1, :]
    carry_ref[0:1, :] = c[tm - 1:tm, :]
    rr = lax.broadcasted_iota(jnp.int32, (GATE_TERMS * LANES, LANES), 0)
    cc = lax.broadcasted_iota(jnp.int32, (GATE_TERMS * LANES, LANES), 1)
    head, term = rr & (LANES - 1), rr >> 7
    place = jnp.where((cc == GATE_TERMS * head + term) & (head < N_MAIN_HEADS), 1.0, 0.0).astype(BF16)
    cp_ref[...] = _dot(jnp.concatenate(_split3(-c), axis=1), place).astype(cp_ref.dtype)


def _proj_b_kernel(x_ref, gmix_ref, gkv_ref, win_ref, wkv_ref, q_ref, qm_ref, k_ref, vt_ref, *, tk):
    xn = _rms_normalize(x_ref[...])
    h = (xn * gmix_ref[...]).astype(BF16)
    _store_chunks(h, win_ref, 0, q_ref, MAIN_WIDTH, Q_SCALE)
    _store_chunks(h, win_ref, MAIN_WIDTH, qm_ref, MEM_WIDTH, Q_SCALE)
    hk = (xn * gkv_ref[...]).astype(BF16)
    _store_chunks(hk, wkv_ref, 0, k_ref, MAIN_WIDTH, None)
    _store_transposed(hk, wkv_ref, MAIN_WIDTH, vt_ref, MAIN_WIDTH, tk)


def _mem_proj_kernel(x_ref, g_ref, w_ref, mk_ref, mvt_ref):
    n_mem = x_ref.shape[0]
    h = (_rms_normalize(x_ref[...]) * g_ref[...]).astype(BF16)
    _store_chunks(h, w_ref, 0, mk_ref, MEM_WIDTH, None)
    _store_transposed(h, w_ref, MEM_WIDTH, mvt_ref, MEM_WIDTH, n_mem)


def _row_spec(tm, width):
    return pl.BlockSpec((tm, width), lambda i: (i, 0))


def _full_spec(shape):
    return pl.BlockSpec(shape, lambda *_: (0,) * len(shape))


def _vt_out(t, seq, tm, tk):
    tiles = seq // tm
    shape = (t // seq, seq // tk, MAIN_WIDTH, tk)
    spec = pl.BlockSpec((1, tm // tk, MAIN_WIDTH, tk), lambda i: (i // tiles, i % tiles, 0, 0))
    return spec, jax.ShapeDtypeStruct(shape, BF16)


def _proj_a(x2, g, w_in, b_f, seq, tm, tk):
    t, d = x2.shape
    n_f = N_MAIN_HEADS
    wqkv = w_in[:, :3 * MAIN_WIDTH].astype(BF16)
    wf = jnp.pad(w_in[:, 3 * MAIN_WIDTH:3 * MAIN_WIDTH + n_f], ((0, 0), (0, LANES - n_f))).astype(BF16)
    wqm = w_in[:, 3 * MAIN_WIDTH + n_f:].astype(BF16)
    bf = jnp.pad(b_f, (0, LANES - n_f)).reshape(1, LANES)
    vt_spec, vt_shape = _vt_out(t, seq, tm, tk)
    kernel = functools.partial(_proj_a_kernel, tiles_per_seq=seq // tm, tk=tk)
    return pl.pallas_call(
        kernel,
        grid=(t // tm,),
        in_specs=[_row_spec(tm, d), _full_spec((1, d)), _full_spec(wqkv.shape), _full_spec(wqm.shape),
                  _full_spec(wf.shape), _full_spec((1, LANES))],
        out_specs=[_row_spec(tm, MAIN_WIDTH), _row_spec(tm, MAIN_WIDTH), vt_spec,
                   _row_spec(tm, MEM_WIDTH), _row_spec(tm, LANES)],
        out_shape=[jax.ShapeDtypeStruct((t, MAIN_WIDTH), BF16), jax.ShapeDtypeStruct((t, MAIN_WIDTH), BF16),
                   vt_shape, jax.ShapeDtypeStruct((t, MEM_WIDTH), BF16), jax.ShapeDtypeStruct((t, LANES), BF16)],
        scratch_shapes=[pltpu.VMEM((SUBLANES, LANES), F32)],
        compiler_params=_params("arbitrary"),
        name="proj_a",
    )(x2, g.reshape(1, d), wqkv, wqm, wf, bf)


def _proj_b(x2, g_mix, g_kv, w_in, w_kv, seq, tm, tk):
    t, d = x2.shape
    win = w_in.astype(BF16)
    wkv = w_kv.astype(BF16)
    vt_spec, vt_shape = _vt_out(t, seq, tm, tk)
    return pl.pallas_call(
        functools.partial(_proj_b_kernel, tk=tk),
        grid=(t // tm,),
        in_specs=[_row_spec(tm, d), _full_spec((1, d)), _full_spec((1, d)), _full_spec(win.shape),
                  _full_spec(wkv.shape)],
        out_specs=[_row_spec(tm, MAIN_WIDTH), _row_spec(tm, MEM_WIDTH), _row_spec(tm, MAIN_WIDTH), vt_spec],
        out_shape=[jax.ShapeDtypeStruct((t, MAIN_WIDTH), BF16), jax.ShapeDtypeStruct((t, MEM_WIDTH), BF16),
                   jax.ShapeDtypeStruct((t, MAIN_WIDTH), BF16), vt_shape],
        compiler_params=_params("arbitrary"),
        name="proj_b",
    )(x2, g_mix.reshape(1, d), g_kv.reshape(1, d), win, wkv)


def _mem_proj(mem2, g, w, batch, n_mem):
    d = mem2.shape[1]
    wb = w.astype(BF16)
    return pl.pallas_call(
        _mem_proj_kernel,
        grid=(batch,),
        in_specs=[_row_spec(n_mem, d), _full_spec((1, d)), _full_spec(wb.shape)],
        out_specs=[_row_spec(n_mem, MEM_WIDTH), pl.BlockSpec((1, 1, MEM_WIDTH, n_mem), lambda i: (i, 0, 0, 0))],
        out_shape=[jax.ShapeDtypeStruct((batch * n_mem, MEM_WIDTH), BF16),
                   jax.ShapeDtypeStruct((batch, 1, MEM_WIDTH, n_mem), BF16)],
        compiler_params=_params("arbitrary"),
        name="mem_proj",
    )(mem2, g.reshape(1, d), wb)


ACC_ROWS = HEAD_DIM + BF16_ROWS
ATTN_FLAGS = None
RUN_DEAD = 105.0
LOOKAHEAD = 6
SCORE_AHEAD = 3
SUFFIX_AHEAD = 3


def _split_heads_into(q_ref, dst_ref, first, n_heads, gate_rows):
    tq = q_ref.shape[0]
    row = lax.broadcasted_iota(jnp.int32, (PAIR_WIDTH, 1), 0)
    low = row < HEAD_DIM
    for hp in range(n_heads // 2):
        qt = q_ref[:, hp * PAIR_WIDTH:(hp + 1) * PAIR_WIDTH].astype(F32).T
        for sub in range(2):
            head = first + 2 * hp + sub
            keep = low if sub == 0 else jnp.logical_not(low)
            dst_ref[head, 0:PAIR_WIDTH, :] = jnp.where(keep, qt, 0.0).astype(dst_ref.dtype)
            if gate_rows:
                pick = (row >= GATE_TERMS * head) & (row < GATE_TERMS * (head + 1))
                dst_ref[head, PAIR_WIDTH:, :] = jnp.broadcast_to(
                    jnp.where(pick, 1.0, 0.0).astype(dst_ref.dtype), (PAIR_WIDTH, tq))


def _values_with_ones(vt2, sub):
    vt = vt2[sub * HEAD_DIM:(sub + 1) * HEAD_DIM]
    return jnp.concatenate([vt, jnp.ones((BF16_ROWS, vt.shape[1]), vt.dtype)], axis=0)


def _softmax_update(s, head, vt_h, m_ref, acc_ref):
    m_prev = m_ref[head][0:1]
    m_new = jnp.maximum(m_prev, jnp.max(s, axis=0, keepdims=True))
    p = jnp.exp(s - m_new).astype(BF16)
    acc_ref[head] = jnp.exp(m_prev - m_new) * acc_ref[head] + _dot(vt_h, p)
    m_ref[head] = jnp.broadcast_to(m_new, m_ref.shape[1:])


def _softmax_finish(acc_ref, first, n_heads, o_ref, col0):
    for hp in range(n_heads // 2):
        outs = []
        for sub in range(2):
            acc = acc_ref[first + 2 * hp + sub]
            outs.append(acc[:HEAD_DIM] / acc[HEAD_DIM:HEAD_DIM + 1])
        o_ref[:, col0 + hp * PAIR_WIDTH:col0 + (hp + 1) * PAIR_WIDTH] = (
            jnp.concatenate(outs, axis=0).T.astype(o_ref.dtype))


def _init_softmax_state(m_ref, acc_ref):
    m_ref[...] = jnp.full_like(m_ref, -jnp.inf)
    acc_ref[...] = jnp.zeros_like(acc_ref)


def _pair_cols(head):
    return slice((head // 2) * PAIR_WIDTH, (head // 2 + 1) * PAIR_WIDTH)


def _softmax_pipeline(items, m_ref, acc_ref):
    pending = [item[0]() for item in items[:LOOKAHEAD]]
    for n, (_, values_fn, head) in enumerate(items):
        s = pending.pop(0)
        if n + LOOKAHEAD < len(items):
            pending.append(items[n + LOOKAHEAD][0]())
        _softmax_update(s, head, values_fn(), m_ref, acc_ref)


def _memory_items(qs_ref, mk_ref, mvt_ref):
    def score(i):
        return _dot(mk_ref[:, _pair_cols(i)], qs_ref[N_MAIN_HEADS + i, 0:PAIR_WIDTH, :])

    def values(i):
        return _values_with_ones(mvt_ref[0, 0, _pair_cols(i), :], i % 2)

    return [(functools.partial(score, i), functools.partial(values, i), N_MAIN_HEADS + i)
            for i in range(N_MEM_HEADS)]


def _fox_attn_kernel(q_ref, k_ref, cp_ref, vt_ref, qm_ref, mk_ref, mvt_ref, o_ref, qs_ref, m_ref, acc_ref, *, tq):
    qi = pl.program_id(1)
    tk = tq
    _split_heads_into(q_ref, qs_ref, 0, N_MAIN_HEADS, True)
    _split_heads_into(qm_ref, qs_ref, N_MAIN_HEADS, N_MEM_HEADS, False)
    _init_softmax_state(m_ref, acc_ref)

    def block_items(kb, causal):
        ks = pl.multiple_of(kb * tk, tk)

        def score(head):
            kk = jnp.concatenate([k_ref[pl.ds(ks, tk), _pair_cols(head)], cp_ref[pl.ds(ks, tk), :]], axis=1)
            s = _dot(kk, qs_ref[head])
            return s if causal is None else jnp.where(causal, s, -jnp.inf)

        def values(head):
            return _values_with_ones(vt_ref[0, kb, _pair_cols(head), :], head % 2)

        return [(functools.partial(score, h), functools.partial(values, h), h) for h in range(N_MAIN_HEADS)]

    def two_blocks(it, carry):
        _softmax_pipeline(block_items(2 * it, None) + block_items(2 * it + 1, None), m_ref, acc_ref)
        return carry

    lax.fori_loop(0, qi >> 1, two_blocks, 0)
    key = lax.broadcasted_iota(jnp.int32, (tk, tq), 0)
    qry = lax.broadcasted_iota(jnp.int32, (tk, tq), 1)
    tail = block_items(qi, key <= qry) + _memory_items(qs_ref, mk_ref, mvt_ref)

    @pl.when((qi & 1) == 1)
    def _():
        _softmax_pipeline(block_items(qi - 1, None) + tail, m_ref, acc_ref)

    @pl.when((qi & 1) == 0)
    def _():
        _softmax_pipeline(tail, m_ref, acc_ref)

    _softmax_finish(acc_ref, 0, N_MAIN_HEADS, o_ref, 0)
    _softmax_finish(acc_ref, N_MAIN_HEADS, N_MEM_HEADS, o_ref, MAIN_WIDTH)


def _sb_attn_kernel(q_ref, k_ref, vt_ref, qm_ref, mk_ref, mvt_ref, o_ref, qs_ref, m_ref, acc_ref, run_ref, *, tq):
    qi = pl.program_id(1)
    tk = tq
    _split_heads_into(q_ref, qs_ref, 0, N_MAIN_HEADS, False)
    _split_heads_into(qm_ref, qs_ref, N_MAIN_HEADS, N_MEM_HEADS, False)
    _init_softmax_state(m_ref, acc_ref)
    run_ref[...] = jnp.zeros_like(run_ref)

    ss = lax.broadcasted_iota(jnp.int32, (LANES + BF16_ROWS, 2 * LANES), 0)
    jj = lax.broadcasted_iota(jnp.int32, (LANES + BF16_ROWS, 2 * LANES), 1) & (LANES - 1)
    suffix = jnp.where((jj > ss) | (ss >= LANES), 1.0, 0.0).astype(BF16)

    n_chunks = tk // LANES

    def stage_pipeline(items):
        def score_stage(kb, causal, head):
            ks = pl.multiple_of(kb * tk, tk)
            z = _dot(k_ref[pl.ds(ks, tk), _pair_cols(head)], qs_ref[head])
            log_beta, split = [], []
            for c in range(n_chunks):
                zc = z[c * LANES:(c + 1) * LANES]
                nl = jnp.maximum(zc, 0.0) + jnp.log(1.0 + jnp.exp2(jnp.abs(zc) * -LOG2_E))
                log_beta.append(zc - nl)
                if causal is not None:
                    nl = jnp.where(causal[c], nl, 0.0)
                hi = nl.astype(BF16)
                split.append(jnp.concatenate([hi, (nl - hi.astype(F32)).astype(BF16)], axis=0))
            return log_beta, split

        def suffix_stage(state):
            log_beta, split = state
            return log_beta, [_dot(suffix, sp) for sp in split]

        def value_stage(kb, causal, head, state):
            log_beta, sums = state
            run = run_ref[head][0:1]
            a_chunks = [None] * n_chunks
            for c in reversed(range(n_chunks)):
                a = jnp.exp(log_beta[c] - (sums[c][:LANES] + run))
                if causal is not None:
                    a = jnp.where(causal[c], a, 0.0)
                a_chunks[c] = a.astype(BF16)
                run = run + sums[c][LANES:LANES + 1]
            run_ref[head] = jnp.broadcast_to(run, run_ref.shape[1:])
            vt = vt_ref[0, kb, _pair_cols(head), :][(head % 2) * HEAD_DIM:(head % 2 + 1) * HEAD_DIM]
            acc_ref[head, 0:HEAD_DIM] += _dot(vt, jnp.concatenate(a_chunks, axis=0))

        scored, summed = {}, {}
        for step in range(len(items) + SCORE_AHEAD + SUFFIX_AHEAD):
            if step < len(items):
                scored[step] = score_stage(*items[step])
            i2 = step - SCORE_AHEAD
            if 0 <= i2 < len(items):
                summed[i2] = suffix_stage(scored.pop(i2))
            i3 = i2 - SUFFIX_AHEAD
            if 0 <= i3 < len(items):
                value_stage(*items[i3], summed.pop(i3))

    def block_items(kb, causal):
        return [(kb, causal, head) for head in range(N_MAIN_HEADS)]

    key = lax.broadcasted_iota(jnp.int32, (LANES, tq), 0)
    qry = lax.broadcasted_iota(jnp.int32, (LANES, tq), 1)
    diagonal = block_items(qi, [(key + c * LANES) < qry for c in range(n_chunks)])
    memory = _memory_items(qs_ref, mk_ref, mvt_ref)

    @pl.when(qi > 0)
    def _():
        stage_pipeline(diagonal + block_items(qi - 1, None))
        _softmax_pipeline(memory, m_ref, acc_ref)

    @pl.when(qi == 0)
    def _():
        stage_pipeline(diagonal)
        _softmax_pipeline(memory, m_ref, acc_ref)

    def any_weight_left():
        least = functools.reduce(jnp.minimum, [run_ref[h] for h in range(N_MAIN_HEADS)])
        return (jnp.min(least) < RUN_DEAD).astype(jnp.int32)

    def cond(carry):
        it, live = carry
        return (it < qi) & (live > 0)

    def body(carry):
        it, _ = carry
        stage_pipeline(block_items(qi - 1 - it, None))
        return it + 1, any_weight_left()

    lax.while_loop(cond, body, (jnp.int32(1), any_weight_left()))
    for hp in range(N_MAIN_HEADS // 2):
        pair = jnp.concatenate([acc_ref[2 * hp, 0:HEAD_DIM], acc_ref[2 * hp + 1, 0:HEAD_DIM]], axis=0)
        o_ref[:, hp * PAIR_WIDTH:(hp + 1) * PAIR_WIDTH] = pair.T.astype(o_ref.dtype)
    _softmax_finish(acc_ref, N_MAIN_HEADS, N_MEM_HEADS, o_ref, MAIN_WIDTH)


def _attention(kind, q, k, cp, vt, qm, mk, mvt, batch, seq, n_mem, tq):
    t = q.shape[0]
    nq = seq // tq
    q_spec = pl.BlockSpec((tq, MAIN_WIDTH), lambda b, i: (b * nq + i, 0))
    k_spec = pl.BlockSpec((seq, MAIN_WIDTH), lambda b, i: (b, 0))
    vt_spec = pl.BlockSpec((1,) + vt.shape[1:], lambda b, i: (b, 0, 0, 0))
    qm_spec = pl.BlockSpec((tq, MEM_WIDTH), lambda b, i: (b * nq + i, 0))
    mk_spec = pl.BlockSpec((n_mem, MEM_WIDTH), lambda b, i: (b, 0))
    mvt_spec = pl.BlockSpec((1, 1, MEM_WIDTH, n_mem), lambda b, i: (b, 0, 0, 0))
    o_spec = pl.BlockSpec((tq, MAIN_WIDTH + MEM_WIDTH), lambda b, i: (b * nq + i, 0))
    q_rows = 2 * PAIR_WIDTH if kind == "fox" else PAIR_WIDTH
    scratch = [pltpu.VMEM((N_HEADS, q_rows, tq), BF16),
               pltpu.VMEM((N_HEADS, SUBLANES, tq), F32),
               pltpu.VMEM((N_HEADS, ACC_ROWS, tq), F32)]
    if kind == "fox":
        cp_spec = pl.BlockSpec((seq, LANES), lambda b, i: (b, 0))
        kernel = functools.partial(_fox_attn_kernel, tq=tq)
        in_specs = [q_spec, k_spec, cp_spec, vt_spec, qm_spec, mk_spec, mvt_spec]
        args = (q, k, cp, vt, qm, mk, mvt)
    else:
        kernel = functools.partial(_sb_attn_kernel, tq=tq)
        in_specs = [q_spec, k_spec, vt_spec, qm_spec, mk_spec, mvt_spec]
        args = (q, k, vt, qm, mk, mvt)
        scratch.append(pltpu.VMEM((N_MAIN_HEADS, SUBLANES, tq), F32))
    return pl.pallas_call(
        kernel,
        grid=(batch, nq),
        in_specs=in_specs,
        out_specs=o_spec,
        out_shape=jax.ShapeDtypeStruct((t, MAIN_WIDTH + MEM_WIDTH), BF16),
        scratch_shapes=scratch,
        compiler_params=_params("arbitrary", "arbitrary", flags=ATTN_FLAGS),
        name=kind + "_attn",
    )(*args)


def _causal_conv(u, prev, cw, cb):
    r8 = lax.broadcasted_iota(jnp.int32, (SUBLANES, 1), 0)
    out = cb
    for j in range(CONV_WIDTH - 1):
        shift = CONV_WIDTH - 1 - j
        rolled = pltpu.roll(u, shift, axis=0)
        top = jnp.where(r8 < shift, pltpu.roll(prev, shift, axis=0), rolled[:SUBLANES])
        shifted = jnp.concatenate([top, rolled[SUBLANES:]], axis=0)
        out = out + cw[j:j + 1, :] * shifted
    return out + cw[CONV_WIDTH - 1:CONV_WIDTH, :] * u


def _out_ffn_kernel(x_ref, o_ref, wout_ref, g_ref, wup_ref, cw_ref, cb_ref, wd_ref, fg_ref, out_ref,
                    act_ref, carry_ref, *, tiles_per_seq, tf, final_norm):
    tm = x_ref.shape[0]
    d_ff = wd_ref.shape[0]

    @pl.when(pl.program_id(0) % tiles_per_seq == 0)
    def _():
        carry_ref[...] = jnp.zeros_like(carry_ref)

    x1 = x_ref[...] + _dot(o_ref[...], wout_ref[...])
    h = (_rms_normalize(x1) * g_ref[...]).astype(BF16)
    u_all = _dot(h, wup_ref[...])
    for c in range(0, d_ff, tf):
        halves = []
        for half in range(2):
            cols = slice(half * d_ff + c, half * d_ff + c + tf)
            u = u_all[:, cols]
            prev = carry_ref[:, cols]
            carry_ref[:, cols] = u[tm - SUBLANES:]
            halves.append(_causal_conv(u, prev, cw_ref[:, cols], cb_ref[:, cols]))
        gate, val = halves
        act_ref[:, c:c + tf] = (gate * (1.0 / (1.0 + jnp.exp(-gate))) * val).astype(BF16)
    y = x1 + _dot(act_ref[...], wd_ref[...])
    if final_norm:
        y = _rms_normalize(y) * fg_ref[...]
    out_ref[...] = y


def _layer_spec(stacked, layer):
    return pl.BlockSpec((None,) + stacked.shape[1:], lambda *_: (layer, 0, 0), pipeline_mode=pl.Buffered(1))


def _out_ffn(x2, o, layer, w_out, g, w_up, conv_w, conv_b, w_down, final_g, seq, tm, tf, final_norm):
    t, d = x2.shape
    d_ff = w_down.shape[1]
    stacked = (w_out, g, w_up, conv_w, conv_b, w_down)
    kernel = functools.partial(_out_ffn_kernel, tiles_per_seq=seq // tm, tf=tf, final_norm=final_norm)
    return pl.pallas_call(
        kernel,
        grid=(t // tm,),
        in_specs=[_row_spec(tm, d), _row_spec(tm, o.shape[1])] + [_layer_spec(w, layer) for w in stacked]
        + [_full_spec((1, d))],
        out_specs=_row_spec(tm, d),
        out_shape=jax.ShapeDtypeStruct((t, d), F32),
        scratch_shapes=[pltpu.VMEM((tm, d_ff), BF16), pltpu.VMEM((SUBLANES, 2 * d_ff), F32)],
        compiler_params=_params("arbitrary"),
        name="out_ffn",
    )(x2, o, *stacked, final_g.reshape(1, d))


def kernel(x, mem, ln_mix_g, w_in_a, b_f_a, w_in_b, ln_kv_g, w_kv, ln_mem_g, w_memkv, w_out, ln_ffn_g,
           w_up, conv_w, conv_b, w_down, final_g):
    batch, seq, d = x.shape
    n_mem = mem.shape[1]
    depth = ln_mix_g.shape[0]
    n_a = w_in_a.shape[0]
    assert w_in_b.shape[0] == 1, "one stick-breaking layer reads the shared K/V"
    tm_proj = min(512, seq)
    tq_attn = min(256, seq)
    tm_ffn, tf = min(512, seq), 256

    x2 = x.reshape(batch * seq, d)
    mem2 = mem.reshape(batch * n_mem, d)
    ffn_weights = (w_out.astype(BF16), ln_ffn_g.reshape(depth, 1, d), w_up.astype(BF16), conv_w,
                   conv_b.reshape(depth, 1, -1), w_down.astype(BF16))
    for layer in range(depth):
        mk, mvt = _mem_proj(mem2, ln_mem_g[layer], w_memkv[layer], batch, n_mem)
        if layer < n_a:
            q, k, vt, qm, cp = _proj_a(x2, ln_mix_g[layer], w_in_a[layer], b_f_a[layer], seq, tm_proj, tq_attn)
            o = _attention("fox", q, k, cp, vt, qm, mk, mvt, batch, seq, n_mem, tq_attn)
        else:
            q, qm, k_sh, vt_sh = _proj_b(x2, ln_mix_g[layer], ln_kv_g, w_in_b[0], w_kv, seq, tm_proj, tq_attn)
            o = _attention("sb", q, k_sh, None, vt_sh, qm, mk, mvt, batch, seq, n_mem, tq_attn)
        x2 = _out_ffn(x2, o, layer, *ffn_weights, final_g, seq, tm_ffn, tf, final_norm=(layer == depth - 1))
    return x2.reshape(batch, seq, d)
```

```python
import functools

import jax
import jax.numpy as jnp
from jax import lax
from jax.experimental import pallas as pl
from jax.experimental.pallas import tpu as pltpu

HEAD_DIM = 64
N_MAIN_HEADS = 12
N_MEM_HEADS = 4
N_HEADS = N_MAIN_HEADS + N_MEM_HEADS
MAIN_WIDTH = N_MAIN_HEADS * HEAD_DIM
MEM_WIDTH = N_MEM_HEADS * HEAD_DIM
CONV_WIDTH = 3
EPS = 1e-6
Q_SCALE = HEAD_DIM ** -0.5
LOG2_E = 1.4426950408889634

LANES = 128
SUBLANES = 8
BF16_ROWS = 16
PAIR_WIDTH = 2 * HEAD_DIM
GATE_TERMS = 3
VMEM_LIMIT = 56 * 1024 * 1024

F32 = jnp.float32
BF16 = jnp.bfloat16

_NT = (((1,), (1,)), ((), ()))


def _params(*sem, flags=None):
    return pltpu.CompilerParams(dimension_semantics=sem, vmem_limit_bytes=VMEM_LIMIT, flags=flags)


def _dot(a, b):
    return jnp.dot(a, b, preferred_element_type=F32)


def _dot_nt(a, b):
    return lax.dot_general(a, b, _NT, preferred_element_type=F32)


def _rms_normalize(x):
    return x * lax.rsqrt(jnp.mean(x * x, axis=-1, keepdims=True) + EPS)


def _log_sigmoid(x):
    return jnp.minimum(x, 0.0) - jnp.log1p(jnp.exp(-jnp.abs(x)))


def _split3(x):
    hi = x.astype(BF16)
    r1 = x - hi.astype(F32)
    mid = r1.astype(BF16)
    return hi, mid, (r1 - mid.astype(F32)).astype(BF16)


MATMUL_ROWS = 256


def _store_chunks(h, w_ref, w_lo, out_ref, width, scale, chunk=256):
    rows = min(MATMUL_ROWS, h.shape[0])
    for r0 in range(0, h.shape[0], rows):
        for c in range(0, width, chunk):
            r = _dot(h[r0:r0 + rows], w_ref[:, w_lo + c:w_lo + c + chunk])
            if scale is not None:
                r = r * scale
            out_ref[r0:r0 + rows, c:c + chunk] = r.astype(out_ref.dtype)


def _store_transposed(h, w_ref, w_lo, out_ref, width, tk, chunk=256):
    for r in range(h.shape[0] // tk):
        for c in range(0, width, chunk):
            v = _dot(h[r * tk:(r + 1) * tk], w_ref[:, w_lo + c:w_lo + c + chunk])
            out_ref[0, r, c:c + chunk, :] = v.T.astype(out_ref.dtype)


def _proj_a_kernel(x_ref, g_ref, wqkv_ref, wqm_ref, wf_ref, bf_ref,
                   q_ref, k_ref, vt_ref, qm_ref, cp_ref, carry_ref, *, tiles_per_seq, tk):
    tm = x_ref.shape[0]

    @pl.when(pl.program_id(0) % tiles_per_seq == 0)
    def _():
        carry_ref[...] = jnp.zeros_like(carry_ref)

    h = (_rms_normalize(x_ref[...]) * g_ref[...]).astype(BF16)
    f_logit = _dot(h, wf_ref[...])
    _store_chunks(h, wqkv_ref, 0, q_ref, MAIN_WIDTH, Q_SCALE)

    parts = jnp.concatenate(_split3(_log_sigmoid(f_logit + bf_ref[...])), axis=1)
    row = lax.broadcasted_iota(jnp.int32, (tm, tm), 0)
    col = lax.broadcasted_iota(jnp.int32, (tm, tm), 1)
    sums = _dot(jnp.where(row >= col, 1.0, 0.0).astype(BF16), parts)
    _store_chunks(h, wqkv_ref, MAIN_WIDTH, k_ref, MAIN_WIDTH, None)

    c = (sums[:, :LANES] + sums[:, LANES:2 * LANES]) + sums[:, 2 * LANES:] + carry_ref[0:1, :]
    carry_ref[0:1, :] = c[tm - 1:tm, :]
    rr = lax.broadcasted_iota(jnp.int32, (GATE_TERMS * LANES, LANES), 0)
    cc = lax.broadcasted_iota(jnp.int32, (GATE_TERMS * LANES, LANES), 1)
    head, term = rr & (LANES - 1), rr >> 7
    place = jnp.where((cc == GATE_TERMS * head + term) & (head < N_MAIN_HEADS), 1.0, 0.0).astype(BF16)
    cp = _dot(jnp.concatenate(_split3(-c), axis=1), place)
    _store_transposed(h, wqkv_ref, 2 * MAIN_WIDTH, vt_ref, MAIN_WIDTH, tk)
    _store_chunks(h, wqm_ref, 0, qm_ref, MEM_WIDTH, Q_SCALE)
    cp_ref[...] = cp.astype(cp_ref.dtype)


def _proj_b_kernel(x_ref, gmix_ref, gkv_ref, win_ref, wkv_ref, q_ref, qm_ref, k_ref, vt_ref, *, tk):
    xn = _rms_normalize(x_ref[...])
    h = (xn * gmix_ref[...]).astype(BF16)
    _store_chunks(h, win_ref, 0, q_ref, MAIN_WIDTH, Q_SCALE)
    _store_chunks(h, win_ref, MAIN_WIDTH, qm_ref, MEM_WIDTH, Q_SCALE)
    hk = (xn * gkv_ref[...]).astype(BF16)
    _store_chunks(hk, wkv_ref, 0, k_ref, MAIN_WIDTH, None)
    _store_transposed(hk, wkv_ref, MAIN_WIDTH, vt_ref, MAIN_WIDTH, tk)


def _mem_proj_kernel(x_ref, g_ref, w_ref, mk_ref, mvt_ref):
    n_mem = x_ref.shape[0]
    h = (_rms_normalize(x_ref[...]) * g_ref[...]).astype(BF16)
    _store_chunks(h, w_ref, 0, mk_ref, MEM_WIDTH, None)
    _store_transposed(h, w_ref, MEM_WIDTH, mvt_ref, MEM_WIDTH, n_mem)


def _row_spec(tm, width):
    return pl.BlockSpec((tm, width), lambda i: (i, 0))


def _full_spec(shape):
    return pl.BlockSpec(shape, lambda *_: (0,) * len(shape))


def _vt_out(t, seq, tm, tk):
    tiles = seq // tm
    shape = (t // seq, seq // tk, MAIN_WIDTH, tk)
    spec = pl.BlockSpec((1, tm // tk, MAIN_WIDTH, tk), lambda i: (i // tiles, i % tiles, 0, 0))
    return spec, jax.ShapeDtypeStruct(shape, BF16)


def _proj_a(x2, g, w_in, b_f, seq, tm, tk):
    t, d = x2.shape
    n_f = N_MAIN_HEADS
    wqkv = w_in[:, :3 * MAIN_WIDTH].astype(BF16)
    wf = jnp.pad(w_in[:, 3 * MAIN_WIDTH:3 * MAIN_WIDTH + n_f], ((0, 0), (0, LANES - n_f))).astype(BF16)
    wqm = w_in[:, 3 * MAIN_WIDTH + n_f:].astype(BF16)
    bf = jnp.pad(b_f, (0, LANES - n_f)).reshape(1, LANES)
    vt_spec, vt_shape = _vt_out(t, seq, tm, tk)
    kernel = functools.partial(_proj_a_kernel, tiles_per_seq=seq // tm, tk=tk)
    return pl.pallas_call(
        kernel,
        grid=(t // tm,),
        in_specs=[_row_spec(tm, d), _full_spec((1, d)), _full_spec(wqkv.shape), _full_spec(wqm.shape),
                  _full_spec(wf.shape), _full_spec((1, LANES))],
        out_specs=[_row_spec(tm, MAIN_WIDTH), _row_spec(tm, MAIN_WIDTH), vt_spec,
                   _row_spec(tm, MEM_WIDTH), _row_spec(tm, LANES)],
        out_shape=[jax.ShapeDtypeStruct((t, MAIN_WIDTH), BF16), jax.ShapeDtypeStruct((t, MAIN_WIDTH), BF16),
                   vt_shape, jax.ShapeDtypeStruct((t, MEM_WIDTH), BF16), jax.ShapeDtypeStruct((t, LANES), BF16)],
        scratch_shapes=[pltpu.VMEM((SUBLANES, LANES), F32)],
        compiler_params=_params("arbitrary"),
        name="proj_a",
    )(x2, g.reshape(1, d), wqkv, wqm, wf, bf)


def _proj_b(x2, g_mix, g_kv, w_in, w_kv, seq, tm, tk):
    t, d = x2.shape
    win = w_in.astype(BF16)
    wkv = w_kv.astype(BF16)
    vt_spec, vt_shape = _vt_out(t, seq, tm, tk)
    return pl.pallas_call(
        functools.partial(_proj_b_kernel, tk=tk),
        grid=(t // tm,),
        in_specs=[_row_spec(tm, d), _full_spec((1, d)), _full_spec((1, d)), _full_spec(win.shape),
                  _full_spec(wkv.shape)],
        out_specs=[_row_spec(tm, MAIN_WIDTH), _row_spec(tm, MEM_WIDTH), _row_spec(tm, MAIN_WIDTH), vt_spec],
        out_shape=[jax.ShapeDtypeStruct((t, MAIN_WIDTH), BF16), jax.ShapeDtypeStruct((t, MEM_WIDTH), BF16),
                   jax.ShapeDtypeStruct((t, MAIN_WIDTH), BF16), vt_shape],
        compiler_params=_params("arbitrary"),
        name="proj_b",
    )(x2, g_mix.reshape(1, d), g_kv.reshape(1, d), win, wkv)


def _mem_proj(mem2, g, w, batch, n_mem):
    d = mem2.shape[1]
    wb = w.astype(BF16)
    return pl.pallas_call(
        _mem_proj_kernel,
        grid=(batch,),
        in_specs=[_row_spec(n_mem, d), _full_spec((1, d)), _full_spec(wb.shape)],
        out_specs=[_row_spec(n_mem, MEM_WIDTH), pl.BlockSpec((1, 1, MEM_WIDTH, n_mem), lambda i: (i, 0, 0, 0))],
        out_shape=[jax.ShapeDtypeStruct((batch * n_mem, MEM_WIDTH), BF16),
                   jax.ShapeDtypeStruct((batch, 1, MEM_WIDTH, n_mem), BF16)],
        compiler_params=_params("arbitrary"),
        name="mem_proj",
    )(mem2, g.reshape(1, d), wb)


ACC_ROWS = HEAD_DIM + BF16_ROWS
ATTN_FLAGS = None
RUN_DEAD = 105.0
LOOKAHEAD = 6
SCORE_AHEAD = 3
SUFFIX_AHEAD = 3


def _split_heads_into(q_ref, dst_ref, first, n_heads, gate_rows):
    tq = q_ref.shape[0]
    row = lax.broadcasted_iota(jnp.int32, (PAIR_WIDTH, 1), 0)
    low = row < HEAD_DIM
    for hp in range(n_heads // 2):
        qt = q_ref[:, hp * PAIR_WIDTH:(hp + 1) * PAIR_WIDTH].astype(F32).T
        for sub in range(2):
            head = first + 2 * hp + sub
            keep = low if sub == 0 else jnp.logical_not(low)
            dst_ref[head, 0:PAIR_WIDTH, :] = jnp.where(keep, qt, 0.0).astype(dst_ref.dtype)
            if gate_rows:
                pick = (row >= GATE_TERMS * head) & (row < GATE_TERMS * (head + 1))
                dst_ref[head, PAIR_WIDTH:, :] = jnp.broadcast_to(
                    jnp.where(pick, 1.0, 0.0).astype(dst_ref.dtype), (PAIR_WIDTH, tq))


def _values_with_ones(vt2, sub):
    vt = vt2[sub * HEAD_DIM:(sub + 1) * HEAD_DIM]
    return jnp.concatenate([vt, jnp.ones((BF16_ROWS, vt.shape[1]), vt.dtype)], axis=0)


def _softmax_update(s, head, vt_h, m_ref, acc_ref):
    m_prev = m_ref[head][0:1]
    m_new = jnp.maximum(m_prev, jnp.max(s, axis=0, keepdims=True))
    p = jnp.exp(s - m_new).astype(BF16)
    acc_ref[head] = jnp.exp(m_prev - m_new) * acc_ref[head] + _dot(vt_h, p)
    m_ref[head] = jnp.broadcast_to(m_new, m_ref.shape[1:])


def _softmax_finish(acc_ref, first, n_heads, o_ref, col0):
    for hp in range(n_heads // 2):
        outs = []
        for sub in range(2):
            acc = acc_ref[first + 2 * hp + sub]
            outs.append(acc[:HEAD_DIM] / acc[HEAD_DIM:HEAD_DIM + 1])
        o_ref[:, col0 + hp * PAIR_WIDTH:col0 + (hp + 1) * PAIR_WIDTH] = (
            jnp.concatenate(outs, axis=0).T.astype(o_ref.dtype))


def _init_softmax_state(m_ref, acc_ref):
    m_ref[...] = jnp.full_like(m_ref, -jnp.inf)
    acc_ref[...] = jnp.zeros_like(acc_ref)


def _pair_cols(head):
    return slice((head // 2) * PAIR_WIDTH, (head // 2 + 1) * PAIR_WIDTH)


def _softmax_pipeline(items, m_ref, acc_ref):
    pending = [item[0]() for item in items[:LOOKAHEAD]]
    for n, (_, values_fn, head) in enumerate(items):
        s = pending.pop(0)
        if n + LOOKAHEAD < len(items):
            pending.append(items[n + LOOKAHEAD][0]())
        _softmax_update(s, head, values_fn(), m_ref, acc_ref)


def _memory_items(qs_ref, mk_ref, mvt_ref):
    def score(i):
        return _dot(mk_ref[:, _pair_cols(i)], qs_ref[N_MAIN_HEADS + i, 0:PAIR_WIDTH, :])

    def values(i):
        return _values_with_ones(mvt_ref[0, 0, _pair_cols(i), :], i % 2)

    return [(functools.partial(score, i), functools.partial(values, i), N_MAIN_HEADS + i)
            for i in range(N_MEM_HEADS)]


def _fox_attn_kernel(q_ref, k_ref, cp_ref, vt_ref, qm_ref, mk_ref, mvt_ref, o_ref, qs_ref, m_ref, acc_ref, *, tq):
    qi = pl.program_id(1)
    tk = tq
    _split_heads_into(q_ref, qs_ref, 0, N_MAIN_HEADS, True)
    _split_heads_into(qm_ref, qs_ref, N_MAIN_HEADS, N_MEM_HEADS, False)
    _init_softmax_state(m_ref, acc_ref)

    def block_items(kb, causal):
        ks = pl.multiple_of(kb * tk, tk)

        def score(head):
            kk = jnp.concatenate([k_ref[pl.ds(ks, tk), _pair_cols(head)], cp_ref[pl.ds(ks, tk), :]], axis=1)
            s = _dot(kk, qs_ref[head])
            return s if causal is None else jnp.where(causal, s, -jnp.inf)

        def values(head):
            return _values_with_ones(vt_ref[0, kb, _pair_cols(head), :], head % 2)

        return [(functools.partial(score, h), functools.partial(values, h), h) for h in range(N_MAIN_HEADS)]

    def two_blocks(it, carry):
        _softmax_pipeline(block_items(2 * it, None) + block_items(2 * it + 1, None), m_ref, acc_ref)
        return carry

    lax.fori_loop(0, qi >> 1, two_blocks, 0)
    key = lax.broadcasted_iota(jnp.int32, (tk, tq), 0)
    qry = lax.broadcasted_iota(jnp.int32, (tk, tq), 1)
    tail = block_items(qi, key <= qry) + _memory_items(qs_ref, mk_ref, mvt_ref)

    @pl.when((qi & 1) == 1)
    def _():
        _softmax_pipeline(block_items(qi - 1, None) + tail, m_ref, acc_ref)

    @pl.when((qi & 1) == 0)
    def _():
        _softmax_pipeline(tail, m_ref, acc_ref)

    _softmax_finish(acc_ref, 0, N_MAIN_HEADS, o_ref, 0)
    _softmax_finish(acc_ref, N_MAIN_HEADS, N_MEM_HEADS, o_ref, MAIN_WIDTH)


def _sb_attn_kernel(q_ref, k_ref, vt_ref, qm_ref, mk_ref, mvt_ref, o_ref, qs_ref, m_ref, acc_ref, run_ref, *, tq):
    qi = pl.program_id(1)
    tk = tq
    _split_heads_into(q_ref, qs_ref, 0, N_MAIN_HEADS, False)
    _split_heads_into(qm_ref, qs_ref, N_MAIN_HEADS, N_MEM_HEADS, False)
    _init_softmax_state(m_ref, acc_ref)
    run_ref[...] = jnp.zeros_like(run_ref)

    ss = lax.broadcasted_iota(jnp.int32, (LANES + BF16_ROWS, 2 * LANES), 0)
    jj = lax.broadcasted_iota(jnp.int32, (LANES + BF16_ROWS, 2 * LANES), 1) & (LANES - 1)
    suffix = jnp.where((jj > ss) | (ss >= LANES), 1.0, 0.0).astype(BF16)

    n_chunks = tk // LANES

    def stage_pipeline(items):
        def score_stage(kb, causal, head):
            ks = pl.multiple_of(kb * tk, tk)
            z = _dot(k_ref[pl.ds(ks, tk), _pair_cols(head)], qs_ref[head])
            log_beta, split = [], []
            for c in range(n_chunks):
                zc = z[c * LANES:(c + 1) * LANES]
                nl = jnp.maximum(zc, 0.0) + jnp.log(1.0 + jnp.exp2(jnp.abs(zc) * -LOG2_E))
                log_beta.append(zc - nl)
                if causal is not None:
                    nl = jnp.where(causal[c], nl, 0.0)
                hi = nl.astype(BF16)
                split.append(jnp.concatenate([hi, (nl - hi.astype(F32)).astype(BF16)], axis=0))
            return log_beta, split

        def suffix_stage(state):
            log_beta, split = state
            return log_beta, [_dot(suffix, sp) for sp in split]

        def value_stage(kb, causal, head, state):
            log_beta, sums = state
            run = run_ref[head][0:1]
            a_chunks = [None] * n_chunks
            for c in reversed(range(n_chunks)):
                a = jnp.exp(log_beta[c] - (sums[c][:LANES] + run))
                if causal is not None:
                    a = jnp.where(causal[c], a, 0.0)
                a_chunks[c] = a.astype(BF16)
                run = run + sums[c][LANES:LANES + 1]
            run_ref[head] = jnp.broadcast_to(run, run_ref.shape[1:])
            vt = vt_ref[0, kb, _pair_cols(head), :][(head % 2) * HEAD_DIM:(head % 2 + 1) * HEAD_DIM]
            acc_ref[head, 0:HEAD_DIM] += _dot(vt, jnp.concatenate(a_chunks, axis=0))

        scored, summed = {}, {}
        for step in range(len(items) + SCORE_AHEAD + SUFFIX_AHEAD):
            if step < len(items):
                scored[step] = score_stage(*items[step])
            i2 = step - SCORE_AHEAD
            if 0 <= i2 < len(items):
                summed[i2] = suffix_stage(scored.pop(i2))
            i3 = i2 - SUFFIX_AHEAD
            if 0 <= i3 < len(items):
                value_stage(*items[i3], summed.pop(i3))

    def block_items(kb, causal):
        return [(kb, causal, head) for head in range(N_MAIN_HEADS)]

    key = lax.broadcasted_iota(jnp.int32, (LANES, tq), 0)
    qry = lax.broadcasted_iota(jnp.int32, (LANES, tq), 1)
    diagonal = block_items(qi, [(key + c * LANES) < qry for c in range(n_chunks)])
    memory = _memory_items(qs_ref, mk_ref, mvt_ref)

    @pl.when(qi > 0)
    def _():
        stage_pipeline(diagonal + block_items(qi - 1, None))
        _softmax_pipeline(memory, m_ref, acc_ref)

    @pl.when(qi == 0)
    def _():
        stage_pipeline(diagonal)
        _softmax_pipeline(memory, m_ref, acc_ref)

    def any_weight_left():
        least = functools.reduce(jnp.minimum, [run_ref[h] for h in range(N_MAIN_HEADS)])
        return (jnp.min(least) < RUN_DEAD).astype(jnp.int32)

    def cond(carry):
        it, live = carry
        return (it < qi) & (live > 0)

    def body(carry):
        it, _ = carry
        stage_pipeline(block_items(qi - 1 - it, None))
        return it + 1, any_weight_left()

    lax.while_loop(cond, body, (jnp.int32(1), any_weight_left()))
    for hp in range(N_MAIN_HEADS // 2):
        pair = jnp.concatenate([acc_ref[2 * hp, 0:HEAD_DIM], acc_ref[2 * hp + 1, 0:HEAD_DIM]], axis=0)
        o_ref[:, hp * PAIR_WIDTH:(hp + 1) * PAIR_WIDTH] = pair.T.astype(o_ref.dtype)
    _softmax_finish(acc_ref, N_MAIN_HEADS, N_MEM_HEADS, o_ref, MAIN_WIDTH)


def _attention(kind, q, k, cp, vt, qm, mk, mvt, batch, seq, n_mem, tq):
    t = q.shape[0]
    nq = seq // tq
    q_spec = pl.BlockSpec((tq, MAIN_WIDTH), lambda b, i: (b * nq + i, 0))
    k_spec = pl.BlockSpec((seq, MAIN_WIDTH), lambda b, i: (b, 0))
    vt_spec = pl.BlockSpec((1,) + vt.shape[1:], lambda b, i: (b, 0, 0, 0))
    qm_spec = pl.BlockSpec((tq, MEM_WIDTH), lambda b, i: (b * nq + i, 0))
    mk_spec = pl.BlockSpec((n_mem, MEM_WIDTH), lambda b, i: (b, 0))
    mvt_spec = pl.BlockSpec((1, 1, MEM_WIDTH, n_mem), lambda b, i: (b, 0, 0, 0))
    o_spec = pl.BlockSpec((tq, MAIN_WIDTH + MEM_WIDTH), lambda b, i: (b * nq + i, 0))
    q_rows = 2 * PAIR_WIDTH if kind == "fox" else PAIR_WIDTH
    scratch = [pltpu.VMEM((N_HEADS, q_rows, tq), BF16),
               pltpu.VMEM((N_HEADS, SUBLANES, tq), F32),
               pltpu.VMEM((N_HEADS, ACC_ROWS, tq), F32)]
    if kind == "fox":
        cp_spec = pl.BlockSpec((seq, LANES), lambda b, i: (b, 0))
        kernel = functools.partial(_fox_attn_kernel, tq=tq)
        in_specs = [q_spec, k_spec, cp_spec, vt_spec, qm_spec, mk_spec, mvt_spec]
        args = (q, k, cp, vt, qm, mk, mvt)
    else:
        kernel = functools.partial(_sb_attn_kernel, tq=tq)
        in_specs = [q_spec, k_spec, vt_spec, qm_spec, mk_spec, mvt_spec]
        args = (q, k, vt, qm, mk, mvt)
        scratch.append(pltpu.VMEM((N_MAIN_HEADS, SUBLANES, tq), F32))
    return pl.pallas_call(
        kernel,
        grid=(batch, nq),
        in_specs=in_specs,
        out_specs=o_spec,
        out_shape=jax.ShapeDtypeStruct((t, MAIN_WIDTH + MEM_WIDTH), BF16),
        scratch_shapes=scratch,
        compiler_params=_params("arbitrary", "arbitrary", flags=ATTN_FLAGS),
        name=kind + "_attn",
    )(*args)


def _causal_conv(u, prev, cw, cb):
    r8 = lax.broadcasted_iota(jnp.int32, (SUBLANES, 1), 0)
    out = cb
    for j in range(CONV_WIDTH - 1):
        shift = CONV_WIDTH - 1 - j
        rolled = pltpu.roll(u, shift, axis=0)
        top = jnp.where(r8 < shift, pltpu.roll(prev, shift, axis=0), rolled[:SUBLANES])
        shifted = jnp.concatenate([top, rolled[SUBLANES:]], axis=0)
        out = out + cw[j:j + 1, :] * shifted
    return out + cw[CONV_WIDTH - 1:CONV_WIDTH, :] * u


def _out_ffn_kernel(x_ref, o_ref, wout_ref, g_ref, wup_ref, cw_ref, cb_ref, wd_ref, fg_ref, out_ref,
                    u_ref, act_ref, carry_ref, *, tiles_per_seq, tf, final_norm):
    tm = x_ref.shape[0]
    d_ff = wd_ref.shape[0]

    @pl.when(pl.program_id(0) % tiles_per_seq == 0)
    def _():
        carry_ref[...] = jnp.zeros_like(carry_ref)

    x1 = x_ref[...] + _dot(o_ref[...], wout_ref[...])
    h = (_rms_normalize(x1) * g_ref[...]).astype(BF16)
    for r in range(0, tm, tm // 2):
        u_ref[r:r + tm // 2] = _dot(h[r:r + tm // 2], wup_ref[...])
    for c in range(0, d_ff, tf):
        halves = []
        for half in range(2):
            cols = slice(half * d_ff + c, half * d_ff + c + tf)
            u = u_ref[:, cols]
            prev = carry_ref[:, cols]
            carry_ref[:, cols] = u[tm - SUBLANES:]
            halves.append(_causal_conv(u, prev, cw_ref[:, cols], cb_ref[:, cols]))
        gate, val = halves
        act_ref[:, c:c + tf] = (gate * (1.0 / (1.0 + jnp.exp(-gate))) * val).astype(BF16)
    y = x1 + _dot(act_ref[...], wd_ref[...])
    if final_norm:
        y = _rms_normalize(y) * fg_ref[...]
    out_ref[...] = y


def _layer_spec(stacked, layer):
    return pl.BlockSpec((None,) + stacked.shape[1:], lambda *_: (layer, 0, 0), pipeline_mode=pl.Buffered(1))


def _out_ffn(x2, o, layer, w_out, g, w_up, conv_w, conv_b, w_down, final_g, seq, tm, tf, final_norm):
    t, d = x2.shape
    d_ff = w_down.shape[1]
    stacked = (w_out, g, w_up, conv_w, conv_b, w_down)
    kernel = functools.partial(_out_ffn_kernel, tiles_per_seq=seq // tm, tf=tf, final_norm=final_norm)
    return pl.pallas_call(
        kernel,
        grid=(t // tm,),
        in_specs=[_row_spec(tm, d), _row_spec(tm, o.shape[1])] + [_layer_spec(w, layer) for w in stacked]
        + [_full_spec((1, d))],
        out_specs=_row_spec(tm, d),
        out_shape=jax.ShapeDtypeStruct((t, d), F32),
        scratch_shapes=[pltpu.VMEM((tm, 2 * d_ff), F32), pltpu.VMEM((tm, d_ff), BF16),
                        pltpu.VMEM((SUBLANES, 2 * d_ff), F32)],
        compiler_params=_params("arbitrary"),
        name="out_ffn",
    )(x2, o, *stacked, final_g.reshape(1, d))


def kernel(x, mem, ln_mix_g, w_in_a, b_f_a, w_in_b, ln_kv_g, w_kv, ln_mem_g, w_memkv, w_out, ln_ffn_g,
           w_up, conv_w, conv_b, w_down, final_g):
    batch, seq, d = x.shape
    n_mem = mem.shape[1]
    depth = ln_mix_g.shape[0]
    n_a = w_in_a.shape[0]
    assert w_in_b.shape[0] == 1, "one stick-breaking layer reads the shared K/V"
    tm_proj = min(512, seq)
    tq_attn = min(256, seq)
    tm_ffn, tf = min(512, seq), 256

    x2 = x.reshape(batch * seq, d)
    mem2 = mem.reshape(batch * n_mem, d)
    ffn_weights = (w_out.astype(BF16), ln_ffn_g.reshape(depth, 1, d), w_up.astype(BF16), conv_w,
                   conv_b.reshape(depth, 1, -1), w_down.astype(BF16))
    for layer in range(depth):
        mk, mvt = _mem_proj(mem2, ln_mem_g[layer], w_memkv[layer], batch, n_mem)
        if layer < n_a:
            q, k, vt, qm, cp = _proj_a(x2, ln_mix_g[layer], w_in_a[layer], b_f_a[layer], seq, tm_proj, tq_attn)
            o = _attention("fox", q, k, cp, vt, qm, mk, mvt, batch, seq, n_mem, tq_attn)
        else:
            q, qm, k_sh, vt_sh = _proj_b(x2, ln_mix_g[layer], ln_kv_g, w_in_b[0], w_kv, seq, tm_proj, tq_attn)
            o = _attention("sb", q, k_sh, None, vt_sh, qm, mk, mvt, batch, seq, n_mem, tq_attn)
        x2 = _out_ffn(x2, o, layer, *ffn_weights, final_g, seq, tm_ffn, tf, final_norm=(layer == depth - 1))
    return x2.reshape(batch, seq, d)
```

```python
import functools

import jax
import jax.numpy as jnp
from jax import lax
from jax.experimental import pallas as pl
from jax.experimental.pallas import tpu as pltpu

HEAD_DIM = 64
N_MAIN_HEADS = 12
N_MEM_HEADS = 4
N_HEADS = N_MAIN_HEADS + N_MEM_HEADS
MAIN_WIDTH = N_MAIN_HEADS * HEAD_DIM
MEM_WIDTH = N_MEM_HEADS * HEAD_DIM
CONV_WIDTH = 3
EPS = 1e-6
Q_SCALE = HEAD_DIM ** -0.5
LOG2_E = 1.4426950408889634

LANES = 128
SUBLANES = 8
BF16_ROWS = 16
PAIR_WIDTH = 2 * HEAD_DIM
GATE_TERMS = 3
VMEM_LIMIT = 56 * 1024 * 1024

F32 = jnp.float32
BF16 = jnp.bfloat16

_NT = (((1,), (1,)), ((), ()))


def _params(*sem):
    return pltpu.CompilerParams(dimension_semantics=sem, vmem_limit_bytes=VMEM_LIMIT)


def _dot(a, b):
    return jnp.dot(a, b, preferred_element_type=F32)


def _dot_nt(a, b):
    return lax.dot_general(a, b, _NT, preferred_element_type=F32)


def _rms_normalize(x):
    return x * lax.rsqrt(jnp.mean(x * x, axis=-1, keepdims=True) + EPS)


def _log_sigmoid(x):
    return jnp.minimum(x, 0.0) - jnp.log1p(jnp.exp(-jnp.abs(x)))


def _split3(x):
    hi = x.astype(BF16)
    r1 = x - hi.astype(F32)
    mid = r1.astype(BF16)
    return hi, mid, (r1 - mid.astype(F32)).astype(BF16)


MATMUL_ROWS = 256


def _store_chunks(h, w_ref, w_lo, out_ref, width, scale, chunk=256):
    rows = min(MATMUL_ROWS, h.shape[0])
    for r0 in range(0, h.shape[0], rows):
        for c in range(0, width, chunk):
            r = _dot(h[r0:r0 + rows], w_ref[:, w_lo + c:w_lo + c + chunk])
            if scale is not None:
                r = r * scale
            out_ref[r0:r0 + rows, c:c + chunk] = r.astype(out_ref.dtype)


def _store_transposed(h, w_ref, w_lo, out_ref, width, tk, chunk=256):
    for r in range(h.shape[0] // tk):
        for c in range(0, width, chunk):
            v = _dot(h[r * tk:(r + 1) * tk], w_ref[:, w_lo + c:w_lo + c + chunk])
            out_ref[0, r, c:c + chunk, :] = v.T.astype(out_ref.dtype)


def _proj_a_kernel(x_ref, g_ref, wqkv_ref, wqm_ref, wf_ref, bf_ref,
                   q_ref, k_ref, vt_ref, qm_ref, cp_ref, carry_ref, *, tiles_per_seq, tk):
    tm = x_ref.shape[0]

    @pl.when(pl.program_id(0) % tiles_per_seq == 0)
    def _():
        carry_ref[...] = jnp.zeros_like(carry_ref)

    h = (_rms_normalize(x_ref[...]) * g_ref[...]).astype(BF16)
    f_logit = _dot(h, wf_ref[...])
    _store_chunks(h, wqkv_ref, 0, q_ref, MAIN_WIDTH, Q_SCALE)

    parts = jnp.concatenate(_split3(_log_sigmoid(f_logit + bf_ref[...])), axis=1)
    row = lax.broadcasted_iota(jnp.int32, (tm, tm), 0)
    col = lax.broadcasted_iota(jnp.int32, (tm, tm), 1)
    sums = _dot(jnp.where(row >= col, 1.0, 0.0).astype(BF16), parts)
    _store_chunks(h, wqkv_ref, MAIN_WIDTH, k_ref, MAIN_WIDTH, None)

    c = (sums[:, :LANES] + sums[:, LANES:2 * LANES]) + sums[:, 2 * LANES:] + carry_ref[0:1, :]
    carry_ref[0:1, :] = c[tm - 1:tm, :]
    rr = lax.broadcasted_iota(jnp.int32, (GATE_TERMS * LANES, LANES), 0)
    cc = lax.broadcasted_iota(jnp.int32, (GATE_TERMS * LANES, LANES), 1)
    head, term = rr & (LANES - 1), rr >> 7
    place = jnp.where((cc == GATE_TERMS * head + term) & (head < N_MAIN_HEADS), 1.0, 0.0).astype(BF16)
    cp = _dot(jnp.concatenate(_split3(-c), axis=1), place)
    _store_transposed(h, wqkv_ref, 2 * MAIN_WIDTH, vt_ref, MAIN_WIDTH, tk)
    _store_chunks(h, wqm_ref, 0, qm_ref, MEM_WIDTH, Q_SCALE)
    cp_ref[...] = cp.astype(cp_ref.dtype)


def _proj_b_kernel(x_ref, gmix_ref, gkv_ref, win_ref, wkv_ref, q_ref, qm_ref, k_ref, vt_ref, *, tk):
    xn = _rms_normalize(x_ref[...])
    h = (xn * gmix_ref[...]).astype(BF16)
    _store_chunks(h, win_ref, 0, q_ref, MAIN_WIDTH, Q_SCALE)
    _store_chunks(h, win_ref, MAIN_WIDTH, qm_ref, MEM_WIDTH, Q_SCALE)
    hk = (xn * gkv_ref[...]).astype(BF16)
    _store_chunks(hk, wkv_ref, 0, k_ref, MAIN_WIDTH, None)
    _store_transposed(hk, wkv_ref, MAIN_WIDTH, vt_ref, MAIN_WIDTH, tk)


def _mem_proj_kernel(x_ref, g_ref, w_ref, mk_ref, mvt_ref):
    n_mem = x_ref.shape[0]
    h = (_rms_normalize(x_ref[...]) * g_ref[...]).astype(BF16)
    _store_chunks(h, w_ref, 0, mk_ref, MEM_WIDTH, None)
    _store_transposed(h, w_ref, MEM_WIDTH, mvt_ref, MEM_WIDTH, n_mem)


def _row_spec(tm, width):
    return pl.BlockSpec((tm, width), lambda i: (i, 0))


def _full_spec(shape):
    return pl.BlockSpec(shape, lambda *_: (0,) * len(shape))


def _vt_out(t, seq, tm, tk):
    tiles = seq // tm
    shape = (t // seq, seq // tk, MAIN_WIDTH, tk)
    spec = pl.BlockSpec((1, tm // tk, MAIN_WIDTH, tk), lambda i: (i // tiles, i % tiles, 0, 0))
    return spec, jax.ShapeDtypeStruct(shape, BF16)


def _proj_a(x2, g, w_in, b_f, seq, tm, tk):
    t, d = x2.shape
    n_f = N_MAIN_HEADS
    wqkv = w_in[:, :3 * MAIN_WIDTH].astype(BF16)
    wf = jnp.pad(w_in[:, 3 * MAIN_WIDTH:3 * MAIN_WIDTH + n_f], ((0, 0), (0, LANES - n_f))).astype(BF16)
    wqm = w_in[:, 3 * MAIN_WIDTH + n_f:].astype(BF16)
    bf = jnp.pad(b_f, (0, LANES - n_f)).reshape(1, LANES)
    vt_spec, vt_shape = _vt_out(t, seq, tm, tk)
    kernel = functools.partial(_proj_a_kernel, tiles_per_seq=seq // tm, tk=tk)
    return pl.pallas_call(
        kernel,
        grid=(t // tm,),
        in_specs=[_row_spec(tm, d), _full_spec((1, d)), _full_spec(wqkv.shape), _full_spec(wqm.shape),
                  _full_spec(wf.shape), _full_spec((1, LANES))],
        out_specs=[_row_spec(tm, MAIN_WIDTH), _row_spec(tm, MAIN_WIDTH), vt_spec,
                   _row_spec(tm, MEM_WIDTH), _row_spec(tm, LANES)],
        out_shape=[jax.ShapeDtypeStruct((t, MAIN_WIDTH), BF16), jax.ShapeDtypeStruct((t, MAIN_WIDTH), BF16),
                   vt_shape, jax.ShapeDtypeStruct((t, MEM_WIDTH), BF16), jax.ShapeDtypeStruct((t, LANES), BF16)],
        scratch_shapes=[pltpu.VMEM((SUBLANES, LANES), F32)],
        compiler_params=_params("arbitrary"),
        name="proj_a",
    )(x2, g.reshape(1, d), wqkv, wqm, wf, bf)


def _proj_b(x2, g_mix, g_kv, w_in, w_kv, seq, tm, tk):
    t, d = x2.shape
    win = w_in.astype(BF16)
    wkv = w_kv.astype(BF16)
    vt_spec, vt_shape = _vt_out(t, seq, tm, tk)
    return pl.pallas_call(
        functools.partial(_proj_b_kernel, tk=tk),
        grid=(t // tm,),
        in_specs=[_row_spec(tm, d), _full_spec((1, d)), _full_spec((1, d)), _full_spec(win.shape),
                  _full_spec(wkv.shape)],
        out_specs=[_row_spec(tm, MAIN_WIDTH), _row_spec(tm, MEM_WIDTH), _row_spec(tm, MAIN_WIDTH), vt_spec],
        out_shape=[jax.ShapeDtypeStruct((t, MAIN_WIDTH), BF16), jax.ShapeDtypeStruct((t, MEM_WIDTH), BF16),
                   jax.ShapeDtypeStruct((t, MAIN_WIDTH), BF16), vt_shape],
        compiler_params=_params("arbitrary"),
        name="proj_b",
    )(x2, g_mix.reshape(1, d), g_kv.reshape(1, d), win, wkv)


def _mem_proj(mem2, g, w, batch, n_mem):
    d = mem2.shape[1]
    wb = w.astype(BF16)
    return pl.pallas_call(
        _mem_proj_kernel,
        grid=(batch,),
        in_specs=[_row_spec(n_mem, d), _full_spec((1, d)), _full_spec(wb.shape)],
        out_specs=[_row_spec(n_mem, MEM_WIDTH), pl.BlockSpec((1, 1, MEM_WIDTH, n_mem), lambda i: (i, 0, 0, 0))],
        out_shape=[jax.ShapeDtypeStruct((batch * n_mem, MEM_WIDTH), BF16),
                   jax.ShapeDtypeStruct((batch, 1, MEM_WIDTH, n_mem), BF16)],
        compiler_params=_params("arbitrary"),
        name="mem_proj",
    )(mem2, g.reshape(1, d), wb)


ACC_ROWS = HEAD_DIM + BF16_ROWS
RUN_DEAD = 105.0
LOOKAHEAD = 6
SCORE_AHEAD = 3
SUFFIX_AHEAD = 3


def _split_heads_into(q_ref, dst_ref, first, n_heads):
    row = lax.broadcasted_iota(jnp.int32, (PAIR_WIDTH, 1), 0)
    low = row < HEAD_DIM
    for hp in range(n_heads // 2):
        qt = q_ref[:, hp * PAIR_WIDTH:(hp + 1) * PAIR_WIDTH].astype(F32).T
        for sub in range(2):
            keep = low if sub == 0 else jnp.logical_not(low)
            dst_ref[first + 2 * hp + sub, 0:PAIR_WIDTH, :] = jnp.where(keep, qt, 0.0).astype(dst_ref.dtype)


def _write_gate_rows(dst_ref):
    tq = dst_ref.shape[2]
    row = lax.broadcasted_iota(jnp.int32, (PAIR_WIDTH, 1), 0)
    for head in range(N_MAIN_HEADS):
        pick = (row >= GATE_TERMS * head) & (row < GATE_TERMS * (head + 1))
        dst_ref[head, PAIR_WIDTH:, :] = jnp.broadcast_to(
            jnp.where(pick, 1.0, 0.0).astype(dst_ref.dtype), (PAIR_WIDTH, tq))


def _values_with_ones(vt2, sub):
    vt = vt2[sub * HEAD_DIM:(sub + 1) * HEAD_DIM]
    return jnp.concatenate([vt, jnp.ones((BF16_ROWS, vt.shape[1]), vt.dtype)], axis=0)


def _softmax_update(s, head, vt_h, m_ref, acc_ref):
    m_prev = m_ref[head][0:1]
    m_new = jnp.maximum(m_prev, jnp.max(s, axis=0, keepdims=True))
    p = jnp.exp(s - m_new).astype(BF16)
    acc_ref[head] = jnp.exp(m_prev - m_new) * acc_ref[head] + _dot(vt_h, p)
    m_ref[head] = jnp.broadcast_to(m_new, m_ref.shape[1:])


def _softmax_finish(acc_ref, first, n_heads, o_ref, col0):
    for hp in range(n_heads // 2):
        outs = []
        for sub in range(2):
            acc = acc_ref[first + 2 * hp + sub]
            outs.append(acc[:HEAD_DIM] / acc[HEAD_DIM:HEAD_DIM + 1])
        o_ref[:, col0 + hp * PAIR_WIDTH:col0 + (hp + 1) * PAIR_WIDTH] = (
            jnp.concatenate(outs, axis=0).T.astype(o_ref.dtype))


def _init_softmax_state(m_ref, acc_ref):
    m_ref[...] = jnp.full_like(m_ref, -jnp.inf)
    acc_ref[...] = jnp.zeros_like(acc_ref)


def _pair_cols(head):
    return slice((head // 2) * PAIR_WIDTH, (head // 2 + 1) * PAIR_WIDTH)


def _softmax_pipeline(items, m_ref, acc_ref, lookahead=LOOKAHEAD):
    pending = [item[0]() for item in items[:lookahead]]
    for n, (_, values_fn, head) in enumerate(items):
        s = pending.pop(0)
        if n + lookahead < len(items):
            pending.append(items[n + lookahead][0]())
        _softmax_update(s, head, values_fn(), m_ref, acc_ref)


def _memory_items(qs_ref, mk_ref, mvt_ref):
    def score(i):
        return _dot(mk_ref[:, _pair_cols(i)], qs_ref[N_MAIN_HEADS + i, 0:PAIR_WIDTH, :])

    def values(i):
        return _values_with_ones(mvt_ref[0, 0, _pair_cols(i), :], i % 2)

    return [(functools.partial(score, i), functools.partial(values, i), N_MAIN_HEADS + i)
            for i in range(N_MEM_HEADS)]


def _fox_attn_kernel(q_ref, k_ref, cp_ref, vt_ref, qm_ref, mk_ref, mvt_ref, o_ref, qs_ref, m_ref, acc_ref, *, tq):
    qi = pl.program_id(1)
    tk = tq
    _write_gate_rows(qs_ref)
    _split_heads_into(q_ref, qs_ref, 0, N_MAIN_HEADS)
    _split_heads_into(qm_ref, qs_ref, N_MAIN_HEADS, N_MEM_HEADS)
    _init_softmax_state(m_ref, acc_ref)

    def block_items(kb, causal):
        ks = pl.multiple_of(kb * tk, tk)

        def score(head):
            kk = jnp.concatenate([k_ref[pl.ds(ks, tk), _pair_cols(head)], cp_ref[pl.ds(ks, tk), :]], axis=1)
            s = _dot(kk, qs_ref[head])
            return s if causal is None else jnp.where(causal, s, -jnp.inf)

        def values(head):
            return _values_with_ones(vt_ref[0, kb, _pair_cols(head), :], head % 2)

        return [(functools.partial(score, h), functools.partial(values, h), h) for h in range(N_MAIN_HEADS)]

    def two_blocks(it, carry):
        _softmax_pipeline(block_items(2 * it, None) + block_items(2 * it + 1, None), m_ref, acc_ref)
        return carry

    lax.fori_loop(0, qi >> 1, two_blocks, 0)
    key = lax.broadcasted_iota(jnp.int32, (tk, tq), 0)
    qry = lax.broadcasted_iota(jnp.int32, (tk, tq), 1)
    tail = block_items(qi, key <= qry) + _memory_items(qs_ref, mk_ref, mvt_ref)

    def finish(items):
        _softmax_pipeline(items, m_ref, acc_ref)
        _softmax_finish(acc_ref, 0, N_MAIN_HEADS, o_ref, 0)
        _softmax_finish(acc_ref, N_MAIN_HEADS, N_MEM_HEADS, o_ref, MAIN_WIDTH)

    @pl.when((qi & 1) == 1)
    def _():
        finish(block_items(qi - 1, None) + tail)

    @pl.when((qi & 1) == 0)
    def _():
        finish(tail)


def _sb_attn_kernel(q_ref, k_ref, vt_ref, qm_ref, mk_ref, mvt_ref, o_ref, qs_ref, m_ref, acc_ref, run_ref, *, tq):
    qi = pl.program_id(1)
    tk = tq
    _split_heads_into(q_ref, qs_ref, 0, N_MAIN_HEADS)
    _split_heads_into(qm_ref, qs_ref, N_MAIN_HEADS, N_MEM_HEADS)
    _init_softmax_state(m_ref, acc_ref)
    run_ref[...] = jnp.zeros_like(run_ref)

    ss = lax.broadcasted_iota(jnp.int32, (LANES + BF16_ROWS, 2 * LANES), 0)
    jj = lax.broadcasted_iota(jnp.int32, (LANES + BF16_ROWS, 2 * LANES), 1) & (LANES - 1)
    suffix = jnp.where((jj >= ss) | (ss >= LANES), 1.0, 0.0).astype(BF16)

    n_chunks = tk // LANES

    def stage_pipeline(items):
        def score_stage(kb, causal, head):
            ks = pl.multiple_of(kb * tk, tk)
            z = _dot(k_ref[pl.ds(ks, tk), _pair_cols(head)], qs_ref[head])
            logits, split = [], []
            for c in range(n_chunks):
                zc = z[c * LANES:(c + 1) * LANES]
                if causal is not None:
                    zc = jnp.where(causal[c], zc, -jnp.inf)
                nl = jnp.maximum(zc, 0.0) + jnp.log(1.0 + jnp.exp2(jnp.abs(zc) * -LOG2_E))
                hi = nl.astype(BF16)
                logits.append(zc)
                split.append(jnp.concatenate([hi, (nl - hi.astype(F32)).astype(BF16)], axis=0))
            return logits, split

        def suffix_stage(state):
            logits, split = state
            return logits, [_dot(suffix, sp) for sp in split]

        def value_stage(kb, causal, head, state):
            logits, sums = state
            run = run_ref[head][0:1]
            a_chunks = [None] * n_chunks
            for c in reversed(range(n_chunks)):
                a_chunks[c] = jnp.exp(logits[c] - (sums[c][:LANES] + run)).astype(BF16)
                run = run + sums[c][LANES:LANES + 1]
            run_ref[head] = jnp.broadcast_to(run, run_ref.shape[1:])
            vt = vt_ref[0, kb, _pair_cols(head), :][(head % 2) * HEAD_DIM:(head % 2 + 1) * HEAD_DIM]
            acc_ref[head, 0:HEAD_DIM] += _dot(vt, jnp.concatenate(a_chunks, axis=0))

        scored, summed = {}, {}
        for step in range(len(items) + SCORE_AHEAD + SUFFIX_AHEAD):
            if step < len(items):
                scored[step] = score_stage(*items[step])
            i2 = step - SCORE_AHEAD
            if 0 <= i2 < len(items):
                summed[i2] = suffix_stage(scored.pop(i2))
            i3 = i2 - SUFFIX_AHEAD
            if 0 <= i3 < len(items):
                value_stage(*items[i3], summed.pop(i3))

    def block_items(kb, causal):
        return [(kb, causal, head) for head in range(N_MAIN_HEADS)]

    key = lax.broadcasted_iota(jnp.int32, (LANES, tq), 0)
    qry = lax.broadcasted_iota(jnp.int32, (LANES, tq), 1)
    diagonal = block_items(qi, [(key + c * LANES) < qry for c in range(n_chunks)])
    memory = _memory_items(qs_ref, mk_ref, mvt_ref)

    @pl.when(qi > 0)
    def _():
        stage_pipeline(diagonal + block_items(qi - 1, None))
        _softmax_pipeline(memory, m_ref, acc_ref)

    @pl.when(qi == 0)
    def _():
        stage_pipeline(diagonal)
        _softmax_pipeline(memory, m_ref, acc_ref)

    def any_weight_left():
        least = functools.reduce(jnp.minimum, [run_ref[h] for h in range(N_MAIN_HEADS)])
        return (jnp.min(least) < RUN_DEAD).astype(jnp.int32)

    def cond(carry):
        it, live = carry
        return (it < qi) & (live > 0)

    def body(carry):
        it, _ = carry
        stage_pipeline(block_items(qi - 1 - it, None))
        return it + 1, any_weight_left()

    lax.while_loop(cond, body, (jnp.int32(1), any_weight_left()))
    for hp in range(N_MAIN_HEADS // 2):
        pair = jnp.concatenate([acc_ref[2 * hp, 0:HEAD_DIM], acc_ref[2 * hp + 1, 0:HEAD_DIM]], axis=0)
        o_ref[:, hp * PAIR_WIDTH:(hp + 1) * PAIR_WIDTH] = pair.T.astype(o_ref.dtype)
    _softmax_finish(acc_ref, N_MAIN_HEADS, N_MEM_HEADS, o_ref, MAIN_WIDTH)


def _attention(kind, q, k, cp, vt, qm, mk, mvt, batch, seq, n_mem, tq):
    t = q.shape[0]
    nq = seq // tq
    q_spec = pl.BlockSpec((tq, MAIN_WIDTH), lambda b, i: (b * nq + i, 0))
    k_spec = pl.BlockSpec((seq, MAIN_WIDTH), lambda b, i: (b, 0))
    vt_spec = pl.BlockSpec((1,) + vt.shape[1:], lambda b, i: (b, 0, 0, 0))
    qm_spec = pl.BlockSpec((tq, MEM_WIDTH), lambda b, i: (b * nq + i, 0))
    mk_spec = pl.BlockSpec((n_mem, MEM_WIDTH), lambda b, i: (b, 0))
    mvt_spec = pl.BlockSpec((1, 1, MEM_WIDTH, n_mem), lambda b, i: (b, 0, 0, 0))
    o_spec = pl.BlockSpec((tq, MAIN_WIDTH + MEM_WIDTH), lambda b, i: (b * nq + i, 0))
    q_rows = 2 * PAIR_WIDTH if kind == "fox" else PAIR_WIDTH
    scratch = [pltpu.VMEM((N_HEADS, q_rows, tq), BF16),
               pltpu.VMEM((N_HEADS, SUBLANES, tq), F32),
               pltpu.VMEM((N_HEADS, ACC_ROWS, tq), F32)]
    if kind == "fox":
        cp_spec = pl.BlockSpec((seq, LANES), lambda b, i: (b, 0))
        kernel = functools.partial(_fox_attn_kernel, tq=tq)
        in_specs = [q_spec, k_spec, cp_spec, vt_spec, qm_spec, mk_spec, mvt_spec]
        args = (q, k, cp, vt, qm, mk, mvt)
    else:
        kernel = functools.partial(_sb_attn_kernel, tq=tq)
        in_specs = [q_spec, k_spec, vt_spec, qm_spec, mk_spec, mvt_spec]
        args = (q, k, vt, qm, mk, mvt)
        scratch.append(pltpu.VMEM((N_MAIN_HEADS, SUBLANES, tq), F32))
    return pl.pallas_call(
        kernel,
        grid=(batch, nq),
        in_specs=in_specs,
        out_specs=o_spec,
        out_shape=jax.ShapeDtypeStruct((t, MAIN_WIDTH + MEM_WIDTH), BF16),
        scratch_shapes=scratch,
        compiler_params=_params("arbitrary", "arbitrary"),
        name=kind + "_attn",
    )(*args)


def _causal_conv(u, prev, cw, cb):
    r8 = lax.broadcasted_iota(jnp.int32, (SUBLANES, 1), 0)
    out = cb
    for j in range(CONV_WIDTH - 1):
        shift = CONV_WIDTH - 1 - j
        rolled = pltpu.roll(u, shift, axis=0)
        top = jnp.where(r8 < shift, pltpu.roll(prev, shift, axis=0), rolled[:SUBLANES])
        shifted = jnp.concatenate([top, rolled[SUBLANES:]], axis=0)
        out = out + cw[j:j + 1, :] * shifted
    return out + cw[CONV_WIDTH - 1:CONV_WIDTH, :] * u


def _out_ffn_kernel(x_ref, o_ref, wout_ref, g_ref, wup_ref, cw_ref, cb_ref, wd_ref, fg_ref, out_ref,
                    u_ref, act_ref, carry_ref, *, tiles_per_seq, tf, final_norm):
    tm = x_ref.shape[0]
    d_ff = wd_ref.shape[0]

    @pl.when(pl.program_id(0) % tiles_per_seq == 0)
    def _():
        carry_ref[...] = jnp.zeros_like(carry_ref)

    x1 = x_ref[...] + _dot(o_ref[...], wout_ref[...])
    h = (_rms_normalize(x1) * g_ref[...]).astype(BF16)
    for r in range(0, tm, tm // 2):
        u_ref[r:r + tm // 2] = _dot(h[r:r + tm // 2], wup_ref[...])
    for c in range(0, d_ff, tf):
        halves = []
        for half in range(2):
            cols = slice(half * d_ff + c, half * d_ff + c + tf)
            u = u_ref[:, cols]
            prev = carry_ref[:, cols]
            carry_ref[:, cols] = u[tm - SUBLANES:]
            halves.append(_causal_conv(u, prev, cw_ref[:, cols], cb_ref[:, cols]))
        gate, val = halves
        act_ref[:, c:c + tf] = (gate * (1.0 / (1.0 + jnp.exp(-gate))) * val).astype(BF16)
    y = x1 + _dot(act_ref[...], wd_ref[...])
    if final_norm:
        y = _rms_normalize(y) * fg_ref[...]
    out_ref[...] = y


def _layer_spec(stacked, layer):
    return pl.BlockSpec((None,) + stacked.shape[1:], lambda *_: (layer, 0, 0), pipeline_mode=pl.Buffered(1))


def _out_ffn(x2, o, layer, w_out, g, w_up, conv_w, conv_b, w_down, final_g, seq, tm, tf, final_norm):
    t, d = x2.shape
    d_ff = w_down.shape[1]
    stacked = (w_out, g, w_up, conv_w, conv_b, w_down)
    kernel = functools.partial(_out_ffn_kernel, tiles_per_seq=seq // tm, tf=tf, final_norm=final_norm)
    return pl.pallas_call(
        kernel,
        grid=(t // tm,),
        in_specs=[_row_spec(tm, d), _row_spec(tm, o.shape[1])] + [_layer_spec(w, layer) for w in stacked]
        + [_full_spec((1, d))],
        out_specs=_row_spec(tm, d),
        out_shape=jax.ShapeDtypeStruct((t, d), F32),
        scratch_shapes=[pltpu.VMEM((tm, 2 * d_ff), F32), pltpu.VMEM((tm, d_ff), BF16),
                        pltpu.VMEM((SUBLANES, 2 * d_ff), F32)],
        compiler_params=_params("arbitrary"),
        name="out_ffn",
    )(x2, o, *stacked, final_g.reshape(1, d))


def kernel(x, mem, ln_mix_g, w_in_a, b_f_a, w_in_b, ln_kv_g, w_kv, ln_mem_g, w_memkv, w_out, ln_ffn_g,
           w_up, conv_w, conv_b, w_down, final_g):
    batch, seq, d = x.shape
    n_mem = mem.shape[1]
    depth = ln_mix_g.shape[0]
    n_a = w_in_a.shape[0]
    assert w_in_b.shape[0] == 1, "one stick-breaking layer reads the shared K/V"
    tm_proj = min(512, seq)
    tq_attn = min(256, seq)
    tm_ffn, tf = min(512, seq), 256

    x2 = x.reshape(batch * seq, d)
    mem2 = mem.reshape(batch * n_mem, d)
    ffn_weights = (w_out.astype(BF16), ln_ffn_g.reshape(depth, 1, d), w_up.astype(BF16), conv_w,
                   conv_b.reshape(depth, 1, -1), w_down.astype(BF16))
    for layer in range(depth):
        mk, mvt = _mem_proj(mem2, ln_mem_g[layer], w_memkv[layer], batch, n_mem)
        if layer < n_a:
            q, k, vt, qm, cp = _proj_a(x2, ln_mix_g[layer], w_in_a[layer], b_f_a[layer], seq, tm_proj, tq_attn)
            o = _attention("fox", q, k, cp, vt, qm, mk, mvt, batch, seq, n_mem, tq_attn)
        else:
            q, qm, k_sh, vt_sh = _proj_b(x2, ln_mix_g[layer], ln_kv_g, w_in_b[0], w_kv, seq, tm_proj, tq_attn)
            o = _attention("sb", q, k_sh, None, vt_sh, qm, mk, mvt, batch, seq, n_mem, tq_attn)
        x2 = _out_ffn(x2, o, layer, *ffn_weights, final_g, seq, tm_ffn, tf, final_norm=(layer == depth - 1))
    return x2.reshape(batch, seq, d)
```

```python
import functools

import jax
import jax.numpy as jnp
from jax import lax
from jax.experimental import pallas as pl
from jax.experimental.pallas import tpu as pltpu

HEAD_DIM = 64
N_MAIN_HEADS = 12
N_MEM_HEADS = 4
N_HEADS = N_MAIN_HEADS + N_MEM_HEADS
MAIN_WIDTH = N_MAIN_HEADS * HEAD_DIM
MEM_WIDTH = N_MEM_HEADS * HEAD_DIM
CONV_WIDTH = 3
EPS = 1e-6
Q_SCALE = HEAD_DIM ** -0.5
LOG2_E = 1.4426950408889634

LANES = 128
SUBLANES = 8
BF16_ROWS = 16
PAIR_WIDTH = 2 * HEAD_DIM
GATE_TERMS = 3
VMEM_LIMIT = 56 * 1024 * 1024

F32 = jnp.float32
BF16 = jnp.bfloat16

_NT = (((1,), (1,)), ((), ()))


def _params(*sem):
    return pltpu.CompilerParams(dimension_semantics=sem, vmem_limit_bytes=VMEM_LIMIT)


def _dot(a, b):
    return jnp.dot(a, b, preferred_element_type=F32)


def _dot_nt(a, b):
    return lax.dot_general(a, b, _NT, preferred_element_type=F32)


def _rms_normalize(x):
    return x * lax.rsqrt(jnp.mean(x * x, axis=-1, keepdims=True) + EPS)


def _log_sigmoid(x):
    return jnp.minimum(x, 0.0) - jnp.log1p(jnp.exp(-jnp.abs(x)))


def _split3(x):
    hi = x.astype(BF16)
    r1 = x - hi.astype(F32)
    mid = r1.astype(BF16)
    return hi, mid, (r1 - mid.astype(F32)).astype(BF16)


MATMUL_ROWS = 256


def _store_chunks(h, w_ref, w_lo, out_ref, width, scale, chunk=256):
    rows = min(MATMUL_ROWS, h.shape[0])
    for r0 in range(0, h.shape[0], rows):
        for c in range(0, width, chunk):
            r = _dot(h[r0:r0 + rows], w_ref[:, w_lo + c:w_lo + c + chunk])
            if scale is not None:
                r = r * scale
            out_ref[r0:r0 + rows, c:c + chunk] = r.astype(out_ref.dtype)


def _store_transposed(h, w_ref, w_lo, out_ref, width, tk, chunk=256):
    for r in range(h.shape[0] // tk):
        for c in range(0, width, chunk):
            v = _dot(h[r * tk:(r + 1) * tk], w_ref[:, w_lo + c:w_lo + c + chunk])
            out_ref[0, r, c:c + chunk, :] = v.T.astype(out_ref.dtype)


def _proj_a_kernel(x_ref, g_ref, wqkv_ref, wqm_ref, wf_ref, bf_ref,
                   q_ref, k_ref, vt_ref, qm_ref, cp_ref, carry_ref, *, tiles_per_seq, tk):
    tm = x_ref.shape[0]

    @pl.when(pl.program_id(0) % tiles_per_seq == 0)
    def _():
        carry_ref[...] = jnp.zeros_like(carry_ref)

    h = (_rms_normalize(x_ref[...]) * g_ref[...]).astype(BF16)
    f_logit = _dot(h, wf_ref[...])
    _store_chunks(h, wqkv_ref, 0, q_ref, MAIN_WIDTH, Q_SCALE)

    parts = jnp.concatenate(_split3(_log_sigmoid(f_logit + bf_ref[...])), axis=1)
    row = lax.broadcasted_iota(jnp.int32, (tm, tm), 0)
    col = lax.broadcasted_iota(jnp.int32, (tm, tm), 1)
    sums = _dot(jnp.where(row >= col, 1.0, 0.0).astype(BF16), parts)
    _store_chunks(h, wqkv_ref, MAIN_WIDTH, k_ref, MAIN_WIDTH, None)

    c = (sums[:, :LANES] + sums[:, LANES:2 * LANES]) + sums[:, 2 * LANES:] + carry_ref[0:1, :]
    carry_ref[0:1, :] = c[tm - 1:tm, :]
    rr = lax.broadcasted_iota(jnp.int32, (GATE_TERMS * LANES, LANES), 0)
    cc = lax.broadcasted_iota(jnp.int32, (GATE_TERMS * LANES, LANES), 1)
    head, term = rr & (LANES - 1), rr >> 7
    place = jnp.where((cc == GATE_TERMS * head + term) & (head < N_MAIN_HEADS), 1.0, 0.0).astype(BF16)
    cp = _dot(jnp.concatenate(_split3(-c), axis=1), place)
    _store_transposed(h, wqkv_ref, 2 * MAIN_WIDTH, vt_ref, MAIN_WIDTH, tk)
    _store_chunks(h, wqm_ref, 0, qm_ref, MEM_WIDTH, Q_SCALE)
    cp_ref[...] = cp.astype(cp_ref.dtype)


def _proj_b_kernel(x_ref, gmix_ref, gkv_ref, win_ref, wkv_ref, q_ref, qm_ref, k_ref, vt_ref, *, tk):
    xn = _rms_normalize(x_ref[...])
    h = (xn * gmix_ref[...]).astype(BF16)
    _store_chunks(h, win_ref, 0, q_ref, MAIN_WIDTH, Q_SCALE)
    _store_chunks(h, win_ref, MAIN_WIDTH, qm_ref, MEM_WIDTH, Q_SCALE)
    hk = (xn * gkv_ref[...]).astype(BF16)
    _store_chunks(hk, wkv_ref, 0, k_ref, MAIN_WIDTH, None)
    _store_transposed(hk, wkv_ref, MAIN_WIDTH, vt_ref, MAIN_WIDTH, tk)


def _mem_proj_kernel(x_ref, g_ref, w_ref, mk_ref, mvt_ref):
    n_mem = x_ref.shape[0]
    h = (_rms_normalize(x_ref[...]) * g_ref[...]).astype(BF16)
    _store_chunks(h, w_ref, 0, mk_ref, MEM_WIDTH, None)
    _store_transposed(h, w_ref, MEM_WIDTH, mvt_ref, MEM_WIDTH, n_mem)


def _row_spec(tm, width):
    return pl.BlockSpec((tm, width), lambda i: (i, 0))


def _full_spec(shape):
    return pl.BlockSpec(shape, lambda *_: (0,) * len(shape))


def _vt_out(t, seq, tm, tk):
    tiles = seq // tm
    shape = (t // seq, seq // tk, MAIN_WIDTH, tk)
    spec = pl.BlockSpec((1, tm // tk, MAIN_WIDTH, tk), lambda i: (i // tiles, i % tiles, 0, 0))
    return spec, jax.ShapeDtypeStruct(shape, BF16)


def _proj_a(x2, g, w_in, b_f, seq, tm, tk):
    t, d = x2.shape
    n_f = N_MAIN_HEADS
    wqkv = w_in[:, :3 * MAIN_WIDTH].astype(BF16)
    wf = jnp.pad(w_in[:, 3 * MAIN_WIDTH:3 * MAIN_WIDTH + n_f], ((0, 0), (0, LANES - n_f))).astype(BF16)
    wqm = w_in[:, 3 * MAIN_WIDTH + n_f:].astype(BF16)
    bf = jnp.pad(b_f, (0, LANES - n_f)).reshape(1, LANES)
    vt_spec, vt_shape = _vt_out(t, seq, tm, tk)
    kernel = functools.partial(_proj_a_kernel, tiles_per_seq=seq // tm, tk=tk)
    return pl.pallas_call(
        kernel,
        grid=(t // tm,),
        in_specs=[_row_spec(tm, d), _full_spec((1, d)), _full_spec(wqkv.shape), _full_spec(wqm.shape),
                  _full_spec(wf.shape), _full_spec((1, LANES))],
        out_specs=[_row_spec(tm, MAIN_WIDTH), _row_spec(tm, MAIN_WIDTH), vt_spec,
                   _row_spec(tm, MEM_WIDTH), _row_spec(tm, LANES)],
        out_shape=[jax.ShapeDtypeStruct((t, MAIN_WIDTH), BF16), jax.ShapeDtypeStruct((t, MAIN_WIDTH), BF16),
                   vt_shape, jax.ShapeDtypeStruct((t, MEM_WIDTH), BF16), jax.ShapeDtypeStruct((t, LANES), BF16)],
        scratch_shapes=[pltpu.VMEM((SUBLANES, LANES), F32)],
        compiler_params=_params("arbitrary"),
        name="proj_a",
    )(x2, g.reshape(1, d), wqkv, wqm, wf, bf)


def _proj_b(x2, g_mix, g_kv, w_in, w_kv, seq, tm, tk):
    t, d = x2.shape
    win = w_in.astype(BF16)
    wkv = w_kv.astype(BF16)
    vt_spec, vt_shape = _vt_out(t, seq, tm, tk)
    return pl.pallas_call(
        functools.partial(_proj_b_kernel, tk=tk),
        grid=(t // tm,),
        in_specs=[_row_spec(tm, d), _full_spec((1, d)), _full_spec((1, d)), _full_spec(win.shape),
                  _full_spec(wkv.shape)],
        out_specs=[_row_spec(tm, MAIN_WIDTH), _row_spec(tm, MEM_WIDTH), _row_spec(tm, MAIN_WIDTH), vt_spec],
        out_shape=[jax.ShapeDtypeStruct((t, MAIN_WIDTH), BF16), jax.ShapeDtypeStruct((t, MEM_WIDTH), BF16),
                   jax.ShapeDtypeStruct((t, MAIN_WIDTH), BF16), vt_shape],
        compiler_params=_params("arbitrary"),
        name="proj_b",
    )(x2, g_mix.reshape(1, d), g_kv.reshape(1, d), win, wkv)


def _mem_proj(mem2, g, w, batch, n_mem):
    d = mem2.shape[1]
    wb = w.astype(BF16)
    return pl.pallas_call(
        _mem_proj_kernel,
        grid=(batch,),
        in_specs=[_row_spec(n_mem, d), _full_spec((1, d)), _full_spec(wb.shape)],
        out_specs=[_row_spec(n_mem, MEM_WIDTH), pl.BlockSpec((1, 1, MEM_WIDTH, n_mem), lambda i: (i, 0, 0, 0))],
        out_shape=[jax.ShapeDtypeStruct((batch * n_mem, MEM_WIDTH), BF16),
                   jax.ShapeDtypeStruct((batch, 1, MEM_WIDTH, n_mem), BF16)],
        compiler_params=_params("arbitrary"),
        name="mem_proj",
    )(mem2, g.reshape(1, d), wb)


ACC_ROWS = HEAD_DIM + BF16_ROWS
RUN_DEAD = 105.0
LOOKAHEAD = 6
SCORE_AHEAD = 3
SUFFIX_AHEAD = 3


def _split_heads_into(q_ref, dst_ref, first, n_heads):
    row = lax.broadcasted_iota(jnp.int32, (PAIR_WIDTH, 1), 0)
    low = row < HEAD_DIM
    for hp in range(n_heads // 2):
        qt = q_ref[:, hp * PAIR_WIDTH:(hp + 1) * PAIR_WIDTH].astype(F32).T
        for sub in range(2):
            keep = low if sub == 0 else jnp.logical_not(low)
            dst_ref[first + 2 * hp + sub, 0:PAIR_WIDTH, :] = jnp.where(keep, qt, 0.0).astype(dst_ref.dtype)


def _write_gate_rows(dst_ref):
    tq = dst_ref.shape[2]
    row = lax.broadcasted_iota(jnp.int32, (PAIR_WIDTH, 1), 0)
    for head in range(N_MAIN_HEADS):
        pick = (row >= GATE_TERMS * head) & (row < GATE_TERMS * (head + 1))
        dst_ref[head, PAIR_WIDTH:, :] = jnp.broadcast_to(
            jnp.where(pick, 1.0, 0.0).astype(dst_ref.dtype), (PAIR_WIDTH, tq))


def _values_with_ones(vt2, sub):
    vt = vt2[sub * HEAD_DIM:(sub + 1) * HEAD_DIM]
    return jnp.concatenate([vt, jnp.ones((BF16_ROWS, vt.shape[1]), vt.dtype)], axis=0)


def _softmax_update(s, head, vt_h, m_ref, acc_ref):
    m_prev = m_ref[head][0:1]
    m_new = jnp.maximum(m_prev, jnp.max(s, axis=0, keepdims=True))
    p = jnp.exp((s - m_new).astype(BF16))
    acc_ref[head] = jnp.exp(m_prev - m_new) * acc_ref[head] + _dot(vt_h, p)
    m_ref[head] = jnp.broadcast_to(m_new, m_ref.shape[1:])


def _softmax_finish(acc_ref, first, n_heads, o_ref, col0):
    for hp in range(n_heads // 2):
        outs = []
        for sub in range(2):
            acc = acc_ref[first + 2 * hp + sub]
            outs.append(acc[:HEAD_DIM] / acc[HEAD_DIM:HEAD_DIM + 1])
        o_ref[:, col0 + hp * PAIR_WIDTH:col0 + (hp + 1) * PAIR_WIDTH] = (
            jnp.concatenate(outs, axis=0).T.astype(o_ref.dtype))


def _init_softmax_state(m_ref, acc_ref):
    m_ref[...] = jnp.full_like(m_ref, -jnp.inf)
    acc_ref[...] = jnp.zeros_like(acc_ref)


def _pair_cols(head):
    return slice((head // 2) * PAIR_WIDTH, (head // 2 + 1) * PAIR_WIDTH)


def _softmax_pipeline(items, m_ref, acc_ref, lookahead=LOOKAHEAD):
    pending = [item[0]() for item in items[:lookahead]]
    for n, (_, values_fn, head) in enumerate(items):
        s = pending.pop(0)
        if n + lookahead < len(items):
            pending.append(items[n + lookahead][0]())
        _softmax_update(s, head, values_fn(), m_ref, acc_ref)


def _memory_items(qs_ref, mk_ref, mvt_ref):
    def score(i):
        return _dot(mk_ref[:, _pair_cols(i)], qs_ref[N_MAIN_HEADS + i, 0:PAIR_WIDTH, :])

    def values(i):
        return _values_with_ones(mvt_ref[0, 0, _pair_cols(i), :], i % 2)

    return [(functools.partial(score, i), functools.partial(values, i), N_MAIN_HEADS + i)
            for i in range(N_MEM_HEADS)]


def _fox_attn_kernel(q_ref, k_ref, cp_ref, vt_ref, qm_ref, mk_ref, mvt_ref, o_ref, qs_ref, m_ref, acc_ref, *, tq):
    qi = pl.program_id(1)
    tk = tq
    _write_gate_rows(qs_ref)
    _split_heads_into(q_ref, qs_ref, 0, N_MAIN_HEADS)
    _split_heads_into(qm_ref, qs_ref, N_MAIN_HEADS, N_MEM_HEADS)
    _init_softmax_state(m_ref, acc_ref)

    def block_items(kb, causal):
        ks = pl.multiple_of(kb * tk, tk)

        def score(head):
            kk = jnp.concatenate([k_ref[pl.ds(ks, tk), _pair_cols(head)], cp_ref[pl.ds(ks, tk), :]], axis=1)
            s = _dot(kk, qs_ref[head])
            return s if causal is None else jnp.where(causal, s, -jnp.inf)

        def values(head):
            return _values_with_ones(vt_ref[0, kb, _pair_cols(head), :], head % 2)

        return [(functools.partial(score, h), functools.partial(values, h), h) for h in range(N_MAIN_HEADS)]

    def two_blocks(it, carry):
        _softmax_pipeline(block_items(2 * it, None) + block_items(2 * it + 1, None), m_ref, acc_ref)
        return carry

    lax.fori_loop(0, qi >> 1, two_blocks, 0)
    key = lax.broadcasted_iota(jnp.int32, (tk, tq), 0)
    qry = lax.broadcasted_iota(jnp.int32, (tk, tq), 1)
    tail = block_items(qi, key <= qry) + _memory_items(qs_ref, mk_ref, mvt_ref)

    def finish(items):
        _softmax_pipeline(items, m_ref, acc_ref)
        _softmax_finish(acc_ref, 0, N_MAIN_HEADS, o_ref, 0)
        _softmax_finish(acc_ref, N_MAIN_HEADS, N_MEM_HEADS, o_ref, MAIN_WIDTH)

    @pl.when((qi & 1) == 1)
    def _():
        finish(block_items(qi - 1, None) + tail)

    @pl.when((qi & 1) == 0)
    def _():
        finish(tail)


def _sb_attn_kernel(q_ref, k_ref, vt_ref, qm_ref, mk_ref, mvt_ref, o_ref, qs_ref, m_ref, acc_ref, run_ref, *, tq):
    qi = pl.program_id(1)
    tk = tq
    _split_heads_into(q_ref, qs_ref, 0, N_MAIN_HEADS)
    _split_heads_into(qm_ref, qs_ref, N_MAIN_HEADS, N_MEM_HEADS)
    _init_softmax_state(m_ref, acc_ref)
    run_ref[...] = jnp.zeros_like(run_ref)

    ss = lax.broadcasted_iota(jnp.int32, (LANES + BF16_ROWS, 2 * LANES), 0)
    jj = lax.broadcasted_iota(jnp.int32, (LANES + BF16_ROWS, 2 * LANES), 1) & (LANES - 1)
    suffix = jnp.where((jj >= ss) | (ss >= LANES), 1.0, 0.0).astype(BF16)

    n_chunks = tk // LANES

    def stage_pipeline(items):
        def score_stage(kb, causal, head):
            ks = pl.multiple_of(kb * tk, tk)
            z = _dot(k_ref[pl.ds(ks, tk), _pair_cols(head)], qs_ref[head])
            logits, split = [], []
            for c in range(n_chunks):
                zc = z[c * LANES:(c + 1) * LANES]
                if causal is not None:
                    zc = jnp.where(causal[c], zc, -jnp.inf)
                nl = jnp.maximum(zc, 0.0) + jnp.log(1.0 + jnp.exp2(jnp.abs(zc) * -LOG2_E))
                hi = nl.astype(BF16)
                logits.append(zc)
                split.append(jnp.concatenate([hi, (nl - hi.astype(F32)).astype(BF16)], axis=0))
            return logits, split

        def suffix_stage(state):
            logits, split = state
            return logits, [_dot(suffix, sp) for sp in split]

        def value_stage(kb, causal, head, state):
            logits, sums = state
            run = run_ref[head][0:1]
            a_chunks = [None] * n_chunks
            for c in reversed(range(n_chunks)):
                a_chunks[c] = jnp.exp((logits[c] - (sums[c][:LANES] + run)).astype(BF16))
                run = run + sums[c][LANES:LANES + 1]
            run_ref[head] = jnp.broadcast_to(run, run_ref.shape[1:])
            vt = vt_ref[0, kb, _pair_cols(head), :][(head % 2) * HEAD_DIM:(head % 2 + 1) * HEAD_DIM]
            acc_ref[head, 0:HEAD_DIM] += _dot(vt, jnp.concatenate(a_chunks, axis=0))

        scored, summed = {}, {}
        for step in range(len(items) + SCORE_AHEAD + SUFFIX_AHEAD):
            if step < len(items):
                scored[step] = score_stage(*items[step])
            i2 = step - SCORE_AHEAD
            if 0 <= i2 < len(items):
                summed[i2] = suffix_stage(scored.pop(i2))
            i3 = i2 - SUFFIX_AHEAD
            if 0 <= i3 < len(items):
                value_stage(*items[i3], summed.pop(i3))

    def block_items(kb, causal):
        return [(kb, causal, head) for head in range(N_MAIN_HEADS)]

    key = lax.broadcasted_iota(jnp.int32, (LANES, tq), 0)
    qry = lax.broadcasted_iota(jnp.int32, (LANES, tq), 1)
    diagonal = block_items(qi, [(key + c * LANES) < qry for c in range(n_chunks)])
    memory = _memory_items(qs_ref, mk_ref, mvt_ref)

    @pl.when(qi > 0)
    def _():
        stage_pipeline(diagonal + block_items(qi - 1, None))
        _softmax_pipeline(memory, m_ref, acc_ref)

    @pl.when(qi == 0)
    def _():
        stage_pipeline(diagonal)
        _softmax_pipeline(memory, m_ref, acc_ref)

    def any_weight_left():
        least = functools.reduce(jnp.minimum, [run_ref[h] for h in range(N_MAIN_HEADS)])
        return (jnp.min(least) < RUN_DEAD).astype(jnp.int32)

    def cond(carry):
        it, live = carry
        return (it < qi) & (live > 0)

    def body(carry):
        it, _ = carry
        stage_pipeline(block_items(qi - 1 - it, None))
        return it + 1, any_weight_left()

    lax.while_loop(cond, body, (jnp.int32(1), any_weight_left()))
    for hp in range(N_MAIN_HEADS // 2):
        pair = jnp.concatenate([acc_ref[2 * hp, 0:HEAD_DIM], acc_ref[2 * hp + 1, 0:HEAD_DIM]], axis=0)
        o_ref[:, hp * PAIR_WIDTH:(hp + 1) * PAIR_WIDTH] = pair.T.astype(o_ref.dtype)
    _softmax_finish(acc_ref, N_MAIN_HEADS, N_MEM_HEADS, o_ref, MAIN_WIDTH)


def _attention(kind, q, k, cp, vt, qm, mk, mvt, batch, seq, n_mem, tq):
    t = q.shape[0]
    nq = seq // tq
    q_spec = pl.BlockSpec((tq, MAIN_WIDTH), lambda b, i: (b * nq + i, 0))
    k_spec = pl.BlockSpec((seq, MAIN_WIDTH), lambda b, i: (b, 0))
    vt_spec = pl.BlockSpec((1,) + vt.shape[1:], lambda b, i: (b, 0, 0, 0))
    qm_spec = pl.BlockSpec((tq, MEM_WIDTH), lambda b, i: (b * nq + i, 0))
    mk_spec = pl.BlockSpec((n_mem, MEM_WIDTH), lambda b, i: (b, 0))
    mvt_spec = pl.BlockSpec((1, 1, MEM_WIDTH, n_mem), lambda b, i: (b, 0, 0, 0))
    o_spec = pl.BlockSpec((tq, MAIN_WIDTH + MEM_WIDTH), lambda b, i: (b * nq + i, 0))
    q_rows = 2 * PAIR_WIDTH if kind == "fox" else PAIR_WIDTH
    scratch = [pltpu.VMEM((N_HEADS, q_rows, tq), BF16),
               pltpu.VMEM((N_HEADS, SUBLANES, tq), F32),
               pltpu.VMEM((N_HEADS, ACC_ROWS, tq), F32)]
    if kind == "fox":
        cp_spec = pl.BlockSpec((seq, LANES), lambda b, i: (b, 0))
        kernel = functools.partial(_fox_attn_kernel, tq=tq)
        in_specs = [q_spec, k_spec, cp_spec, vt_spec, qm_spec, mk_spec, mvt_spec]
        args = (q, k, cp, vt, qm, mk, mvt)
    else:
        kernel = functools.partial(_sb_attn_kernel, tq=tq)
        in_specs = [q_spec, k_spec, vt_spec, qm_spec, mk_spec, mvt_spec]
        args = (q, k, vt, qm, mk, mvt)
        scratch.append(pltpu.VMEM((N_MAIN_HEADS, SUBLANES, tq), F32))
    return pl.pallas_call(
        kernel,
        grid=(batch, nq),
        in_specs=in_specs,
        out_specs=o_spec,
        out_shape=jax.ShapeDtypeStruct((t, MAIN_WIDTH + MEM_WIDTH), BF16),
        scratch_shapes=scratch,
        compiler_params=_params("arbitrary", "arbitrary"),
        name=kind + "_attn",
    )(*args)


def _causal_conv(u, prev, cw, cb):
    r8 = lax.broadcasted_iota(jnp.int32, (SUBLANES, 1), 0)
    out = cb
    for j in range(CONV_WIDTH - 1):
        shift = CONV_WIDTH - 1 - j
        rolled = pltpu.roll(u, shift, axis=0)
        top = jnp.where(r8 < shift, pltpu.roll(prev, shift, axis=0), rolled[:SUBLANES])
        shifted = jnp.concatenate([top, rolled[SUBLANES:]], axis=0)
        out = out + cw[j:j + 1, :] * shifted
    return out + cw[CONV_WIDTH - 1:CONV_WIDTH, :] * u


def _out_ffn_kernel(x_ref, o_ref, wout_ref, g_ref, wup_ref, cw_ref, cb_ref, wd_ref, fg_ref, out_ref,
                    u_ref, act_ref, carry_ref, *, tiles_per_seq, tf, final_norm):
    tm = x_ref.shape[0]
    d_ff = wd_ref.shape[0]

    @pl.when(pl.program_id(0) % tiles_per_seq == 0)
    def _():
        carry_ref[...] = jnp.zeros_like(carry_ref)

    x1 = x_ref[...] + _dot(o_ref[...], wout_ref[...])
    h = (_rms_normalize(x1) * g_ref[...]).astype(BF16)
    for r in range(0, tm, tm // 2):
        u_ref[r:r + tm // 2] = _dot(h[r:r + tm // 2], wup_ref[...])
    for c in range(0, d_ff, tf):
        halves = []
        for half in range(2):
            cols = slice(half * d_ff + c, half * d_ff + c + tf)
            u = u_ref[:, cols]
            prev = carry_ref[:, cols]
            carry_ref[:, cols] = u[tm - SUBLANES:]
            halves.append(_causal_conv(u, prev, cw_ref[:, cols], cb_ref[:, cols]))
        gate, val = halves
        act_ref[:, c:c + tf] = (gate * (1.0 / (1.0 + jnp.exp(-gate))) * val).astype(BF16)
    y = x1 + _dot(act_ref[...], wd_ref[...])
    if final_norm:
        y = _rms_normalize(y) * fg_ref[...]
    out_ref[...] = y


def _layer_spec(stacked, layer):
    return pl.BlockSpec((None,) + stacked.shape[1:], lambda *_: (layer, 0, 0), pipeline_mode=pl.Buffered(1))


def _out_ffn(x2, o, layer, w_out, g, w_up, conv_w, conv_b, w_down, final_g, seq, tm, tf, final_norm):
    t, d = x2.shape
    d_ff = w_down.shape[1]
    stacked = (w_out, g, w_up, conv_w, conv_b, w_down)
    kernel = functools.partial(_out_ffn_kernel, tiles_per_seq=seq // tm, tf=tf, final_norm=final_norm)
    return pl.pallas_call(
        kernel,
        grid=(t // tm,),
        in_specs=[_row_spec(tm, d), _row_spec(tm, o.shape[1])] + [_layer_spec(w, layer) for w in stacked]
        + [_full_spec((1, d))],
        out_specs=_row_spec(tm, d),
        out_shape=jax.ShapeDtypeStruct((t, d), F32),
        scratch_shapes=[pltpu.VMEM((tm, 2 * d_ff), F32), pltpu.VMEM((tm, d_ff), BF16),
                        pltpu.VMEM((SUBLANES, 2 * d_ff), F32)],
        compiler_params=_params("arbitrary"),
        name="out_ffn",
    )(x2, o, *stacked, final_g.reshape(1, d))


def kernel(x, mem, ln_mix_g, w_in_a, b_f_a, w_in_b, ln_kv_g, w_kv, ln_mem_g, w_memkv, w_out, ln_ffn_g,
           w_up, conv_w, conv_b, w_down, final_g):
    batch, seq, d = x.shape
    n_mem = mem.shape[1]
    depth = ln_mix_g.shape[0]
    n_a = w_in_a.shape[0]
    assert w_in_b.shape[0] == 1, "one stick-breaking layer reads the shared K/V"
    tm_proj = min(512, seq)
    tq_attn = min(256, seq)
    tm_ffn, tf = min(512, seq), 256

    x2 = x.reshape(batch * seq, d)
    mem2 = mem.reshape(batch * n_mem, d)
    ffn_weights = (w_out.astype(BF16), ln_ffn_g.reshape(depth, 1, d), w_up.astype(BF16), conv_w,
                   conv_b.reshape(depth, 1, -1), w_down.astype(BF16))
    for layer in range(depth):
        mk, mvt = _mem_proj(mem2, ln_mem_g[layer], w_memkv[layer], batch, n_mem)
        if layer < n_a:
            q, k, vt, qm, cp = _proj_a(x2, ln_mix_g[layer], w_in_a[layer], b_f_a[layer], seq, tm_proj, tq_attn)
            o = _attention("fox", q, k, cp, vt, qm, mk, mvt, batch, seq, n_mem, tq_attn)
        else:
            q, qm, k_sh, vt_sh = _proj_b(x2, ln_mix_g[layer], ln_kv_g, w_in_b[0], w_kv, seq, tm_proj, tq_attn)
            o = _attention("sb", q, k_sh, None, vt_sh, qm, mk, mvt, batch, seq, n_mem, tq_attn)
        x2 = _out_ffn(x2, o, layer, *ffn_weights, final_g, seq, tm_ffn, tf, final_norm=(layer == depth - 1))
    return x2.reshape(batch, seq, d)
```

```python
import functools

import jax
import jax.numpy as jnp
from jax import lax
from jax.experimental import pallas as pl
from jax.experimental.pallas import tpu as pltpu

HEAD_DIM = 64
N_MAIN_HEADS = 12
N_MEM_HEADS = 4
N_HEADS = N_MAIN_HEADS + N_MEM_HEADS
MAIN_WIDTH = N_MAIN_HEADS * HEAD_DIM
MEM_WIDTH = N_MEM_HEADS * HEAD_DIM
CONV_WIDTH = 3
EPS = 1e-6
Q_SCALE = HEAD_DIM ** -0.5
LOG2_E = 1.4426950408889634

LANES = 128
SUBLANES = 8
BF16_ROWS = 16
PAIR_WIDTH = 2 * HEAD_DIM
GATE_TERMS = 3
VMEM_LIMIT = 56 * 1024 * 1024

MATMUL_ROWS = 256

F32 = jnp.float32
BF16 = jnp.bfloat16


def _params(*sem):
    return pltpu.CompilerParams(dimension_semantics=sem, vmem_limit_bytes=VMEM_LIMIT)


def _dot(a, b):
    return jnp.dot(a, b, preferred_element_type=F32)


def _rms_normalize(x):
    return x * lax.rsqrt(jnp.mean(x * x, axis=-1, keepdims=True) + EPS)


def _log_sigmoid(x):
    return jnp.minimum(x, 0.0) - jnp.log1p(jnp.exp(-jnp.abs(x)))


def _split3(x):
    hi = x.astype(BF16)
    r1 = x - hi.astype(F32)
    mid = r1.astype(BF16)
    return hi, mid, (r1 - mid.astype(F32)).astype(BF16)


def _store_chunks(h, w_ref, w_lo, out_ref, width, scale, chunk=256):
    rows = min(MATMUL_ROWS, h.shape[0])
    for r0 in range(0, h.shape[0], rows):
        for c in range(0, width, chunk):
            r = _dot(h[r0:r0 + rows], w_ref[:, w_lo + c:w_lo + c + chunk])
            if scale is not None:
                r = r * scale
            out_ref[r0:r0 + rows, c:c + chunk] = r.astype(out_ref.dtype)


def _store_transposed(h, w_ref, w_lo, out_ref, width, tk, chunk=256):
    for r in range(h.shape[0] // tk):
        for c in range(0, width, chunk):
            v = _dot(h[r * tk:(r + 1) * tk], w_ref[:, w_lo + c:w_lo + c + chunk])
            out_ref[0, r, c:c + chunk, :] = v.T.astype(out_ref.dtype)


def _proj_a_kernel(x_ref, g_ref, wqkv_ref, wqm_ref, wf_ref, bf_ref,
                   q_ref, k_ref, vt_ref, qm_ref, cp_ref, carry_ref, *, tiles_per_seq, tk):
    tm = x_ref.shape[0]

    @pl.when(pl.program_id(0) % tiles_per_seq == 0)
    def _():
        carry_ref[...] = jnp.zeros_like(carry_ref)

    h = (_rms_normalize(x_ref[...]) * g_ref[...]).astype(BF16)
    f_logit = _dot(h, wf_ref[...])
    _store_chunks(h, wqkv_ref, 0, q_ref, MAIN_WIDTH, Q_SCALE)

    parts = jnp.concatenate(_split3(_log_sigmoid(f_logit + bf_ref[...])), axis=1)
    row = lax.broadcasted_iota(jnp.int32, (tm, tm), 0)
    col = lax.broadcasted_iota(jnp.int32, (tm, tm), 1)
    sums = _dot(jnp.where(row >= col, 1.0, 0.0).astype(BF16), parts)
    _store_chunks(h, wqkv_ref, MAIN_WIDTH, k_ref, MAIN_WIDTH, None)

    c = (sums[:, :LANES] + sums[:, LANES:2 * LANES]) + sums[:, 2 * LANES:] + carry_ref[0:1, :]
    carry_ref[0:1, :] = c[tm - 1:tm, :]
    rr = lax.broadcasted_iota(jnp.int32, (GATE_TERMS * LANES, LANES), 0)
    cc = lax.broadcasted_iota(jnp.int32, (GATE_TERMS * LANES, LANES), 1)
    head, term = rr & (LANES - 1), rr >> 7
    place = jnp.where((cc == GATE_TERMS * head + term) & (head < N_MAIN_HEADS), 1.0, 0.0).astype(BF16)
    cp = _dot(jnp.concatenate(_split3(-c), axis=1), place)
    _store_transposed(h, wqkv_ref, 2 * MAIN_WIDTH, vt_ref, MAIN_WIDTH, tk)
    _store_chunks(h, wqm_ref, 0, qm_ref, MEM_WIDTH, Q_SCALE)
    cp_ref[...] = cp.astype(cp_ref.dtype)


def _proj_b_kernel(x_ref, gmix_ref, gkv_ref, win_ref, wkv_ref, q_ref, qm_ref, k_ref, vt_ref, *, tk):
    xn = _rms_normalize(x_ref[...])
    h = (xn * gmix_ref[...]).astype(BF16)
    _store_chunks(h, win_ref, 0, q_ref, MAIN_WIDTH, Q_SCALE)
    _store_chunks(h, win_ref, MAIN_WIDTH, qm_ref, MEM_WIDTH, Q_SCALE)
    hk = (xn * gkv_ref[...]).astype(BF16)
    _store_chunks(hk, wkv_ref, 0, k_ref, MAIN_WIDTH, None)
    _store_transposed(hk, wkv_ref, MAIN_WIDTH, vt_ref, MAIN_WIDTH, tk)


def _mem_proj_kernel(x_ref, g_ref, w_ref, mk_ref, mvt_ref):
    n_mem = x_ref.shape[0]
    xn = _rms_normalize(x_ref[...])
    for layer in range(g_ref.shape[0]):
        h = (xn * g_ref[layer]).astype(BF16)
        _store_chunks(h, w_ref.at[layer], 0, mk_ref.at[layer], MEM_WIDTH, None)
        _store_transposed(h, w_ref.at[layer], MEM_WIDTH, mvt_ref.at[layer], MEM_WIDTH, n_mem)


def _row_spec(tm, width):
    return pl.BlockSpec((tm, width), lambda i: (i, 0))


def _full_spec(shape):
    return pl.BlockSpec(shape, lambda *_: (0,) * len(shape))


def _vt_out(t, seq, tm, tk):
    tiles = seq // tm
    shape = (t // seq, seq // tk, MAIN_WIDTH, tk)
    spec = pl.BlockSpec((1, tm // tk, MAIN_WIDTH, tk), lambda i: (i // tiles, i % tiles, 0, 0))
    return spec, jax.ShapeDtypeStruct(shape, BF16)


def _proj_a(x2, g, w_in, b_f, seq, tm, tk):
    t, d = x2.shape
    n_f = N_MAIN_HEADS
    wqkv = w_in[:, :3 * MAIN_WIDTH].astype(BF16)
    wf = jnp.pad(w_in[:, 3 * MAIN_WIDTH:3 * MAIN_WIDTH + n_f], ((0, 0), (0, LANES - n_f))).astype(BF16)
    wqm = w_in[:, 3 * MAIN_WIDTH + n_f:].astype(BF16)
    bf = jnp.pad(b_f, (0, LANES - n_f)).reshape(1, LANES)
    vt_spec, vt_shape = _vt_out(t, seq, tm, tk)
    kernel = functools.partial(_proj_a_kernel, tiles_per_seq=seq // tm, tk=tk)
    return pl.pallas_call(
        kernel,
        grid=(t // tm,),
        in_specs=[_row_spec(tm, d), _full_spec((1, d)), _full_spec(wqkv.shape), _full_spec(wqm.shape),
                  _full_spec(wf.shape), _full_spec((1, LANES))],
        out_specs=[_row_spec(tm, MAIN_WIDTH), _row_spec(tm, MAIN_WIDTH), vt_spec,
                   _row_spec(tm, MEM_WIDTH), _row_spec(tm, LANES)],
        out_shape=[jax.ShapeDtypeStruct((t, MAIN_WIDTH), BF16), jax.ShapeDtypeStruct((t, MAIN_WIDTH), BF16),
                   vt_shape, jax.ShapeDtypeStruct((t, MEM_WIDTH), BF16), jax.ShapeDtypeStruct((t, LANES), BF16)],
        scratch_shapes=[pltpu.VMEM((SUBLANES, LANES), F32)],
        compiler_params=_params("arbitrary"),
        name="proj_a",
    )(x2, g.reshape(1, d), wqkv, wqm, wf, bf)


def _proj_b(x2, g_mix, g_kv, w_in, w_kv, seq, tm, tk):
    t, d = x2.shape
    win = w_in.astype(BF16)
    wkv = w_kv.astype(BF16)
    vt_spec, vt_shape = _vt_out(t, seq, tm, tk)
    return pl.pallas_call(
        functools.partial(_proj_b_kernel, tk=tk),
        grid=(t // tm,),
        in_specs=[_row_spec(tm, d), _full_spec((1, d)), _full_spec((1, d)), _full_spec(win.shape),
                  _full_spec(wkv.shape)],
        out_specs=[_row_spec(tm, MAIN_WIDTH), _row_spec(tm, MEM_WIDTH), _row_spec(tm, MAIN_WIDTH), vt_spec],
        out_shape=[jax.ShapeDtypeStruct((t, MAIN_WIDTH), BF16), jax.ShapeDtypeStruct((t, MEM_WIDTH), BF16),
                   jax.ShapeDtypeStruct((t, MAIN_WIDTH), BF16), vt_shape],
        compiler_params=_params("arbitrary"),
        name="proj_b",
    )(x2, g_mix.reshape(1, d), g_kv.reshape(1, d), win, wkv)


def _mem_proj(mem2, g, w, batch, n_mem):
    depth, d = g.shape
    wb = w.astype(BF16)
    return pl.pallas_call(
        _mem_proj_kernel,
        grid=(batch,),
        in_specs=[_row_spec(n_mem, d), _full_spec((depth, 1, d)), _full_spec(wb.shape)],
        out_specs=[pl.BlockSpec((depth, n_mem, MEM_WIDTH), lambda i: (0, i, 0)),
                   pl.BlockSpec((depth, 1, 1, MEM_WIDTH, n_mem), lambda i: (0, i, 0, 0, 0))],
        out_shape=[jax.ShapeDtypeStruct((depth, batch * n_mem, MEM_WIDTH), BF16),
                   jax.ShapeDtypeStruct((depth, batch, 1, MEM_WIDTH, n_mem), BF16)],
        compiler_params=_params("arbitrary"),
        name="mem_proj",
    )(mem2, g.reshape(depth, 1, d), wb)


ACC_ROWS = HEAD_DIM + BF16_ROWS
RUN_DEAD = 105.0
LOOKAHEAD = 6
SCORE_AHEAD = 3
SUFFIX_AHEAD = 3


def _split_heads_into(q_ref, dst_ref, first, n_heads):
    row = lax.broadcasted_iota(jnp.int32, (PAIR_WIDTH, 1), 0)
    low = row < HEAD_DIM
    for hp in range(n_heads // 2):
        qt = q_ref[:, hp * PAIR_WIDTH:(hp + 1) * PAIR_WIDTH].astype(F32).T
        for sub in range(2):
            keep = low if sub == 0 else jnp.logical_not(low)
            dst_ref[first + 2 * hp + sub, 0:PAIR_WIDTH, :] = jnp.where(keep, qt, 0.0).astype(dst_ref.dtype)


def _write_gate_rows(dst_ref):
    tq = dst_ref.shape[2]
    row = lax.broadcasted_iota(jnp.int32, (PAIR_WIDTH, 1), 0)
    for head in range(N_MAIN_HEADS):
        pick = (row >= GATE_TERMS * head) & (row < GATE_TERMS * (head + 1))
        dst_ref[head, PAIR_WIDTH:, :] = jnp.broadcast_to(
            jnp.where(pick, 1.0, 0.0).astype(dst_ref.dtype), (PAIR_WIDTH, tq))


def _values_with_ones(vt2, sub):
    vt = vt2[sub * HEAD_DIM:(sub + 1) * HEAD_DIM]
    return jnp.concatenate([vt, jnp.ones((BF16_ROWS, vt.shape[1]), vt.dtype)], axis=0)


def _softmax_update(s, head, vt_h, m_ref, acc_ref):
    m_prev = m_ref[head][0:1]
    m_new = jnp.maximum(m_prev, jnp.max(s, axis=0, keepdims=True))
    p = jnp.exp((s - m_new).astype(BF16))
    acc_ref[head] = jnp.exp(m_prev - m_new) * acc_ref[head] + _dot(vt_h, p)
    m_ref[head] = jnp.broadcast_to(m_new, m_ref.shape[1:])


def _softmax_finish(acc_ref, first, n_heads, o_ref, col0):
    for hp in range(n_heads // 2):
        outs = []
        for sub in range(2):
            acc = acc_ref[first + 2 * hp + sub]
            outs.append(acc[:HEAD_DIM] / acc[HEAD_DIM:HEAD_DIM + 1])
        o_ref[:, col0 + hp * PAIR_WIDTH:col0 + (hp + 1) * PAIR_WIDTH] = (
            jnp.concatenate(outs, axis=0).T.astype(o_ref.dtype))


def _init_softmax_state(m_ref, acc_ref):
    m_ref[...] = jnp.full_like(m_ref, -jnp.inf)
    acc_ref[...] = jnp.zeros_like(acc_ref)


def _pair_cols(head):
    return slice((head // 2) * PAIR_WIDTH, (head // 2 + 1) * PAIR_WIDTH)


def _softmax_pipeline(items, m_ref, acc_ref, lookahead=LOOKAHEAD):
    pending = [item[0]() for item in items[:lookahead]]
    for n, (_, values_fn, head) in enumerate(items):
        s = pending.pop(0)
        if n + lookahead < len(items):
            pending.append(items[n + lookahead][0]())
        _softmax_update(s, head, values_fn(), m_ref, acc_ref)


def _memory_items(qs_ref, mk_ref, mvt_ref):
    def score(i):
        return _dot(mk_ref[:, _pair_cols(i)], qs_ref[N_MAIN_HEADS + i, 0:PAIR_WIDTH, :])

    def values(i):
        return _values_with_ones(mvt_ref[_pair_cols(i), :], i % 2)

    return [(functools.partial(score, i), functools.partial(values, i), N_MAIN_HEADS + i)
            for i in range(N_MEM_HEADS)]


def _fox_attn_kernel(q_ref, k_ref, cp_ref, vt_ref, qm_ref, mk_ref, mvt_ref, o_ref, qs_ref, m_ref, acc_ref, *, tq):
    qi = pl.program_id(1)
    tk = tq
    _write_gate_rows(qs_ref)
    _split_heads_into(q_ref, qs_ref, 0, N_MAIN_HEADS)
    _split_heads_into(qm_ref, qs_ref, N_MAIN_HEADS, N_MEM_HEADS)
    _init_softmax_state(m_ref, acc_ref)

    def block_items(kb, causal):
        ks = pl.multiple_of(kb * tk, tk)

        def score(head):
            kk = jnp.concatenate([k_ref[pl.ds(ks, tk), _pair_cols(head)], cp_ref[pl.ds(ks, tk), :]], axis=1)
            s = _dot(kk, qs_ref[head])
            return s if causal is None else jnp.where(causal, s, -jnp.inf)

        def values(head):
            return _values_with_ones(vt_ref[0, kb, _pair_cols(head), :], head % 2)

        return [(functools.partial(score, h), functools.partial(values, h), h) for h in range(N_MAIN_HEADS)]

    def two_blocks(it, carry):
        _softmax_pipeline(block_items(2 * it, None) + block_items(2 * it + 1, None), m_ref, acc_ref)
        return carry

    lax.fori_loop(0, qi >> 1, two_blocks, 0)
    key = lax.broadcasted_iota(jnp.int32, (tk, tq), 0)
    qry = lax.broadcasted_iota(jnp.int32, (tk, tq), 1)
    tail = block_items(qi, key <= qry) + _memory_items(qs_ref, mk_ref, mvt_ref)

    def finish(items):
        _softmax_pipeline(items, m_ref, acc_ref)
        _softmax_finish(acc_ref, 0, N_MAIN_HEADS, o_ref, 0)
        _softmax_finish(acc_ref, N_MAIN_HEADS, N_MEM_HEADS, o_ref, MAIN_WIDTH)

    @pl.when((qi & 1) == 1)
    def _():
        finish(block_items(qi - 1, None) + tail)

    @pl.when((qi & 1) == 0)
    def _():
        finish(tail)


def _sb_attn_kernel(q_ref, k_ref, vt_ref, qm_ref, mk_ref, mvt_ref, o_ref, qs_ref, m_ref, acc_ref, run_ref, *, tq):
    qi = pl.program_id(1)
    tk = tq
    _split_heads_into(q_ref, qs_ref, 0, N_MAIN_HEADS)
    _split_heads_into(qm_ref, qs_ref, N_MAIN_HEADS, N_MEM_HEADS)
    _init_softmax_state(m_ref, acc_ref)
    run_ref[...] = jnp.zeros_like(run_ref)

    ss = lax.broadcasted_iota(jnp.int32, (LANES + BF16_ROWS, 2 * LANES), 0)
    jj = lax.broadcasted_iota(jnp.int32, (LANES + BF16_ROWS, 2 * LANES), 1) & (LANES - 1)
    suffix = jnp.where((jj >= ss) | (ss >= LANES), 1.0, 0.0).astype(BF16)

    n_chunks = tk // LANES

    def stage_pipeline(items):
        def score_stage(kb, causal, head):
            ks = pl.multiple_of(kb * tk, tk)
            z = _dot(k_ref[pl.ds(ks, tk), _pair_cols(head)], qs_ref[head])
            logits, split = [], []
            for c in range(n_chunks):
                zc = z[c * LANES:(c + 1) * LANES]
                if causal is not None:
                    zc = jnp.where(causal[c], zc, -jnp.inf)
                nl = jnp.maximum(zc, 0.0) + jnp.log(1.0 + jnp.exp2(jnp.abs(zc) * -LOG2_E))
                hi = nl.astype(BF16)
                logits.append(zc)
                split.append(jnp.concatenate([hi, (nl - hi.astype(F32)).astype(BF16)], axis=0))
            return logits, split

        def suffix_stage(state):
            logits, split = state
            return logits, [_dot(suffix, sp) for sp in split]

        def value_stage(kb, causal, head, state):
            logits, sums = state
            run = run_ref[head][0:1]
            a_chunks = [None] * n_chunks
            for c in reversed(range(n_chunks)):
                a_chunks[c] = jnp.exp((logits[c] - (sums[c][:LANES] + run)).astype(BF16))
                run = run + sums[c][LANES:LANES + 1]
            run_ref[head] = jnp.broadcast_to(run, run_ref.shape[1:])
            vt = vt_ref[0, kb, _pair_cols(head), :][(head % 2) * HEAD_DIM:(head % 2 + 1) * HEAD_DIM]
            acc_ref[head, 0:HEAD_DIM] += _dot(vt, jnp.concatenate(a_chunks, axis=0))

        scored, summed = {}, {}
        for step in range(len(items) + SCORE_AHEAD + SUFFIX_AHEAD):
            if step < len(items):
                scored[step] = score_stage(*items[step])
            i2 = step - SCORE_AHEAD
            if 0 <= i2 < len(items):
                summed[i2] = suffix_stage(scored.pop(i2))
            i3 = i2 - SUFFIX_AHEAD
            if 0 <= i3 < len(items):
                value_stage(*items[i3], summed.pop(i3))

    def block_items(kb, causal):
        return [(kb, causal, head) for head in range(N_MAIN_HEADS)]

    key = lax.broadcasted_iota(jnp.int32, (LANES, tq), 0)
    qry = lax.broadcasted_iota(jnp.int32, (LANES, tq), 1)
    diagonal = block_items(qi, [(key + c * LANES) < qry for c in range(n_chunks)])
    memory = _memory_items(qs_ref, mk_ref, mvt_ref)

    @pl.when(qi > 0)
    def _():
        stage_pipeline(diagonal + block_items(qi - 1, None))
        _softmax_pipeline(memory, m_ref, acc_ref)

    @pl.when(qi == 0)
    def _():
        stage_pipeline(diagonal)
        _softmax_pipeline(memory, m_ref, acc_ref)

    def any_weight_left():
        least = functools.reduce(jnp.minimum, [run_ref[h] for h in range(N_MAIN_HEADS)])
        return (jnp.min(least) < RUN_DEAD).astype(jnp.int32)

    def cond(carry):
        it, live = carry
        return (it < qi) & (live > 0)

    def body(carry):
        it, _ = carry
        stage_pipeline(block_items(qi - 1 - it, None))
        return it + 1, any_weight_left()

    lax.while_loop(cond, body, (jnp.int32(1), any_weight_left()))
    for hp in range(N_MAIN_HEADS // 2):
        pair = jnp.concatenate([acc_ref[2 * hp, 0:HEAD_DIM], acc_ref[2 * hp + 1, 0:HEAD_DIM]], axis=0)
        o_ref[:, hp * PAIR_WIDTH:(hp + 1) * PAIR_WIDTH] = pair.T.astype(o_ref.dtype)
    _softmax_finish(acc_ref, N_MAIN_HEADS, N_MEM_HEADS, o_ref, MAIN_WIDTH)


def _attention(kind, layer, q, k, cp, vt, qm, mk, mvt, batch, seq, n_mem, tq):
    t = q.shape[0]
    nq = seq // tq
    q_spec = pl.BlockSpec((tq, MAIN_WIDTH), lambda b, i: (b * nq + i, 0))
    k_spec = pl.BlockSpec((seq, MAIN_WIDTH), lambda b, i: (b, 0))
    vt_spec = pl.BlockSpec((1,) + vt.shape[1:], lambda b, i: (b, 0, 0, 0))
    qm_spec = pl.BlockSpec((tq, MEM_WIDTH), lambda b, i: (b * nq + i, 0))
    mk_spec = pl.BlockSpec((None, n_mem, MEM_WIDTH), lambda b, i: (layer, b, 0))
    mvt_spec = pl.BlockSpec((None, None, None, MEM_WIDTH, n_mem), lambda b, i: (layer, b, 0, 0, 0))
    o_spec = pl.BlockSpec((tq, MAIN_WIDTH + MEM_WIDTH), lambda b, i: (b * nq + i, 0))
    q_rows = 2 * PAIR_WIDTH if kind == "fox" else PAIR_WIDTH
    scratch = [pltpu.VMEM((N_HEADS, q_rows, tq), BF16),
               pltpu.VMEM((N_HEADS, SUBLANES, tq), F32),
               pltpu.VMEM((N_HEADS, ACC_ROWS, tq), F32)]
    if kind == "fox":
        cp_spec = pl.BlockSpec((seq, LANES), lambda b, i: (b, 0))
        kernel = functools.partial(_fox_attn_kernel, tq=tq)
        in_specs = [q_spec, k_spec, cp_spec, vt_spec, qm_spec, mk_spec, mvt_spec]
        args = (q, k, cp, vt, qm, mk, mvt)
    else:
        kernel = functools.partial(_sb_attn_kernel, tq=tq)
        in_specs = [q_spec, k_spec, vt_spec, qm_spec, mk_spec, mvt_spec]
        args = (q, k, vt, qm, mk, mvt)
        scratch.append(pltpu.VMEM((N_MAIN_HEADS, SUBLANES, tq), F32))
    return pl.pallas_call(
        kernel,
        grid=(batch, nq),
        in_specs=in_specs,
        out_specs=o_spec,
        out_shape=jax.ShapeDtypeStruct((t, MAIN_WIDTH + MEM_WIDTH), BF16),
        scratch_shapes=scratch,
        compiler_params=_params("arbitrary", "arbitrary"),
        name=kind + "_attn",
    )(*args)


def _causal_conv(u, prev, cw, cb):
    r8 = lax.broadcasted_iota(jnp.int32, (SUBLANES, 1), 0)
    out = cb
    for j in range(CONV_WIDTH - 1):
        shift = CONV_WIDTH - 1 - j
        rolled = pltpu.roll(u, shift, axis=0)
        top = jnp.where(r8 < shift, pltpu.roll(prev, shift, axis=0), rolled[:SUBLANES])
        shifted = jnp.concatenate([top, rolled[SUBLANES:]], axis=0)
        out = out + cw[j:j + 1, :] * shifted
    return out + cw[CONV_WIDTH - 1:CONV_WIDTH, :] * u


def _out_ffn_kernel(x_ref, o_ref, wout_ref, g_ref, wup_ref, cw_ref, cb_ref, wd_ref, fg_ref, out_ref,
                    u_ref, act_ref, carry_ref, *, tiles_per_seq, tf, final_norm):
    tm = x_ref.shape[0]
    d_ff = wd_ref.shape[0]

    @pl.when(pl.program_id(0) % tiles_per_seq == 0)
    def _():
        carry_ref[...] = jnp.zeros_like(carry_ref)

    x1 = x_ref[...] + _dot(o_ref[...], wout_ref[...])
    h = (_rms_normalize(x1) * g_ref[...]).astype(BF16)
    rows = min(MATMUL_ROWS, tm)
    for r in range(0, tm, rows):
        u_ref[r:r + rows] = _dot(h[r:r + rows], wup_ref[...])
    for c in range(0, d_ff, tf):
        halves = []
        for half in range(2):
            cols = slice(half * d_ff + c, half * d_ff + c + tf)
            u = u_ref[:, cols]
            prev = carry_ref[:, cols]
            carry_ref[:, cols] = u[tm - SUBLANES:]
            halves.append(_causal_conv(u, prev, cw_ref[:, cols], cb_ref[:, cols]))
        gate, val = halves
        act_ref[:, c:c + tf] = (gate * (1.0 / (1.0 + jnp.exp(-gate))) * val).astype(BF16)
    y = x1 + _dot(act_ref[...], wd_ref[...])
    if final_norm:
        y = _rms_normalize(y) * fg_ref[...]
    out_ref[...] = y


def _layer_spec(stacked, layer):
    return pl.BlockSpec((None,) + stacked.shape[1:], lambda *_: (layer, 0, 0), pipeline_mode=pl.Buffered(1))


def _out_ffn(x2, o, layer, w_out, g, w_up, conv_w, conv_b, w_down, final_g, seq, tm, tf, final_norm):
    t, d = x2.shape
    d_ff = w_down.shape[1]
    stacked = (w_out, g, w_up, conv_w, conv_b, w_down)
    kernel = functools.partial(_out_ffn_kernel, tiles_per_seq=seq // tm, tf=tf, final_norm=final_norm)
    return pl.pallas_call(
        kernel,
        grid=(t // tm,),
        in_specs=[_row_spec(tm, d), _row_spec(tm, o.shape[1])] + [_layer_spec(w, layer) for w in stacked]
        + [_full_spec((1, d))],
        out_specs=_row_spec(tm, d),
        out_shape=jax.ShapeDtypeStruct((t, d), F32),
        scratch_shapes=[pltpu.VMEM((tm, 2 * d_ff), F32), pltpu.VMEM((tm, d_ff), BF16),
                        pltpu.VMEM((SUBLANES, 2 * d_ff), F32)],
        compiler_params=_params("arbitrary"),
        name="out_ffn",
    )(x2, o, *stacked, final_g.reshape(1, d))


def kernel(x, mem, ln_mix_g, w_in_a, b_f_a, w_in_b, ln_kv_g, w_kv, ln_mem_g, w_memkv, w_out, ln_ffn_g,
           w_up, conv_w, conv_b, w_down, final_g):
    batch, seq, d = x.shape
    n_mem = mem.shape[1]
    depth = ln_mix_g.shape[0]
    n_a = w_in_a.shape[0]
    assert w_in_b.shape[0] == 1, "one stick-breaking layer reads the shared K/V"
    tm_proj = min(512, seq)
    tq_attn = min(256, seq)
    tm_ffn, tf = min(512, seq), 256

    x2 = x.reshape(batch * seq, d)
    mem2 = mem.reshape(batch * n_mem, d)
    ffn_weights = (w_out.astype(BF16), ln_ffn_g.reshape(depth, 1, d), w_up.astype(BF16), conv_w,
                   conv_b.reshape(depth, 1, -1), w_down.astype(BF16))
    mk, mvt = _mem_proj(mem2, ln_mem_g, w_memkv, batch, n_mem)
    for layer in range(depth):
        if layer < n_a:
            q, k, vt, qm, cp = _proj_a(x2, ln_mix_g[layer], w_in_a[layer], b_f_a[layer], seq, tm_proj, tq_attn)
            o = _attention("fox", layer, q, k, cp, vt, qm, mk, mvt, batch, seq, n_mem, tq_attn)
        else:
            q, qm, k_sh, vt_sh = _proj_b(x2, ln_mix_g[layer], ln_kv_g, w_in_b[0], w_kv, seq, tm_proj, tq_attn)
            o = _attention("sb", layer, q, k_sh, None, vt_sh, qm, mk, mvt, batch, seq, n_mem, tq_attn)
        x2 = _out_ffn(x2, o, layer, *ffn_weights, final_g, seq, tm_ffn, tf, final_norm=(layer == depth - 1))
    return x2.reshape(batch, seq, d)
```

```python
import functools

import jax
import jax.numpy as jnp
from jax import lax
from jax.experimental import pallas as pl
from jax.experimental.pallas import tpu as pltpu

HEAD_DIM = 64
N_MAIN_HEADS = 12
N_MEM_HEADS = 4
N_HEADS = N_MAIN_HEADS + N_MEM_HEADS
MAIN_WIDTH = N_MAIN_HEADS * HEAD_DIM
MEM_WIDTH = N_MEM_HEADS * HEAD_DIM
CONV_WIDTH = 3
EPS = 1e-6
Q_SCALE = HEAD_DIM ** -0.5
LOG2_E = 1.4426950408889634

LANES = 128
SUBLANES = 8
BF16_ROWS = 16
PAIR_WIDTH = 2 * HEAD_DIM
GATE_TERMS = 3
VMEM_LIMIT = 56 * 1024 * 1024

MATMUL_ROWS = 256
PREFIX_ROWS = 512

F32 = jnp.float32
BF16 = jnp.bfloat16


def _params(*sem):
    return pltpu.CompilerParams(dimension_semantics=sem, vmem_limit_bytes=VMEM_LIMIT)


def _dot(a, b):
    return jnp.dot(a, b, preferred_element_type=F32)


def _rms_normalize(x):
    return x * lax.rsqrt(jnp.mean(x * x, axis=-1, keepdims=True) + EPS)


def _log_sigmoid(x):
    return jnp.minimum(x, 0.0) - jnp.log1p(jnp.exp(-jnp.abs(x)))


def _split3(x):
    hi = x.astype(BF16)
    r1 = x - hi.astype(F32)
    mid = r1.astype(BF16)
    return hi, mid, (r1 - mid.astype(F32)).astype(BF16)


def _store_chunks(h, w_ref, w_lo, out_ref, width, scale, chunk=256):
    rows = min(MATMUL_ROWS, h.shape[0])
    for r0 in range(0, h.shape[0], rows):
        for c in range(0, width, chunk):
            r = _dot(h[r0:r0 + rows], w_ref[:, w_lo + c:w_lo + c + chunk])
            if scale is not None:
                r = r * scale
            out_ref[r0:r0 + rows, c:c + chunk] = r.astype(out_ref.dtype)


def _store_transposed(h, w_ref, w_lo, out_ref, width, tk, chunk=256):
    for r in range(h.shape[0] // tk):
        for c in range(0, width, chunk):
            v = _dot(h[r * tk:(r + 1) * tk], w_ref[:, w_lo + c:w_lo + c + chunk])
            out_ref[0, r, c:c + chunk, :] = v.T.astype(out_ref.dtype)


def _proj_a_kernel(x_ref, g_ref, wqkv_ref, wqm_ref, wf_ref, bf_ref,
                   q_ref, k_ref, vt_ref, qm_ref, cp_ref, carry_ref, *, tiles_per_seq, tk):
    tm = x_ref.shape[0]

    @pl.when(pl.program_id(0) % tiles_per_seq == 0)
    def _():
        carry_ref[...] = jnp.zeros_like(carry_ref)

    h = (_rms_normalize(x_ref[...]) * g_ref[...]).astype(BF16)
    pr = min(PREFIX_ROWS, tm)
    f_logits = [_dot(h[r:r + pr], wf_ref[...]) for r in range(0, tm, pr)]
    _store_chunks(h, wqkv_ref, 0, q_ref, MAIN_WIDTH, Q_SCALE)

    row = lax.broadcasted_iota(jnp.int32, (pr, pr), 0)
    col = lax.broadcasted_iota(jnp.int32, (pr, pr), 1)
    tri = jnp.where(row >= col, 1.0, 0.0).astype(BF16)
    sums = [_dot(tri, jnp.concatenate(_split3(_log_sigmoid(f + bf_ref[...])), axis=1)) for f in f_logits]
    _store_chunks(h, wqkv_ref, MAIN_WIDTH, k_ref, MAIN_WIDTH, None)

    rr = lax.broadcasted_iota(jnp.int32, (GATE_TERMS * LANES, LANES), 0)
    cc = lax.broadcasted_iota(jnp.int32, (GATE_TERMS * LANES, LANES), 1)
    head, term = rr & (LANES - 1), rr >> 7
    place = jnp.where((cc == GATE_TERMS * head + term) & (head < N_MAIN_HEADS), 1.0, 0.0).astype(BF16)
    carry = carry_ref[0:1, :]
    cps = []
    for s in sums:
        c = (s[:, :LANES] + s[:, LANES:2 * LANES]) + s[:, 2 * LANES:] + carry
        carry = c[pr - 1:pr, :]
        cps.append(_dot(jnp.concatenate(_split3(-c), axis=1), place))
    carry_ref[0:1, :] = carry
    _store_transposed(h, wqkv_ref, 2 * MAIN_WIDTH, vt_ref, MAIN_WIDTH, tk)
    _store_chunks(h, wqm_ref, 0, qm_ref, MEM_WIDTH, Q_SCALE)
    cp_ref[...] = jnp.concatenate(cps, axis=0).astype(cp_ref.dtype)


def _proj_b_kernel(x_ref, gmix_ref, gkv_ref, win_ref, wkv_ref, q_ref, qm_ref, k_ref, vt_ref, *, tk):
    xn = _rms_normalize(x_ref[...])
    h = (xn * gmix_ref[...]).astype(BF16)
    _store_chunks(h, win_ref, 0, q_ref, MAIN_WIDTH, Q_SCALE)
    _store_chunks(h, win_ref, MAIN_WIDTH, qm_ref, MEM_WIDTH, Q_SCALE)
    hk = (xn * gkv_ref[...]).astype(BF16)
    _store_chunks(hk, wkv_ref, 0, k_ref, MAIN_WIDTH, None)
    _store_transposed(hk, wkv_ref, MAIN_WIDTH, vt_ref, MAIN_WIDTH, tk)


def _mem_proj_kernel(x_ref, g_ref, w_ref, mk_ref, mvt_ref):
    n_mem = x_ref.shape[0]
    xn = _rms_normalize(x_ref[...])
    for layer in range(g_ref.shape[0]):
        h = (xn * g_ref[layer]).astype(BF16)
        _store_chunks(h, w_ref.at[layer], 0, mk_ref.at[layer], MEM_WIDTH, None)
        _store_transposed(h, w_ref.at[layer], MEM_WIDTH, mvt_ref.at[layer], MEM_WIDTH, n_mem)


def _row_spec(tm, width):
    return pl.BlockSpec((tm, width), lambda i: (i, 0))


def _full_spec(shape):
    return pl.BlockSpec(shape, lambda *_: (0,) * len(shape))


def _vt_out(t, seq, tm, tk):
    tiles = seq // tm
    shape = (t // seq, seq // tk, MAIN_WIDTH, tk)
    spec = pl.BlockSpec((1, tm // tk, MAIN_WIDTH, tk), lambda i: (i // tiles, i % tiles, 0, 0))
    return spec, jax.ShapeDtypeStruct(shape, BF16)


def _proj_a(x2, g, w_in, b_f, seq, tm, tk):
    t, d = x2.shape
    n_f = N_MAIN_HEADS
    wqkv = w_in[:, :3 * MAIN_WIDTH].astype(BF16)
    wf = jnp.pad(w_in[:, 3 * MAIN_WIDTH:3 * MAIN_WIDTH + n_f], ((0, 0), (0, LANES - n_f))).astype(BF16)
    wqm = w_in[:, 3 * MAIN_WIDTH + n_f:].astype(BF16)
    bf = jnp.pad(b_f, (0, LANES - n_f)).reshape(1, LANES)
    vt_spec, vt_shape = _vt_out(t, seq, tm, tk)
    kernel = functools.partial(_proj_a_kernel, tiles_per_seq=seq // tm, tk=tk)
    return pl.pallas_call(
        kernel,
        grid=(t // tm,),
        in_specs=[_row_spec(tm, d), _full_spec((1, d)), _full_spec(wqkv.shape), _full_spec(wqm.shape),
                  _full_spec(wf.shape), _full_spec((1, LANES))],
        out_specs=[_row_spec(tm, MAIN_WIDTH), _row_spec(tm, MAIN_WIDTH), vt_spec,
                   _row_spec(tm, MEM_WIDTH), _row_spec(tm, LANES)],
        out_shape=[jax.ShapeDtypeStruct((t, MAIN_WIDTH), BF16), jax.ShapeDtypeStruct((t, MAIN_WIDTH), BF16),
                   vt_shape, jax.ShapeDtypeStruct((t, MEM_WIDTH), BF16), jax.ShapeDtypeStruct((t, LANES), BF16)],
        scratch_shapes=[pltpu.VMEM((SUBLANES, LANES), F32)],
        compiler_params=_params("arbitrary"),
        name="proj_a",
    )(x2, g.reshape(1, d), wqkv, wqm, wf, bf)


def _proj_b(x2, g_mix, g_kv, w_in, w_kv, seq, tm, tk):
    t, d = x2.shape
    win = w_in.astype(BF16)
    wkv = w_kv.astype(BF16)
    vt_spec, vt_shape = _vt_out(t, seq, tm, tk)
    return pl.pallas_call(
        functools.partial(_proj_b_kernel, tk=tk),
        grid=(t // tm,),
        in_specs=[_row_spec(tm, d), _full_spec((1, d)), _full_spec((1, d)), _full_spec(win.shape),
                  _full_spec(wkv.shape)],
        out_specs=[_row_spec(tm, MAIN_WIDTH), _row_spec(tm, MEM_WIDTH), _row_spec(tm, MAIN_WIDTH), vt_spec],
        out_shape=[jax.ShapeDtypeStruct((t, MAIN_WIDTH), BF16), jax.ShapeDtypeStruct((t, MEM_WIDTH), BF16),
                   jax.ShapeDtypeStruct((t, MAIN_WIDTH), BF16), vt_shape],
        compiler_params=_params("arbitrary"),
        name="proj_b",
    )(x2, g_mix.reshape(1, d), g_kv.reshape(1, d), win, wkv)


def _mem_proj(mem2, g, w, batch, n_mem):
    depth, d = g.shape
    wb = w.astype(BF16)
    return pl.pallas_call(
        _mem_proj_kernel,
        grid=(batch,),
        in_specs=[_row_spec(n_mem, d), _full_spec((depth, 1, d)), _full_spec(wb.shape)],
        out_specs=[pl.BlockSpec((depth, n_mem, MEM_WIDTH), lambda i: (0, i, 0)),
                   pl.BlockSpec((depth, 1, 1, MEM_WIDTH, n_mem), lambda i: (0, i, 0, 0, 0))],
        out_shape=[jax.ShapeDtypeStruct((depth, batch * n_mem, MEM_WIDTH), BF16),
                   jax.ShapeDtypeStruct((depth, batch, 1, MEM_WIDTH, n_mem), BF16)],
        compiler_params=_params("arbitrary"),
        name="mem_proj",
    )(mem2, g.reshape(depth, 1, d), wb)


ACC_ROWS = HEAD_DIM + BF16_ROWS
RUN_DEAD = 105.0
LOOKAHEAD = 6
SCORE_AHEAD = 3
SUFFIX_AHEAD = 3


def _split_heads_into(q_ref, dst_ref, first, n_heads):
    row = lax.broadcasted_iota(jnp.int32, (PAIR_WIDTH, 1), 0)
    low = row < HEAD_DIM
    for hp in range(n_heads // 2):
        qt = q_ref[:, hp * PAIR_WIDTH:(hp + 1) * PAIR_WIDTH].astype(F32).T
        for sub in range(2):
            keep = low if sub == 0 else jnp.logical_not(low)
            dst_ref[first + 2 * hp + sub, 0:PAIR_WIDTH, :] = jnp.where(keep, qt, 0.0).astype(dst_ref.dtype)


def _write_gate_rows(dst_ref):
    tq = dst_ref.shape[2]
    row = lax.broadcasted_iota(jnp.int32, (PAIR_WIDTH, 1), 0)
    for head in range(N_MAIN_HEADS):
        pick = (row >= GATE_TERMS * head) & (row < GATE_TERMS * (head + 1))
        dst_ref[head, PAIR_WIDTH:, :] = jnp.broadcast_to(
            jnp.where(pick, 1.0, 0.0).astype(dst_ref.dtype), (PAIR_WIDTH, tq))


def _values_with_ones(vt2, sub):
    vt = vt2[sub * HEAD_DIM:(sub + 1) * HEAD_DIM]
    return jnp.concatenate([vt, jnp.ones((BF16_ROWS, vt.shape[1]), vt.dtype)], axis=0)


def _softmax_update(s, head, vt_h, m_ref, acc_ref):
    m_prev = m_ref[head][0:1]
    m_new = jnp.maximum(m_prev, jnp.max(s, axis=0, keepdims=True))
    p = jnp.exp((s - m_new).astype(BF16))
    acc_ref[head] = jnp.exp(m_prev - m_new) * acc_ref[head] + _dot(vt_h, p)
    m_ref[head] = jnp.broadcast_to(m_new, m_ref.shape[1:])


def _softmax_finish(acc_ref, first, n_heads, o_ref, col0):
    for hp in range(n_heads // 2):
        outs = []
        for sub in range(2):
            acc = acc_ref[first + 2 * hp + sub]
            outs.append(acc[:HEAD_DIM] / acc[HEAD_DIM:HEAD_DIM + 1])
        o_ref[:, col0 + hp * PAIR_WIDTH:col0 + (hp + 1) * PAIR_WIDTH] = (
            jnp.concatenate(outs, axis=0).T.astype(o_ref.dtype))


def _init_softmax_state(m_ref, acc_ref):
    m_ref[...] = jnp.full_like(m_ref, -jnp.inf)
    acc_ref[...] = jnp.zeros_like(acc_ref)


def _pair_cols(head):
    return slice((head // 2) * PAIR_WIDTH, (head // 2 + 1) * PAIR_WIDTH)


def _softmax_pipeline(items, m_ref, acc_ref, lookahead=LOOKAHEAD):
    pending = [item[0]() for item in items[:lookahead]]
    for n, (_, values_fn, head) in enumerate(items):
        s = pending.pop(0)
        if n + lookahead < len(items):
            pending.append(items[n + lookahead][0]())
        _softmax_update(s, head, values_fn(), m_ref, acc_ref)


def _memory_items(qs_ref, mk_ref, mvt_ref):
    def score(i):
        return _dot(mk_ref[:, _pair_cols(i)], qs_ref[N_MAIN_HEADS + i, 0:PAIR_WIDTH, :])

    def values(i):
        return _values_with_ones(mvt_ref[_pair_cols(i), :], i % 2)

    return [(functools.partial(score, i), functools.partial(values, i), N_MAIN_HEADS + i)
            for i in range(N_MEM_HEADS)]


def _fox_attn_kernel(q_ref, k_ref, cp_ref, vt_ref, qm_ref, mk_ref, mvt_ref, o_ref, qs_ref, m_ref, acc_ref, *, tq):
    qi = pl.program_id(1)
    tk = tq
    _write_gate_rows(qs_ref)
    _split_heads_into(q_ref, qs_ref, 0, N_MAIN_HEADS)
    _split_heads_into(qm_ref, qs_ref, N_MAIN_HEADS, N_MEM_HEADS)
    _init_softmax_state(m_ref, acc_ref)

    def block_items(kb, causal):
        ks = pl.multiple_of(kb * tk, tk)

        def score(head):
            kk = jnp.concatenate([k_ref[pl.ds(ks, tk), _pair_cols(head)], cp_ref[pl.ds(ks, tk), :]], axis=1)
            s = _dot(kk, qs_ref[head])
            return s if causal is None else jnp.where(causal, s, -jnp.inf)

        def values(head):
            return _values_with_ones(vt_ref[0, kb, _pair_cols(head), :], head % 2)

        return [(functools.partial(score, h), functools.partial(values, h), h) for h in range(N_MAIN_HEADS)]

    def two_blocks(it, carry):
        _softmax_pipeline(block_items(2 * it, None) + block_items(2 * it + 1, None), m_ref, acc_ref)
        return carry

    lax.fori_loop(0, qi >> 1, two_blocks, 0)
    key = lax.broadcasted_iota(jnp.int32, (tk, tq), 0)
    qry = lax.broadcasted_iota(jnp.int32, (tk, tq), 1)
    tail = block_items(qi, key <= qry) + _memory_items(qs_ref, mk_ref, mvt_ref)

    def finish(items):
        _softmax_pipeline(items, m_ref, acc_ref)
        _softmax_finish(acc_ref, 0, N_MAIN_HEADS, o_ref, 0)
        _softmax_finish(acc_ref, N_MAIN_HEADS, N_MEM_HEADS, o_ref, MAIN_WIDTH)

    @pl.when((qi & 1) == 1)
    def _():
        finish(block_items(qi - 1, None) + tail)

    @pl.when((qi & 1) == 0)
    def _():
        finish(tail)


def _sb_attn_kernel(q_ref, k_ref, vt_ref, qm_ref, mk_ref, mvt_ref, o_ref, qs_ref, m_ref, acc_ref, run_ref, *, tq):
    qi = pl.program_id(1)
    tk = tq
    _split_heads_into(q_ref, qs_ref, 0, N_MAIN_HEADS)
    _split_heads_into(qm_ref, qs_ref, N_MAIN_HEADS, N_MEM_HEADS)
    _init_softmax_state(m_ref, acc_ref)
    run_ref[...] = jnp.zeros_like(run_ref)

    ss = lax.broadcasted_iota(jnp.int32, (LANES + BF16_ROWS, 2 * LANES), 0)
    jj = lax.broadcasted_iota(jnp.int32, (LANES + BF16_ROWS, 2 * LANES), 1) & (LANES - 1)
    suffix = jnp.where((jj >= ss) | (ss >= LANES), 1.0, 0.0).astype(BF16)

    n_chunks = tk // LANES

    def stage_pipeline(items):
        def score_stage(kb, causal, head):
            ks = pl.multiple_of(kb * tk, tk)
            z = _dot(k_ref[pl.ds(ks, tk), _pair_cols(head)], qs_ref[head])
            logits, split = [], []
            for c in range(n_chunks):
                zc = z[c * LANES:(c + 1) * LANES]
                if causal is not None:
                    zc = jnp.where(causal[c], zc, -jnp.inf)
                nl = jnp.maximum(zc, 0.0) + jnp.log(1.0 + jnp.exp2(jnp.abs(zc) * -LOG2_E))
                hi = nl.astype(BF16)
                logits.append(zc)
                split.append(jnp.concatenate([hi, (nl - hi.astype(F32)).astype(BF16)], axis=0))
            return logits, split

        def suffix_stage(state):
            logits, split = state
            return logits, [_dot(suffix, sp) for sp in split]

        def value_stage(kb, causal, head, state):
            logits, sums = state
            run = run_ref[head][0:1]
            a_chunks = [None] * n_chunks
            for c in reversed(range(n_chunks)):
                a_chunks[c] = jnp.exp((logits[c] - (sums[c][:LANES] + run)).astype(BF16))
                run = run + sums[c][LANES:LANES + 1]
            run_ref[head] = jnp.broadcast_to(run, run_ref.shape[1:])
            vt = vt_ref[0, kb, _pair_cols(head), :][(head % 2) * HEAD_DIM:(head % 2 + 1) * HEAD_DIM]
            acc_ref[head, 0:HEAD_DIM] += _dot(vt, jnp.concatenate(a_chunks, axis=0))

        scored, summed = {}, {}
        for step in range(len(items) + SCORE_AHEAD + SUFFIX_AHEAD):
            if step < len(items):
                scored[step] = score_stage(*items[step])
            i2 = step - SCORE_AHEAD
            if 0 <= i2 < len(items):
                summed[i2] = suffix_stage(scored.pop(i2))
            i3 = i2 - SUFFIX_AHEAD
            if 0 <= i3 < len(items):
                value_stage(*items[i3], summed.pop(i3))

    def block_items(kb, causal):
        return [(kb, causal, head) for head in range(N_MAIN_HEADS)]

    key = lax.broadcasted_iota(jnp.int32, (LANES, tq), 0)
    qry = lax.broadcasted_iota(jnp.int32, (LANES, tq), 1)
    diagonal = block_items(qi, [(key + c * LANES) < qry for c in range(n_chunks)])
    memory = _memory_items(qs_ref, mk_ref, mvt_ref)

    @pl.when(qi > 0)
    def _():
        stage_pipeline(diagonal + block_items(qi - 1, None))
        _softmax_pipeline(memory, m_ref, acc_ref)

    @pl.when(qi == 0)
    def _():
        stage_pipeline(diagonal)
        _softmax_pipeline(memory, m_ref, acc_ref)

    def any_weight_left():
        least = functools.reduce(jnp.minimum, [run_ref[h] for h in range(N_MAIN_HEADS)])
        return (jnp.min(least) < RUN_DEAD).astype(jnp.int32)

    def cond(carry):
        it, live = carry
        return (it < qi) & (live > 0)

    def body(carry):
        it, _ = carry
        stage_pipeline(block_items(qi - 1 - it, None))
        return it + 1, any_weight_left()

    lax.while_loop(cond, body, (jnp.int32(1), any_weight_left()))
    for hp in range(N_MAIN_HEADS // 2):
        pair = jnp.concatenate([acc_ref[2 * hp, 0:HEAD_DIM], acc_ref[2 * hp + 1, 0:HEAD_DIM]], axis=0)
        o_ref[:, hp * PAIR_WIDTH:(hp + 1) * PAIR_WIDTH] = pair.T.astype(o_ref.dtype)
    _softmax_finish(acc_ref, N_MAIN_HEADS, N_MEM_HEADS, o_ref, MAIN_WIDTH)


def _attention(kind, layer, q, k, cp, vt, qm, mk, mvt, batch, seq, n_mem, tq):
    t = q.shape[0]
    nq = seq // tq
    q_spec = pl.BlockSpec((tq, MAIN_WIDTH), lambda b, i: (b * nq + i, 0))
    k_spec = pl.BlockSpec((seq, MAIN_WIDTH), lambda b, i: (b, 0))
    vt_spec = pl.BlockSpec((1,) + vt.shape[1:], lambda b, i: (b, 0, 0, 0))
    qm_spec = pl.BlockSpec((tq, MEM_WIDTH), lambda b, i: (b * nq + i, 0))
    mk_spec = pl.BlockSpec((None, n_mem, MEM_WIDTH), lambda b, i: (layer, b, 0))
    mvt_spec = pl.BlockSpec((None, None, None, MEM_WIDTH, n_mem), lambda b, i: (layer, b, 0, 0, 0))
    o_spec = pl.BlockSpec((tq, MAIN_WIDTH + MEM_WIDTH), lambda b, i: (b * nq + i, 0))
    q_rows = 2 * PAIR_WIDTH if kind == "fox" else PAIR_WIDTH
    scratch = [pltpu.VMEM((N_HEADS, q_rows, tq), BF16),
               pltpu.VMEM((N_HEADS, SUBLANES, tq), F32),
               pltpu.VMEM((N_HEADS, ACC_ROWS, tq), F32)]
    if kind == "fox":
        cp_spec = pl.BlockSpec((seq, LANES), lambda b, i: (b, 0))
        kernel = functools.partial(_fox_attn_kernel, tq=tq)
        in_specs = [q_spec, k_spec, cp_spec, vt_spec, qm_spec, mk_spec, mvt_spec]
        args = (q, k, cp, vt, qm, mk, mvt)
    else:
        kernel = functools.partial(_sb_attn_kernel, tq=tq)
        in_specs = [q_spec, k_spec, vt_spec, qm_spec, mk_spec, mvt_spec]
        args = (q, k, vt, qm, mk, mvt)
        scratch.append(pltpu.VMEM((N_MAIN_HEADS, SUBLANES, tq), F32))
    return pl.pallas_call(
        kernel,
        grid=(batch, nq),
        in_specs=in_specs,
        out_specs=o_spec,
        out_shape=jax.ShapeDtypeStruct((t, MAIN_WIDTH + MEM_WIDTH), BF16),
        scratch_shapes=scratch,
        compiler_params=_params("arbitrary", "arbitrary"),
        name=kind + "_attn",
    )(*args)


def _causal_conv(u, prev, cw, cb):
    r8 = lax.broadcasted_iota(jnp.int32, (SUBLANES, 1), 0)
    out = cb
    for j in range(CONV_WIDTH - 1):
        shift = CONV_WIDTH - 1 - j
        rolled = pltpu.roll(u, shift, axis=0)
        top = jnp.where(r8 < shift, pltpu.roll(prev, shift, axis=0), rolled[:SUBLANES])
        shifted = jnp.concatenate([top, rolled[SUBLANES:]], axis=0)
        out = out + cw[j:j + 1, :] * shifted
    return out + cw[CONV_WIDTH - 1:CONV_WIDTH, :] * u


def _out_ffn_kernel(x_ref, o_ref, wout_ref, g_ref, wup_ref, cw_ref, cb_ref, wd_ref, fg_ref, out_ref,
                    u_ref, act_ref, carry_ref, *, tiles_per_seq, tf, final_norm):
    tm = x_ref.shape[0]
    d_ff = wd_ref.shape[0]

    @pl.when(pl.program_id(0) % tiles_per_seq == 0)
    def _():
        carry_ref[...] = jnp.zeros_like(carry_ref)

    x1 = x_ref[...] + _dot(o_ref[...], wout_ref[...])
    h = (_rms_normalize(x1) * g_ref[...]).astype(BF16)
    rows = min(MATMUL_ROWS, tm)
    for r in range(0, tm, rows):
        u_ref[r:r + rows] = _dot(h[r:r + rows], wup_ref[...])
    for c in range(0, d_ff, tf):
        halves = []
        for half in range(2):
            cols = slice(half * d_ff + c, half * d_ff + c + tf)
            u = u_ref[:, cols]
            prev = carry_ref[:, cols]
            carry_ref[:, cols] = u[tm - SUBLANES:]
            halves.append(_causal_conv(u, prev, cw_ref[:, cols], cb_ref[:, cols]))
        gate, val = halves
        act_ref[:, c:c + tf] = (gate * (1.0 / (1.0 + jnp.exp(-gate))) * val).astype(BF16)
    y = x1 + _dot(act_ref[...], wd_ref[...])
    if final_norm:
        y = _rms_normalize(y) * fg_ref[...]
    out_ref[...] = y


def _layer_spec(stacked, layer):
    return pl.BlockSpec((None,) + stacked.shape[1:], lambda *_: (layer, 0, 0), pipeline_mode=pl.Buffered(1))


def _out_ffn(x2, o, layer, w_out, g, w_up, conv_w, conv_b, w_down, final_g, seq, tm, tf, final_norm):
    t, d = x2.shape
    d_ff = w_down.shape[1]
    stacked = (w_out, g, w_up, conv_w, conv_b, w_down)
    kernel = functools.partial(_out_ffn_kernel, tiles_per_seq=seq // tm, tf=tf, final_norm=final_norm)
    return pl.pallas_call(
        kernel,
        grid=(t // tm,),
        in_specs=[_row_spec(tm, d), _row_spec(tm, o.shape[1])] + [_layer_spec(w, layer) for w in stacked]
        + [_full_spec((1, d))],
        out_specs=_row_spec(tm, d),
        out_shape=jax.ShapeDtypeStruct((t, d), F32),
        scratch_shapes=[pltpu.VMEM((tm, 2 * d_ff), F32), pltpu.VMEM((tm, d_ff), BF16),
                        pltpu.VMEM((SUBLANES, 2 * d_ff), F32)],
        compiler_params=_params("arbitrary"),
        name="out_ffn",
    )(x2, o, *stacked, final_g.reshape(1, d))


def kernel(x, mem, ln_mix_g, w_in_a, b_f_a, w_in_b, ln_kv_g, w_kv, ln_mem_g, w_memkv, w_out, ln_ffn_g,
           w_up, conv_w, conv_b, w_down, final_g):
    batch, seq, d = x.shape
    n_mem = mem.shape[1]
    depth = ln_mix_g.shape[0]
    n_a = w_in_a.shape[0]
    assert w_in_b.shape[0] == 1, "one stick-breaking layer reads the shared K/V"
    tm_proj = min(1024, seq)
    tq_attn = min(256, seq)
    tm_ffn, tf = min(512, seq), 256

    x2 = x.reshape(batch * seq, d)
    mem2 = mem.reshape(batch * n_mem, d)
    ffn_weights = (w_out.astype(BF16), ln_ffn_g.reshape(depth, 1, d), w_up.astype(BF16), conv_w,
                   conv_b.reshape(depth, 1, -1), w_down.astype(BF16))
    mk, mvt = _mem_proj(mem2, ln_mem_g, w_memkv, batch, n_mem)
    for layer in range(depth):
        if layer < n_a:
            q, k, vt, qm, cp = _proj_a(x2, ln_mix_g[layer], w_in_a[layer], b_f_a[layer], seq, tm_proj, tq_attn)
            o = _attention("fox", layer, q, k, cp, vt, qm, mk, mvt, batch, seq, n_mem, tq_attn)
        else:
            q, qm, k_sh, vt_sh = _proj_b(x2, ln_mix_g[layer], ln_kv_g, w_in_b[0], w_kv, seq, tm_proj, tq_attn)
            o = _attention("sb", layer, q, k_sh, None, vt_sh, qm, mk, mvt, batch, seq, n_mem, tq_attn)
        x2 = _out_ffn(x2, o, layer, *ffn_weights, final_g, seq, tm_ffn, tf, final_norm=(layer == depth - 1))
    return x2.reshape(batch, seq, d)
```

```python
import functools

import jax
import jax.numpy as jnp
from jax import lax
from jax.experimental import pallas as pl
from jax.experimental.pallas import tpu as pltpu

HEAD_DIM = 64
N_MAIN_HEADS = 12
N_MEM_HEADS = 4
N_HEADS = N_MAIN_HEADS + N_MEM_HEADS
MAIN_WIDTH = N_MAIN_HEADS * HEAD_DIM
MEM_WIDTH = N_MEM_HEADS * HEAD_DIM
CONV_WIDTH = 3
EPS = 1e-6
Q_SCALE = HEAD_DIM ** -0.5
LOG2_E = 1.4426950408889634

LANES = 128
SUBLANES = 8
BF16_ROWS = 16
PAIR_WIDTH = 2 * HEAD_DIM
GATE_TERMS = 3
VMEM_LIMIT = 56 * 1024 * 1024

MATMUL_ROWS = 256
PREFIX_ROWS = 512

F32 = jnp.float32
BF16 = jnp.bfloat16


def _params(*sem):
    return pltpu.CompilerParams(dimension_semantics=sem, vmem_limit_bytes=VMEM_LIMIT)


def _dot(a, b):
    return jnp.dot(a, b, preferred_element_type=F32)


def _rms_normalize(x):
    return x * lax.rsqrt(jnp.mean(x * x, axis=-1, keepdims=True) + EPS)


def _log_sigmoid(x):
    return jnp.minimum(x, 0.0) - jnp.log1p(jnp.exp(-jnp.abs(x)))


def _split3(x):
    hi = x.astype(BF16)
    r1 = x - hi.astype(F32)
    mid = r1.astype(BF16)
    return hi, mid, (r1 - mid.astype(F32)).astype(BF16)


def _store_chunks(h, w_ref, w_lo, out_ref, width, scale, chunk=256):
    rows = min(MATMUL_ROWS, h.shape[0])
    for r0 in range(0, h.shape[0], rows):
        for c in range(0, width, chunk):
            r = _dot(h[r0:r0 + rows], w_ref[:, w_lo + c:w_lo + c + chunk])
            if scale is not None:
                r = r * scale
            out_ref[r0:r0 + rows, c:c + chunk] = r.astype(out_ref.dtype)


def _store_transposed(h, w_ref, w_lo, out_ref, width, tk, scale=None, chunk=256):
    for r in range(h.shape[0] // tk):
        for c in range(0, width, chunk):
            v = _dot(h[r * tk:(r + 1) * tk], w_ref[:, w_lo + c:w_lo + c + chunk])
            if scale is not None:
                v = v * scale
            out_ref[0, r, c:c + chunk, :] = v.T.astype(out_ref.dtype)


def _proj_a_kernel(x_ref, g_ref, wqkv_ref, wqm_ref, wf_ref, bf_ref,
                   q_ref, k_ref, vt_ref, qm_ref, cp_ref, carry_ref, *, tiles_per_seq, tk):
    tm = x_ref.shape[0]

    @pl.when(pl.program_id(0) % tiles_per_seq == 0)
    def _():
        carry_ref[...] = jnp.zeros_like(carry_ref)

    h = (_rms_normalize(x_ref[...]) * g_ref[...]).astype(BF16)
    pr = min(PREFIX_ROWS, tm)
    f_logits = [_dot(h[r:r + pr], wf_ref[...]) for r in range(0, tm, pr)]
    _store_transposed(h, wqkv_ref, 0, q_ref, MAIN_WIDTH, tk, Q_SCALE)

    row = lax.broadcasted_iota(jnp.int32, (pr, pr), 0)
    col = lax.broadcasted_iota(jnp.int32, (pr, pr), 1)
    tri = jnp.where(row >= col, 1.0, 0.0).astype(BF16)
    sums = [_dot(tri, jnp.concatenate(_split3(_log_sigmoid(f + bf_ref[...])), axis=1)) for f in f_logits]
    _store_chunks(h, wqkv_ref, MAIN_WIDTH, k_ref, MAIN_WIDTH, None)

    rr = lax.broadcasted_iota(jnp.int32, (GATE_TERMS * LANES, LANES), 0)
    cc = lax.broadcasted_iota(jnp.int32, (GATE_TERMS * LANES, LANES), 1)
    head, term = rr & (LANES - 1), rr >> 7
    place = jnp.where((cc == GATE_TERMS * head + term) & (head < N_MAIN_HEADS), 1.0, 0.0).astype(BF16)
    carry = carry_ref[0:1, :]
    cps = []
    for s in sums:
        c = (s[:, :LANES] + s[:, LANES:2 * LANES]) + s[:, 2 * LANES:] + carry
        carry = c[pr - 1:pr, :]
        cps.append(_dot(jnp.concatenate(_split3(-c), axis=1), place))
    carry_ref[0:1, :] = carry
    _store_transposed(h, wqkv_ref, 2 * MAIN_WIDTH, vt_ref, MAIN_WIDTH, tk)
    _store_transposed(h, wqm_ref, 0, qm_ref, MEM_WIDTH, tk, Q_SCALE)
    cp_ref[...] = jnp.concatenate(cps, axis=0).astype(cp_ref.dtype)


def _proj_b_kernel(x_ref, gmix_ref, gkv_ref, win_ref, wkv_ref, q_ref, qm_ref, k_ref, vt_ref, *, tk):
    xn = _rms_normalize(x_ref[...])
    h = (xn * gmix_ref[...]).astype(BF16)
    _store_transposed(h, win_ref, 0, q_ref, MAIN_WIDTH, tk, Q_SCALE)
    _store_transposed(h, win_ref, MAIN_WIDTH, qm_ref, MEM_WIDTH, tk, Q_SCALE)
    hk = (xn * gkv_ref[...]).astype(BF16)
    _store_chunks(hk, wkv_ref, 0, k_ref, MAIN_WIDTH, None)
    _store_transposed(hk, wkv_ref, MAIN_WIDTH, vt_ref, MAIN_WIDTH, tk)


def _mem_proj_kernel(x_ref, g_ref, w_ref, mk_ref, mvt_ref):
    n_mem = x_ref.shape[0]
    xn = _rms_normalize(x_ref[...])
    for layer in range(g_ref.shape[0]):
        h = (xn * g_ref[layer]).astype(BF16)
        _store_chunks(h, w_ref.at[layer], 0, mk_ref.at[layer], MEM_WIDTH, None)
        _store_transposed(h, w_ref.at[layer], MEM_WIDTH, mvt_ref.at[layer], MEM_WIDTH, n_mem)


def _row_spec(tm, width):
    return pl.BlockSpec((tm, width), lambda i: (i, 0))


def _full_spec(shape):
    return pl.BlockSpec(shape, lambda *_: (0,) * len(shape))


def _transposed_out(t, seq, tm, tk, width):
    tiles = seq // tm
    shape = (t // seq, seq // tk, width, tk)
    spec = pl.BlockSpec((1, tm // tk, width, tk), lambda i: (i // tiles, i % tiles, 0, 0))
    return spec, jax.ShapeDtypeStruct(shape, BF16)


def _proj_a(x2, g, w_in, b_f, seq, tm, tk):
    t, d = x2.shape
    n_f = N_MAIN_HEADS
    wqkv = w_in[:, :3 * MAIN_WIDTH].astype(BF16)
    wf = jnp.pad(w_in[:, 3 * MAIN_WIDTH:3 * MAIN_WIDTH + n_f], ((0, 0), (0, LANES - n_f))).astype(BF16)
    wqm = w_in[:, 3 * MAIN_WIDTH + n_f:].astype(BF16)
    bf = jnp.pad(b_f, (0, LANES - n_f)).reshape(1, LANES)
    vt_spec, vt_shape = _transposed_out(t, seq, tm, tk, MAIN_WIDTH)
    qmt_spec, qmt_shape = _transposed_out(t, seq, tm, tk, MEM_WIDTH)
    kernel = functools.partial(_proj_a_kernel, tiles_per_seq=seq // tm, tk=tk)
    return pl.pallas_call(
        kernel,
        grid=(t // tm,),
        in_specs=[_row_spec(tm, d), _full_spec((1, d)), _full_spec(wqkv.shape), _full_spec(wqm.shape),
                  _full_spec(wf.shape), _full_spec((1, LANES))],
        out_specs=[vt_spec, _row_spec(tm, MAIN_WIDTH), vt_spec, qmt_spec, _row_spec(tm, LANES)],
        out_shape=[vt_shape, jax.ShapeDtypeStruct((t, MAIN_WIDTH), BF16), vt_shape, qmt_shape,
                   jax.ShapeDtypeStruct((t, LANES), BF16)],
        scratch_shapes=[pltpu.VMEM((SUBLANES, LANES), F32)],
        compiler_params=_params("arbitrary"),
        name="proj_a",
    )(x2, g.reshape(1, d), wqkv, wqm, wf, bf)


def _proj_b(x2, g_mix, g_kv, w_in, w_kv, seq, tm, tk):
    t, d = x2.shape
    win = w_in.astype(BF16)
    wkv = w_kv.astype(BF16)
    vt_spec, vt_shape = _transposed_out(t, seq, tm, tk, MAIN_WIDTH)
    qmt_spec, qmt_shape = _transposed_out(t, seq, tm, tk, MEM_WIDTH)
    return pl.pallas_call(
        functools.partial(_proj_b_kernel, tk=tk),
        grid=(t // tm,),
        in_specs=[_row_spec(tm, d), _full_spec((1, d)), _full_spec((1, d)), _full_spec(win.shape),
                  _full_spec(wkv.shape)],
        out_specs=[vt_spec, qmt_spec, _row_spec(tm, MAIN_WIDTH), vt_spec],
        out_shape=[vt_shape, qmt_shape, jax.ShapeDtypeStruct((t, MAIN_WIDTH), BF16), vt_shape],
        compiler_params=_params("arbitrary"),
        name="proj_b",
    )(x2, g_mix.reshape(1, d), g_kv.reshape(1, d), win, wkv)


def _mem_proj(mem2, g, w, batch, n_mem):
    depth, d = g.shape
    wb = w.astype(BF16)
    return pl.pallas_call(
        _mem_proj_kernel,
        grid=(batch,),
        in_specs=[_row_spec(n_mem, d), _full_spec((depth, 1, d)), _full_spec(wb.shape)],
        out_specs=[pl.BlockSpec((depth, n_mem, MEM_WIDTH), lambda i: (0, i, 0)),
                   pl.BlockSpec((depth, 1, 1, MEM_WIDTH, n_mem), lambda i: (0, i, 0, 0, 0))],
        out_shape=[jax.ShapeDtypeStruct((depth, batch * n_mem, MEM_WIDTH), BF16),
                   jax.ShapeDtypeStruct((depth, batch, 1, MEM_WIDTH, n_mem), BF16)],
        compiler_params=_params("arbitrary"),
        name="mem_proj",
    )(mem2, g.reshape(depth, 1, d), wb)


ACC_ROWS = HEAD_DIM + BF16_ROWS
RUN_DEAD = 105.0
LOOKAHEAD = 6
SCORE_AHEAD = 3
SUFFIX_AHEAD = 3


def _split_heads_into(qt_ref, dst_ref, first, n_heads):
    row = lax.broadcasted_iota(jnp.int32, (PAIR_WIDTH, 1), 0)
    low = row < HEAD_DIM
    for hp in range(n_heads // 2):
        qt = qt_ref[0, 0, hp * PAIR_WIDTH:(hp + 1) * PAIR_WIDTH, :]
        zero = jnp.zeros_like(qt)
        for sub in range(2):
            keep = low if sub == 0 else jnp.logical_not(low)
            dst_ref[first + 2 * hp + sub, 0:PAIR_WIDTH, :] = jnp.where(keep, qt, zero)


def _write_gate_rows(dst_ref):
    tq = dst_ref.shape[2]
    row = lax.broadcasted_iota(jnp.int32, (PAIR_WIDTH, 1), 0)
    for head in range(N_MAIN_HEADS):
        pick = (row >= GATE_TERMS * head) & (row < GATE_TERMS * (head + 1))
        dst_ref[head, PAIR_WIDTH:, :] = jnp.broadcast_to(
            jnp.where(pick, 1.0, 0.0).astype(dst_ref.dtype), (PAIR_WIDTH, tq))


def _values_with_ones(vt2, sub):
    vt = vt2[sub * HEAD_DIM:(sub + 1) * HEAD_DIM]
    return jnp.concatenate([vt, jnp.ones((BF16_ROWS, vt.shape[1]), vt.dtype)], axis=0)


def _softmax_update(s, head, vt_h, m_ref, acc_ref):
    m_prev = m_ref[head][0:1]
    m_new = jnp.maximum(m_prev, jnp.max(s, axis=0, keepdims=True))
    p = jnp.exp((s - m_new).astype(BF16))
    acc_ref[head] = jnp.exp(m_prev - m_new) * acc_ref[head] + _dot(vt_h, p)
    m_ref[head] = jnp.broadcast_to(m_new, m_ref.shape[1:])


def _softmax_finish(acc_ref, first, n_heads, o_ref, col0):
    for hp in range(n_heads // 2):
        outs = []
        for sub in range(2):
            acc = acc_ref[first + 2 * hp + sub]
            outs.append(acc[:HEAD_DIM] / acc[HEAD_DIM:HEAD_DIM + 1])
        o_ref[:, col0 + hp * PAIR_WIDTH:col0 + (hp + 1) * PAIR_WIDTH] = (
            jnp.concatenate(outs, axis=0).T.astype(o_ref.dtype))


def _init_softmax_state(m_ref, acc_ref):
    m_ref[...] = jnp.full_like(m_ref, -jnp.inf)
    acc_ref[...] = jnp.zeros_like(acc_ref)


def _pair_cols(head):
    return slice((head // 2) * PAIR_WIDTH, (head // 2 + 1) * PAIR_WIDTH)


def _softmax_pipeline(items, m_ref, acc_ref, lookahead=LOOKAHEAD):
    pending = [item[0]() for item in items[:lookahead]]
    for n, (_, values_fn, head) in enumerate(items):
        s = pending.pop(0)
        if n + lookahead < len(items):
            pending.append(items[n + lookahead][0]())
        _softmax_update(s, head, values_fn(), m_ref, acc_ref)


def _memory_items(qs_ref, mk_ref, mvt_ref):
    def score(i):
        return _dot(mk_ref[:, _pair_cols(i)], qs_ref[N_MAIN_HEADS + i, 0:PAIR_WIDTH, :])

    def values(i):
        return _values_with_ones(mvt_ref[_pair_cols(i), :], i % 2)

    return [(functools.partial(score, i), functools.partial(values, i), N_MAIN_HEADS + i)
            for i in range(N_MEM_HEADS)]


def _fox_attn_kernel(q_ref, k_ref, cp_ref, vt_ref, qm_ref, mk_ref, mvt_ref, o_ref, qs_ref, m_ref, acc_ref, *, tq):
    qi = pl.program_id(1)
    tk = tq
    _write_gate_rows(qs_ref)
    _split_heads_into(q_ref, qs_ref, 0, N_MAIN_HEADS)
    _split_heads_into(qm_ref, qs_ref, N_MAIN_HEADS, N_MEM_HEADS)
    _init_softmax_state(m_ref, acc_ref)

    def block_items(kb, causal):
        ks = pl.multiple_of(kb * tk, tk)

        def score(head):
            kk = jnp.concatenate([k_ref[pl.ds(ks, tk), _pair_cols(head)], cp_ref[pl.ds(ks, tk), :]], axis=1)
            s = _dot(kk, qs_ref[head])
            return s if causal is None else jnp.where(causal, s, -jnp.inf)

        def values(head):
            return _values_with_ones(vt_ref[0, kb, _pair_cols(head), :], head % 2)

        return [(functools.partial(score, h), functools.partial(values, h), h) for h in range(N_MAIN_HEADS)]

    def two_blocks(it, carry):
        _softmax_pipeline(block_items(2 * it, None) + block_items(2 * it + 1, None), m_ref, acc_ref)
        return carry

    lax.fori_loop(0, qi >> 1, two_blocks, 0)
    key = lax.broadcasted_iota(jnp.int32, (tk, tq), 0)
    qry = lax.broadcasted_iota(jnp.int32, (tk, tq), 1)
    tail = block_items(qi, key <= qry) + _memory_items(qs_ref, mk_ref, mvt_ref)

    def finish(items):
        _softmax_pipeline(items, m_ref, acc_ref)
        _softmax_finish(acc_ref, 0, N_MAIN_HEADS, o_ref, 0)
        _softmax_finish(acc_ref, N_MAIN_HEADS, N_MEM_HEADS, o_ref, MAIN_WIDTH)

    @pl.when((qi & 1) == 1)
    def _():
        finish(block_items(qi - 1, None) + tail)

    @pl.when((qi & 1) == 0)
    def _():
        finish(tail)


def _sb_attn_kernel(q_ref, k_ref, vt_ref, qm_ref, mk_ref, mvt_ref, o_ref, qs_ref, m_ref, acc_ref, run_ref, *, tq):
    qi = pl.program_id(1)
    tk = tq
    _split_heads_into(q_ref, qs_ref, 0, N_MAIN_HEADS)
    _split_heads_into(qm_ref, qs_ref, N_MAIN_HEADS, N_MEM_HEADS)
    _init_softmax_state(m_ref, acc_ref)
    run_ref[...] = jnp.zeros_like(run_ref)

    ss = lax.broadcasted_iota(jnp.int32, (LANES + BF16_ROWS, 2 * LANES), 0)
    jj = lax.broadcasted_iota(jnp.int32, (LANES + BF16_ROWS, 2 * LANES), 1) & (LANES - 1)
    suffix = jnp.where((jj >= ss) | (ss >= LANES), 1.0, 0.0).astype(BF16)

    n_chunks = tk // LANES

    def stage_pipeline(items):
        def score_stage(kb, causal, head):
            ks = pl.multiple_of(kb * tk, tk)
            z = _dot(k_ref[pl.ds(ks, tk), _pair_cols(head)], qs_ref[head])
            logits, split = [], []
            for c in range(n_chunks):
                zc = z[c * LANES:(c + 1) * LANES]
                if causal is not None:
                    zc = jnp.where(causal[c], zc, -jnp.inf)
                nl = jnp.maximum(zc, 0.0) + jnp.log(1.0 + jnp.exp2(jnp.abs(zc) * -LOG2_E))
                hi = nl.astype(BF16)
                logits.append(zc)
                split.append(jnp.concatenate([hi, (nl - hi.astype(F32)).astype(BF16)], axis=0))
            return logits, split

        def suffix_stage(state):
            logits, split = state
            return logits, [_dot(suffix, sp) for sp in split]

        def value_stage(kb, causal, head, state):
            logits, sums = state
            run = run_ref[head][0:1]
            a_chunks = [None] * n_chunks
            for c in reversed(range(n_chunks)):
                a_chunks[c] = jnp.exp((logits[c] - (sums[c][:LANES] + run)).astype(BF16))
                run = run + sums[c][LANES:LANES + 1]
            run_ref[head] = jnp.broadcast_to(run, run_ref.shape[1:])
            vt = vt_ref[0, kb, _pair_cols(head), :][(head % 2) * HEAD_DIM:(head % 2 + 1) * HEAD_DIM]
            acc_ref[head, 0:HEAD_DIM] += _dot(vt, jnp.concatenate(a_chunks, axis=0))

        scored, summed = {}, {}
        for step in range(len(items) + SCORE_AHEAD + SUFFIX_AHEAD):
            if step < len(items):
                scored[step] = score_stage(*items[step])
            i2 = step - SCORE_AHEAD
            if 0 <= i2 < len(items):
                summed[i2] = suffix_stage(scored.pop(i2))
            i3 = i2 - SUFFIX_AHEAD
            if 0 <= i3 < len(items):
                value_stage(*items[i3], summed.pop(i3))

    def block_items(kb, causal):
        return [(kb, causal, head) for head in range(N_MAIN_HEADS)]

    key = lax.broadcasted_iota(jnp.int32, (LANES, tq), 0)
    qry = lax.broadcasted_iota(jnp.int32, (LANES, tq), 1)
    diagonal = block_items(qi, [(key + c * LANES) < qry for c in range(n_chunks)])
    memory = _memory_items(qs_ref, mk_ref, mvt_ref)

    @pl.when(qi > 0)
    def _():
        stage_pipeline(diagonal + block_items(qi - 1, None))
        _softmax_pipeline(memory, m_ref, acc_ref)

    @pl.when(qi == 0)
    def _():
        stage_pipeline(diagonal)
        _softmax_pipeline(memory, m_ref, acc_ref)

    def any_weight_left():
        least = functools.reduce(jnp.minimum, [run_ref[h] for h in range(N_MAIN_HEADS)])
        return (jnp.min(least) < RUN_DEAD).astype(jnp.int32)

    def cond(carry):
        it, live = carry
        return (it < qi) & (live > 0)

    def body(carry):
        it, _ = carry
        stage_pipeline(block_items(qi - 1 - it, None))
        return it + 1, any_weight_left()

    lax.while_loop(cond, body, (jnp.int32(1), any_weight_left()))
    for hp in range(N_MAIN_HEADS // 2):
        pair = jnp.concatenate([acc_ref[2 * hp, 0:HEAD_DIM], acc_ref[2 * hp + 1, 0:HEAD_DIM]], axis=0)
        o_ref[:, hp * PAIR_WIDTH:(hp + 1) * PAIR_WIDTH] = pair.T.astype(o_ref.dtype)
    _softmax_finish(acc_ref, N_MAIN_HEADS, N_MEM_HEADS, o_ref, MAIN_WIDTH)


def _attention(kind, layer, q, k, cp, vt, qm, mk, mvt, batch, seq, n_mem, tq):
    t = k.shape[0]
    nq = seq // tq
    q_spec = pl.BlockSpec((1, 1, MAIN_WIDTH, tq), lambda b, i: (b, i, 0, 0))
    k_spec = pl.BlockSpec((seq, MAIN_WIDTH), lambda b, i: (b, 0))
    vt_spec = pl.BlockSpec((1,) + vt.shape[1:], lambda b, i: (b, 0, 0, 0))
    qm_spec = pl.BlockSpec((1, 1, MEM_WIDTH, tq), lambda b, i: (b, i, 0, 0))
    mk_spec = pl.BlockSpec((None, n_mem, MEM_WIDTH), lambda b, i: (layer, b, 0))
    mvt_spec = pl.BlockSpec((None, None, None, MEM_WIDTH, n_mem), lambda b, i: (layer, b, 0, 0, 0))
    o_spec = pl.BlockSpec((tq, MAIN_WIDTH + MEM_WIDTH), lambda b, i: (b * nq + i, 0))
    q_rows = 2 * PAIR_WIDTH if kind == "fox" else PAIR_WIDTH
    scratch = [pltpu.VMEM((N_HEADS, q_rows, tq), BF16),
               pltpu.VMEM((N_HEADS, SUBLANES, tq), F32),
               pltpu.VMEM((N_HEADS, ACC_ROWS, tq), F32)]
    if kind == "fox":
        cp_spec = pl.BlockSpec((seq, LANES), lambda b, i: (b, 0))
        kernel = functools.partial(_fox_attn_kernel, tq=tq)
        in_specs = [q_spec, k_spec, cp_spec, vt_spec, qm_spec, mk_spec, mvt_spec]
        args = (q, k, cp, vt, qm, mk, mvt)
    else:
        kernel = functools.partial(_sb_attn_kernel, tq=tq)
        in_specs = [q_spec, k_spec, vt_spec, qm_spec, mk_spec, mvt_spec]
        args = (q, k, vt, qm, mk, mvt)
        scratch.append(pltpu.VMEM((N_MAIN_HEADS, SUBLANES, tq), F32))
    return pl.pallas_call(
        kernel,
        grid=(batch, nq),
        in_specs=in_specs,
        out_specs=o_spec,
        out_shape=jax.ShapeDtypeStruct((t, MAIN_WIDTH + MEM_WIDTH), BF16),
        scratch_shapes=scratch,
        compiler_params=_params("arbitrary", "arbitrary"),
        name=kind + "_attn",
    )(*args)


def _causal_conv(u, prev, cw, cb):
    r8 = lax.broadcasted_iota(jnp.int32, (SUBLANES, 1), 0)
    out = cb
    for j in range(CONV_WIDTH - 1):
        shift = CONV_WIDTH - 1 - j
        rolled = pltpu.roll(u, shift, axis=0)
        top = jnp.where(r8 < shift, pltpu.roll(prev, shift, axis=0), rolled[:SUBLANES])
        shifted = jnp.concatenate([top, rolled[SUBLANES:]], axis=0)
        out = out + cw[j:j + 1, :] * shifted
    return out + cw[CONV_WIDTH - 1:CONV_WIDTH, :] * u


def _out_ffn_kernel(x_ref, o_ref, wout_ref, g_ref, wup_ref, cw_ref, cb_ref, wd_ref, fg_ref, out_ref,
                    u_ref, act_ref, carry_ref, *, tiles_per_seq, tf, final_norm):
    tm = x_ref.shape[0]
    d_ff = wd_ref.shape[0]

    @pl.when(pl.program_id(0) % tiles_per_seq == 0)
    def _():
        carry_ref[...] = jnp.zeros_like(carry_ref)

    x1 = x_ref[...] + _dot(o_ref[...], wout_ref[...])
    h = (_rms_normalize(x1) * g_ref[...]).astype(BF16)
    rows = min(MATMUL_ROWS, tm)
    for r in range(0, tm, rows):
        u_ref[r:r + rows] = _dot(h[r:r + rows], wup_ref[...])
    for c in range(0, d_ff, tf):
        halves = []
        for half in range(2):
            cols = slice(half * d_ff + c, half * d_ff + c + tf)
            u = u_ref[:, cols]
            prev = carry_ref[:, cols]
            carry_ref[:, cols] = u[tm - SUBLANES:]
            halves.append(_causal_conv(u, prev, cw_ref[:, cols], cb_ref[:, cols]))
        gate, val = halves
        act_ref[:, c:c + tf] = (gate * (1.0 / (1.0 + jnp.exp(-gate))) * val).astype(BF16)
    y = x1 + _dot(act_ref[...], wd_ref[...])
    if final_norm:
        y = _rms_normalize(y) * fg_ref[...]
    out_ref[...] = y


def _layer_spec(stacked, layer):
    return pl.BlockSpec((None,) + stacked.shape[1:], lambda *_: (layer, 0, 0), pipeline_mode=pl.Buffered(1))


def _out_ffn(x2, o, layer, w_out, g, w_up, conv_w, conv_b, w_down, final_g, seq, tm, tf, final_norm):
    t, d = x2.shape
    d_ff = w_down.shape[1]
    stacked = (w_out, g, w_up, conv_w, conv_b, w_down)
    kernel = functools.partial(_out_ffn_kernel, tiles_per_seq=seq // tm, tf=tf, final_norm=final_norm)
    return pl.pallas_call(
        kernel,
        grid=(t // tm,),
        in_specs=[_row_spec(tm, d), _row_spec(tm, o.shape[1])] + [_layer_spec(w, layer) for w in stacked]
        + [_full_spec((1, d))],
        out_specs=_row_spec(tm, d),
        out_shape=jax.ShapeDtypeStruct((t, d), F32),
        scratch_shapes=[pltpu.VMEM((tm, 2 * d_ff), F32), pltpu.VMEM((tm, d_ff), BF16),
                        pltpu.VMEM((SUBLANES, 2 * d_ff), F32)],
        compiler_params=_params("arbitrary"),
        name="out_ffn",
    )(x2, o, *stacked, final_g.reshape(1, d))


def kernel(x, mem, ln_mix_g, w_in_a, b_f_a, w_in_b, ln_kv_g, w_kv, ln_mem_g, w_memkv, w_out, ln_ffn_g,
           w_up, conv_w, conv_b, w_down, final_g):
    batch, seq, d = x.shape
    n_mem = mem.shape[1]
    depth = ln_mix_g.shape[0]
    n_a = w_in_a.shape[0]
    assert w_in_b.shape[0] == 1, "one stick-breaking layer reads the shared K/V"
    tm_proj = min(1024, seq)
    tq_attn = min(256, seq)
    tm_ffn, tf = min(512, seq), 256

    x2 = x.reshape(batch * seq, d)
    mem2 = mem.reshape(batch * n_mem, d)
    ffn_weights = (w_out.astype(BF16), ln_ffn_g.reshape(depth, 1, d), w_up.astype(BF16), conv_w,
                   conv_b.reshape(depth, 1, -1), w_down.astype(BF16))
    mk, mvt = _mem_proj(mem2, ln_mem_g, w_memkv, batch, n_mem)
    for layer in range(depth):
        if layer < n_a:
            q, k, vt, qm, cp = _proj_a(x2, ln_mix_g[layer], w_in_a[layer], b_f_a[layer], seq, tm_proj, tq_attn)
            o = _attention("fox", layer, q, k, cp, vt, qm, mk, mvt, batch, seq, n_mem, tq_attn)
        else:
            q, qm, k_sh, vt_sh = _proj_b(x2, ln_mix_g[layer], ln_kv_g, w_in_b[0], w_kv, seq, tm_proj, tq_attn)
            o = _attention("sb", layer, q, k_sh, None, vt_sh, qm, mk, mvt, batch, seq, n_mem, tq_attn)
        x2 = _out_ffn(x2, o, layer, *ffn_weights, final_g, seq, tm_ffn, tf, final_norm=(layer == depth - 1))
    return x2.reshape(batch, seq, d)
```

```python
import functools

import jax
import jax.numpy as jnp
from jax import lax
from jax.experimental import pallas as pl
from jax.experimental.pallas import tpu as pltpu

HEAD_DIM = 64
N_MAIN_HEADS = 12
N_MEM_HEADS = 4
N_HEADS = N_MAIN_HEADS + N_MEM_HEADS
MAIN_WIDTH = N_MAIN_HEADS * HEAD_DIM
MEM_WIDTH = N_MEM_HEADS * HEAD_DIM
CONV_WIDTH = 3
EPS = 1e-6
Q_SCALE = HEAD_DIM ** -0.5
LOG2_E = 1.4426950408889634

LANES = 128
SUBLANES = 8
BF16_ROWS = 16
PAIR_WIDTH = 2 * HEAD_DIM
GATE_TERMS = 3
VMEM_LIMIT = 56 * 1024 * 1024

MATMUL_ROWS = 256
PREFIX_ROWS = 512

F32 = jnp.float32
BF16 = jnp.bfloat16


def _params(*sem):
    return pltpu.CompilerParams(dimension_semantics=sem, vmem_limit_bytes=VMEM_LIMIT)


def _dot(a, b):
    return jnp.dot(a, b, preferred_element_type=F32)


def _rms_normalize(x):
    return x * lax.rsqrt(jnp.mean(x * x, axis=-1, keepdims=True) + EPS)


def _log_sigmoid(x):
    return jnp.minimum(x, 0.0) - jnp.log1p(jnp.exp(-jnp.abs(x)))


def _split3(x):
    hi = x.astype(BF16)
    r1 = x - hi.astype(F32)
    mid = r1.astype(BF16)
    return hi, mid, (r1 - mid.astype(F32)).astype(BF16)


def _store_chunks(h, w_ref, w_lo, out_ref, width, scale, chunk=256):
    rows = min(MATMUL_ROWS, h.shape[0])
    for r0 in range(0, h.shape[0], rows):
        for c in range(0, width, chunk):
            r = _dot(h[r0:r0 + rows], w_ref[:, w_lo + c:w_lo + c + chunk])
            if scale is not None:
                r = r * scale
            out_ref[r0:r0 + rows, c:c + chunk] = r.astype(out_ref.dtype)


def _store_transposed(h, w_ref, w_lo, out_ref, width, tk, scale=None, chunk=256):
    for r in range(h.shape[0] // tk):
        for c in range(0, width, chunk):
            v = _dot(h[r * tk:(r + 1) * tk], w_ref[:, w_lo + c:w_lo + c + chunk])
            if scale is not None:
                v = v * scale
            out_ref[0, r, c:c + chunk, :] = v.T.astype(out_ref.dtype)


def _proj_a_kernel(x_ref, g_ref, wqkv_ref, wqm_ref, wf_ref, bf_ref,
                   q_ref, k_ref, vt_ref, qm_ref, cp_ref, carry_ref, *, tiles_per_seq, tk):
    tm = x_ref.shape[0]

    @pl.when(pl.program_id(0) % tiles_per_seq == 0)
    def _():
        carry_ref[...] = jnp.zeros_like(carry_ref)

    h = (_rms_normalize(x_ref[...]) * g_ref[...]).astype(BF16)
    pr = min(PREFIX_ROWS, tm)
    f_logits = [_dot(h[r:r + pr], wf_ref[...]) for r in range(0, tm, pr)]
    _store_transposed(h, wqkv_ref, 0, q_ref, MAIN_WIDTH, tk, Q_SCALE)

    row = lax.broadcasted_iota(jnp.int32, (pr, pr), 0)
    col = lax.broadcasted_iota(jnp.int32, (pr, pr), 1)
    tri = jnp.where(row >= col, 1.0, 0.0).astype(BF16)
    sums = [_dot(tri, jnp.concatenate(_split3(_log_sigmoid(f + bf_ref[...])), axis=1)) for f in f_logits]
    _store_chunks(h, wqkv_ref, MAIN_WIDTH, k_ref, MAIN_WIDTH, None)

    rr = lax.broadcasted_iota(jnp.int32, (GATE_TERMS * LANES, LANES), 0)
    cc = lax.broadcasted_iota(jnp.int32, (GATE_TERMS * LANES, LANES), 1)
    head, term = rr & (LANES - 1), rr >> 7
    place = jnp.where((cc == GATE_TERMS * head + term) & (head < N_MAIN_HEADS), 1.0, 0.0).astype(BF16)
    carry = carry_ref[0:1, :]
    cps = []
    for s in sums:
        c = (s[:, :LANES] + s[:, LANES:2 * LANES]) + s[:, 2 * LANES:] + carry
        carry = c[pr - 1:pr, :]
        cps.append(_dot(jnp.concatenate(_split3(-c), axis=1), place))
    carry_ref[0:1, :] = carry
    _store_transposed(h, wqkv_ref, 2 * MAIN_WIDTH, vt_ref, MAIN_WIDTH, tk)
    _store_transposed(h, wqm_ref, 0, qm_ref, MEM_WIDTH, tk, Q_SCALE)
    cp_ref[...] = jnp.concatenate(cps, axis=0).astype(cp_ref.dtype)


def _proj_b_kernel(x_ref, gmix_ref, gkv_ref, win_ref, wkv_ref, q_ref, qm_ref, k_ref, vt_ref, *, tk):
    xn = _rms_normalize(x_ref[...])
    h = (xn * gmix_ref[...]).astype(BF16)
    _store_transposed(h, win_ref, 0, q_ref, MAIN_WIDTH, tk, Q_SCALE)
    _store_transposed(h, win_ref, MAIN_WIDTH, qm_ref, MEM_WIDTH, tk, Q_SCALE)
    hk = (xn * gkv_ref[...]).astype(BF16)
    _store_chunks(hk, wkv_ref, 0, k_ref, MAIN_WIDTH, None)
    _store_transposed(hk, wkv_ref, MAIN_WIDTH, vt_ref, MAIN_WIDTH, tk)


def _mem_proj_kernel(x_ref, g_ref, w_ref, mk_ref, mvt_ref):
    n_mem = x_ref.shape[0]
    xn = _rms_normalize(x_ref[...])
    for layer in range(g_ref.shape[0]):
        h = (xn * g_ref[layer]).astype(BF16)
        _store_chunks(h, w_ref.at[layer], 0, mk_ref.at[layer], MEM_WIDTH, None)
        _store_transposed(h, w_ref.at[layer], MEM_WIDTH, mvt_ref.at[layer], MEM_WIDTH, n_mem)


def _row_spec(tm, width):
    return pl.BlockSpec((tm, width), lambda i: (i, 0))


def _full_spec(shape):
    return pl.BlockSpec(shape, lambda *_: (0,) * len(shape))


def _transposed_out(t, seq, tm, tk, width):
    tiles = seq // tm
    shape = (t // seq, seq // tk, width, tk)
    spec = pl.BlockSpec((1, tm // tk, width, tk), lambda i: (i // tiles, i % tiles, 0, 0))
    return spec, jax.ShapeDtypeStruct(shape, BF16)


def _proj_a(x2, g, w_in, b_f, seq, tm, tk):
    t, d = x2.shape
    n_f = N_MAIN_HEADS
    wqkv = w_in[:, :3 * MAIN_WIDTH].astype(BF16)
    wf = jnp.pad(w_in[:, 3 * MAIN_WIDTH:3 * MAIN_WIDTH + n_f], ((0, 0), (0, LANES - n_f))).astype(BF16)
    wqm = w_in[:, 3 * MAIN_WIDTH + n_f:].astype(BF16)
    bf = jnp.pad(b_f, (0, LANES - n_f)).reshape(1, LANES)
    vt_spec, vt_shape = _transposed_out(t, seq, tm, tk, MAIN_WIDTH)
    qmt_spec, qmt_shape = _transposed_out(t, seq, tm, tk, MEM_WIDTH)
    kernel = functools.partial(_proj_a_kernel, tiles_per_seq=seq // tm, tk=tk)
    return pl.pallas_call(
        kernel,
        grid=(t // tm,),
        in_specs=[_row_spec(tm, d), _full_spec((1, d)), _full_spec(wqkv.shape), _full_spec(wqm.shape),
                  _full_spec(wf.shape), _full_spec((1, LANES))],
        out_specs=[vt_spec, _row_spec(tm, MAIN_WIDTH), vt_spec, qmt_spec, _row_spec(tm, LANES)],
        out_shape=[vt_shape, jax.ShapeDtypeStruct((t, MAIN_WIDTH), BF16), vt_shape, qmt_shape,
                   jax.ShapeDtypeStruct((t, LANES), BF16)],
        scratch_shapes=[pltpu.VMEM((SUBLANES, LANES), F32)],
        compiler_params=_params("arbitrary"),
        name="proj_a",
    )(x2, g.reshape(1, d), wqkv, wqm, wf, bf)


def _proj_b(x2, g_mix, g_kv, w_in, w_kv, seq, tm, tk):
    t, d = x2.shape
    win = w_in.astype(BF16)
    wkv = w_kv.astype(BF16)
    vt_spec, vt_shape = _transposed_out(t, seq, tm, tk, MAIN_WIDTH)
    qmt_spec, qmt_shape = _transposed_out(t, seq, tm, tk, MEM_WIDTH)
    return pl.pallas_call(
        functools.partial(_proj_b_kernel, tk=tk),
        grid=(t // tm,),
        in_specs=[_row_spec(tm, d), _full_spec((1, d)), _full_spec((1, d)), _full_spec(win.shape),
                  _full_spec(wkv.shape)],
        out_specs=[vt_spec, qmt_spec, _row_spec(tm, MAIN_WIDTH), vt_spec],
        out_shape=[vt_shape, qmt_shape, jax.ShapeDtypeStruct((t, MAIN_WIDTH), BF16), vt_shape],
        compiler_params=_params("arbitrary"),
        name="proj_b",
    )(x2, g_mix.reshape(1, d), g_kv.reshape(1, d), win, wkv)


def _mem_proj(mem2, g, w, batch, n_mem):
    depth, d = g.shape
    wb = w.astype(BF16)
    return pl.pallas_call(
        _mem_proj_kernel,
        grid=(batch,),
        in_specs=[_row_spec(n_mem, d), _full_spec((depth, 1, d)), _full_spec(wb.shape)],
        out_specs=[pl.BlockSpec((depth, n_mem, MEM_WIDTH), lambda i: (0, i, 0)),
                   pl.BlockSpec((depth, 1, 1, MEM_WIDTH, n_mem), lambda i: (0, i, 0, 0, 0))],
        out_shape=[jax.ShapeDtypeStruct((depth, batch * n_mem, MEM_WIDTH), BF16),
                   jax.ShapeDtypeStruct((depth, batch, 1, MEM_WIDTH, n_mem), BF16)],
        compiler_params=_params("arbitrary"),
        name="mem_proj",
    )(mem2, g.reshape(depth, 1, d), wb)


ACC_ROWS = HEAD_DIM + BF16_ROWS
RUN_DEAD = 105.0
LOOKAHEAD = 12
SCORE_AHEAD = 3
SUFFIX_AHEAD = 3


def _split_heads_into(qt_ref, dst_ref, first, n_heads):
    row = lax.broadcasted_iota(jnp.int32, (PAIR_WIDTH, 1), 0)
    low = row < HEAD_DIM
    for hp in range(n_heads // 2):
        qt = qt_ref[0, 0, hp * PAIR_WIDTH:(hp + 1) * PAIR_WIDTH, :]
        zero = jnp.zeros_like(qt)
        for sub in range(2):
            keep = low if sub == 0 else jnp.logical_not(low)
            dst_ref[first + 2 * hp + sub, 0:PAIR_WIDTH, :] = jnp.where(keep, qt, zero)


def _write_gate_rows(dst_ref):
    tq = dst_ref.shape[2]
    row = lax.broadcasted_iota(jnp.int32, (PAIR_WIDTH, 1), 0)
    for head in range(N_MAIN_HEADS):
        pick = (row >= GATE_TERMS * head) & (row < GATE_TERMS * (head + 1))
        dst_ref[head, PAIR_WIDTH:, :] = jnp.broadcast_to(
            jnp.where(pick, 1.0, 0.0).astype(dst_ref.dtype), (PAIR_WIDTH, tq))


def _values_with_ones(vt2, sub):
    vt = vt2[sub * HEAD_DIM:(sub + 1) * HEAD_DIM]
    return jnp.concatenate([vt, jnp.ones((BF16_ROWS, vt.shape[1]), vt.dtype)], axis=0)


def _softmax_update(s, head, vt_h, m_ref, acc_ref):
    m_prev = m_ref[head][0:1]
    m_new = jnp.maximum(m_prev, jnp.max(s, axis=0, keepdims=True))
    p = jnp.exp((s - m_new).astype(BF16))
    acc_ref[head] = jnp.exp(m_prev - m_new) * acc_ref[head] + _dot(vt_h, p)
    m_ref[head] = jnp.broadcast_to(m_new, m_ref.shape[1:])


def _softmax_finish(acc_ref, first, n_heads, o_ref, col0):
    for hp in range(n_heads // 2):
        outs = []
        for sub in range(2):
            acc = acc_ref[first + 2 * hp + sub]
            outs.append(acc[:HEAD_DIM] / acc[HEAD_DIM:HEAD_DIM + 1])
        o_ref[:, col0 + hp * PAIR_WIDTH:col0 + (hp + 1) * PAIR_WIDTH] = (
            jnp.concatenate(outs, axis=0).T.astype(o_ref.dtype))


def _init_softmax_state(m_ref, acc_ref):
    m_ref[...] = jnp.full_like(m_ref, -jnp.inf)
    acc_ref[...] = jnp.zeros_like(acc_ref)


def _pair_cols(head):
    return slice((head // 2) * PAIR_WIDTH, (head // 2 + 1) * PAIR_WIDTH)


def _softmax_pipeline(items, m_ref, acc_ref, lookahead=LOOKAHEAD):
    pending = [item[0]() for item in items[:lookahead]]
    for n, (_, values_fn, head) in enumerate(items):
        s = pending.pop(0)
        if n + lookahead < len(items):
            pending.append(items[n + lookahead][0]())
        _softmax_update(s, head, values_fn(), m_ref, acc_ref)


def _memory_items(qs_ref, mk_ref, mvt_ref):
    def score(i):
        return _dot(mk_ref[:, _pair_cols(i)], qs_ref[N_MAIN_HEADS + i, 0:PAIR_WIDTH, :])

    def values(i):
        return _values_with_ones(mvt_ref[_pair_cols(i), :], i % 2)

    return [(functools.partial(score, i), functools.partial(values, i), N_MAIN_HEADS + i)
            for i in range(N_MEM_HEADS)]


def _fox_attn_kernel(q_ref, k_ref, cp_ref, vt_ref, qm_ref, mk_ref, mvt_ref, o_ref, qs_ref, m_ref, acc_ref, *, tq):
    qi = pl.program_id(1)
    tk = tq
    _write_gate_rows(qs_ref)
    _split_heads_into(q_ref, qs_ref, 0, N_MAIN_HEADS)
    _split_heads_into(qm_ref, qs_ref, N_MAIN_HEADS, N_MEM_HEADS)
    _init_softmax_state(m_ref, acc_ref)

    def block_items(kb, causal):
        ks = pl.multiple_of(kb * tk, tk)

        def score(head):
            kk = jnp.concatenate([k_ref[pl.ds(ks, tk), _pair_cols(head)], cp_ref[pl.ds(ks, tk), :]], axis=1)
            s = _dot(kk, qs_ref[head])
            return s if causal is None else jnp.where(causal, s, -jnp.inf)

        def values(head):
            return _values_with_ones(vt_ref[0, kb, _pair_cols(head), :], head % 2)

        return [(functools.partial(score, h), functools.partial(values, h), h) for h in range(N_MAIN_HEADS)]

    def two_blocks(it, carry):
        _softmax_pipeline(block_items(2 * it, None) + block_items(2 * it + 1, None), m_ref, acc_ref)
        return carry

    lax.fori_loop(0, qi >> 1, two_blocks, 0)
    key = lax.broadcasted_iota(jnp.int32, (tk, tq), 0)
    qry = lax.broadcasted_iota(jnp.int32, (tk, tq), 1)
    tail = block_items(qi, key <= qry) + _memory_items(qs_ref, mk_ref, mvt_ref)

    def finish(items):
        _softmax_pipeline(items, m_ref, acc_ref)
        _softmax_finish(acc_ref, 0, N_MAIN_HEADS, o_ref, 0)
        _softmax_finish(acc_ref, N_MAIN_HEADS, N_MEM_HEADS, o_ref, MAIN_WIDTH)

    @pl.when((qi & 1) == 1)
    def _():
        finish(block_items(qi - 1, None) + tail)

    @pl.when((qi & 1) == 0)
    def _():
        finish(tail)


def _sb_attn_kernel(q_ref, k_ref, vt_ref, qm_ref, mk_ref, mvt_ref, o_ref, qs_ref, m_ref, acc_ref, run_ref, *, tq):
    qi = pl.program_id(1)
    tk = tq
    _split_heads_into(q_ref, qs_ref, 0, N_MAIN_HEADS)
    _split_heads_into(qm_ref, qs_ref, N_MAIN_HEADS, N_MEM_HEADS)
    _init_softmax_state(m_ref, acc_ref)
    run_ref[...] = jnp.zeros_like(run_ref)

    ss = lax.broadcasted_iota(jnp.int32, (LANES + BF16_ROWS, 2 * LANES), 0)
    jj = lax.broadcasted_iota(jnp.int32, (LANES + BF16_ROWS, 2 * LANES), 1) & (LANES - 1)
    suffix = jnp.where((jj >= ss) | (ss >= LANES), 1.0, 0.0).astype(BF16)

    n_chunks = tk // LANES

    def stage_pipeline(items):
        def score_stage(kb, causal, head):
            ks = pl.multiple_of(kb * tk, tk)
            z = _dot(k_ref[pl.ds(ks, tk), _pair_cols(head)], qs_ref[head])
            logits, split = [], []
            for c in range(n_chunks):
                zc = z[c * LANES:(c + 1) * LANES]
                if causal is not None:
                    zc = jnp.where(causal[c], zc, -jnp.inf)
                nl = jnp.maximum(zc, 0.0) + jnp.log(1.0 + jnp.exp2(jnp.abs(zc) * -LOG2_E))
                hi = nl.astype(BF16)
                logits.append(zc)
                split.append(jnp.concatenate([hi, (nl - hi.astype(F32)).astype(BF16)], axis=0))
            return logits, split

        def suffix_stage(state):
            logits, split = state
            return logits, [_dot(suffix, sp) for sp in split]

        def value_stage(kb, causal, head, state):
            logits, sums = state
            run = run_ref[head][0:1]
            a_chunks = [None] * n_chunks
            for c in reversed(range(n_chunks)):
                a_chunks[c] = jnp.exp((logits[c] - (sums[c][:LANES] + run)).astype(BF16))
                run = run + sums[c][LANES:LANES + 1]
            run_ref[head] = jnp.broadcast_to(run, run_ref.shape[1:])
            vt = vt_ref[0, kb, _pair_cols(head), :][(head % 2) * HEAD_DIM:(head % 2 + 1) * HEAD_DIM]
            acc_ref[head, 0:HEAD_DIM] += _dot(vt, jnp.concatenate(a_chunks, axis=0))

        scored, summed = {}, {}
        for step in range(len(items) + SCORE_AHEAD + SUFFIX_AHEAD):
            if step < len(items):
                scored[step] = score_stage(*items[step])
            i2 = step - SCORE_AHEAD
            if 0 <= i2 < len(items):
                summed[i2] = suffix_stage(scored.pop(i2))
            i3 = i2 - SUFFIX_AHEAD
            if 0 <= i3 < len(items):
                value_stage(*items[i3], summed.pop(i3))

    def block_items(kb, causal):
        return [(kb, causal, head) for head in range(N_MAIN_HEADS)]

    key = lax.broadcasted_iota(jnp.int32, (LANES, tq), 0)
    qry = lax.broadcasted_iota(jnp.int32, (LANES, tq), 1)
    diagonal = block_items(qi, [(key + c * LANES) < qry for c in range(n_chunks)])
    memory = _memory_items(qs_ref, mk_ref, mvt_ref)

    @pl.when(qi > 0)
    def _():
        stage_pipeline(diagonal + block_items(qi - 1, None))
        _softmax_pipeline(memory, m_ref, acc_ref)

    @pl.when(qi == 0)
    def _():
        stage_pipeline(diagonal)
        _softmax_pipeline(memory, m_ref, acc_ref)

    def any_weight_left():
        least = functools.reduce(jnp.minimum, [run_ref[h] for h in range(N_MAIN_HEADS)])
        return (jnp.min(least) < RUN_DEAD).astype(jnp.int32)

    def cond(carry):
        it, live = carry
        return (it < qi) & (live > 0)

    def body(carry):
        it, _ = carry
        stage_pipeline(block_items(qi - 1 - it, None))
        return it + 1, any_weight_left()

    lax.while_loop(cond, body, (jnp.int32(1), any_weight_left()))
    for hp in range(N_MAIN_HEADS // 2):
        pair = jnp.concatenate([acc_ref[2 * hp, 0:HEAD_DIM], acc_ref[2 * hp + 1, 0:HEAD_DIM]], axis=0)
        o_ref[:, hp * PAIR_WIDTH:(hp + 1) * PAIR_WIDTH] = pair.T.astype(o_ref.dtype)
    _softmax_finish(acc_ref, N_MAIN_HEADS, N_MEM_HEADS, o_ref, MAIN_WIDTH)


def _attention(kind, layer, q, k, cp, vt, qm, mk, mvt, batch, seq, n_mem, tq):
    t = k.shape[0]
    nq = seq // tq
    q_spec = pl.BlockSpec((1, 1, MAIN_WIDTH, tq), lambda b, i: (b, i, 0, 0))
    k_spec = pl.BlockSpec((seq, MAIN_WIDTH), lambda b, i: (b, 0))
    vt_spec = pl.BlockSpec((1,) + vt.shape[1:], lambda b, i: (b, 0, 0, 0))
    qm_spec = pl.BlockSpec((1, 1, MEM_WIDTH, tq), lambda b, i: (b, i, 0, 0))
    mk_spec = pl.BlockSpec((None, n_mem, MEM_WIDTH), lambda b, i: (layer, b, 0))
    mvt_spec = pl.BlockSpec((None, None, None, MEM_WIDTH, n_mem), lambda b, i: (layer, b, 0, 0, 0))
    o_spec = pl.BlockSpec((tq, MAIN_WIDTH + MEM_WIDTH), lambda b, i: (b * nq + i, 0))
    q_rows = 2 * PAIR_WIDTH if kind == "fox" else PAIR_WIDTH
    scratch = [pltpu.VMEM((N_HEADS, q_rows, tq), BF16),
               pltpu.VMEM((N_HEADS, SUBLANES, tq), F32),
               pltpu.VMEM((N_HEADS, ACC_ROWS, tq), F32)]
    if kind == "fox":
        cp_spec = pl.BlockSpec((seq, LANES), lambda b, i: (b, 0))
        kernel = functools.partial(_fox_attn_kernel, tq=tq)
        in_specs = [q_spec, k_spec, cp_spec, vt_spec, qm_spec, mk_spec, mvt_spec]
        args = (q, k, cp, vt, qm, mk, mvt)
    else:
        kernel = functools.partial(_sb_attn_kernel, tq=tq)
        in_specs = [q_spec, k_spec, vt_spec, qm_spec, mk_spec, mvt_spec]
        args = (q, k, vt, qm, mk, mvt)
        scratch.append(pltpu.VMEM((N_MAIN_HEADS, SUBLANES, tq), F32))
    return pl.pallas_call(
        kernel,
        grid=(batch, nq),
        in_specs=in_specs,
        out_specs=o_spec,
        out_shape=jax.ShapeDtypeStruct((t, MAIN_WIDTH + MEM_WIDTH), BF16),
        scratch_shapes=scratch,
        compiler_params=_params("arbitrary", "arbitrary"),
        name=kind + "_attn",
    )(*args)


def _causal_conv(u, prev, cw, cb):
    r8 = lax.broadcasted_iota(jnp.int32, (SUBLANES, 1), 0)
    out = cb
    for j in range(CONV_WIDTH - 1):
        shift = CONV_WIDTH - 1 - j
        rolled = pltpu.roll(u, shift, axis=0)
        top = jnp.where(r8 < shift, pltpu.roll(prev, shift, axis=0), rolled[:SUBLANES])
        shifted = jnp.concatenate([top, rolled[SUBLANES:]], axis=0)
        out = out + cw[j:j + 1, :] * shifted
    return out + cw[CONV_WIDTH - 1:CONV_WIDTH, :] * u


def _out_ffn_kernel(x_ref, o_ref, wout_ref, g_ref, wup_ref, cw_ref, cb_ref, wd_ref, fg_ref, out_ref,
                    u_ref, act_ref, carry_ref, *, tiles_per_seq, tf, final_norm):
    tm = x_ref.shape[0]
    d_ff = wd_ref.shape[0]

    @pl.when(pl.program_id(0) % tiles_per_seq == 0)
    def _():
        carry_ref[...] = jnp.zeros_like(carry_ref)

    x1 = x_ref[...] + _dot(o_ref[...], wout_ref[...])
    h = (_rms_normalize(x1) * g_ref[...]).astype(BF16)
    rows = min(MATMUL_ROWS, tm)
    for r in range(0, tm, rows):
        u_ref[r:r + rows] = _dot(h[r:r + rows], wup_ref[...])
    for c in range(0, d_ff, tf):
        halves = []
        for half in range(2):
            cols = slice(half * d_ff + c, half * d_ff + c + tf)
            u = u_ref[:, cols]
            prev = carry_ref[:, cols]
            carry_ref[:, cols] = u[tm - SUBLANES:]
            halves.append(_causal_conv(u, prev, cw_ref[:, cols], cb_ref[:, cols]))
        gate, val = halves
        act_ref[:, c:c + tf] = (gate * (1.0 / (1.0 + jnp.exp(-gate))) * val).astype(BF16)
    y = x1 + _dot(act_ref[...], wd_ref[...])
    if final_norm:
        y = _rms_normalize(y) * fg_ref[...]
    out_ref[...] = y


def _layer_spec(stacked, layer):
    return pl.BlockSpec((None,) + stacked.shape[1:], lambda *_: (layer, 0, 0), pipeline_mode=pl.Buffered(1))


def _out_ffn(x2, o, layer, w_out, g, w_up, conv_w, conv_b, w_down, final_g, seq, tm, tf, final_norm):
    t, d = x2.shape
    d_ff = w_down.shape[1]
    stacked = (w_out, g, w_up, conv_w, conv_b, w_down)
    kernel = functools.partial(_out_ffn_kernel, tiles_per_seq=seq // tm, tf=tf, final_norm=final_norm)
    return pl.pallas_call(
        kernel,
        grid=(t // tm,),
        in_specs=[_row_spec(tm, d), _row_spec(tm, o.shape[1])] + [_layer_spec(w, layer) for w in stacked]
        + [_full_spec((1, d))],
        out_specs=_row_spec(tm, d),
        out_shape=jax.ShapeDtypeStruct((t, d), F32),
        scratch_shapes=[pltpu.VMEM((tm, 2 * d_ff), F32), pltpu.VMEM((tm, d_ff), BF16),
                        pltpu.VMEM((SUBLANES, 2 * d_ff), F32)],
        compiler_params=_params("arbitrary"),
        name="out_ffn",
    )(x2, o, *stacked, final_g.reshape(1, d))


def kernel(x, mem, ln_mix_g, w_in_a, b_f_a, w_in_b, ln_kv_g, w_kv, ln_mem_g, w_memkv, w_out, ln_ffn_g,
           w_up, conv_w, conv_b, w_down, final_g):
    batch, seq, d = x.shape
    n_mem = mem.shape[1]
    depth = ln_mix_g.shape[0]
    n_a = w_in_a.shape[0]
    assert w_in_b.shape[0] == 1, "one stick-breaking layer reads the shared K/V"
    tm_proj = min(1024, seq)
    tq_attn = min(256, seq)
    tm_ffn, tf = min(512, seq), 256

    x2 = x.reshape(batch * seq, d)
    mem2 = mem.reshape(batch * n_mem, d)
    ffn_weights = (w_out.astype(BF16), ln_ffn_g.reshape(depth, 1, d), w_up.astype(BF16), conv_w,
                   conv_b.reshape(depth, 1, -1), w_down.astype(BF16))
    mk, mvt = _mem_proj(mem2, ln_mem_g, w_memkv, batch, n_mem)
    for layer in range(depth):
        if layer < n_a:
            q, k, vt, qm, cp = _proj_a(x2, ln_mix_g[layer], w_in_a[layer], b_f_a[layer], seq, tm_proj, tq_attn)
            o = _attention("fox", layer, q, k, cp, vt, qm, mk, mvt, batch, seq, n_mem, tq_attn)
        else:
            q, qm, k_sh, vt_sh = _proj_b(x2, ln_mix_g[layer], ln_kv_g, w_in_b[0], w_kv, seq, tm_proj, tq_attn)
            o = _attention("sb", layer, q, k_sh, None, vt_sh, qm, mk, mvt, batch, seq, n_mem, tq_attn)
        x2 = _out_ffn(x2, o, layer, *ffn_weights, final_g, seq, tm_ffn, tf, final_norm=(layer == depth - 1))
    return x2.reshape(batch, seq, d)
```

```python
import functools

import jax
import jax.numpy as jnp
from jax import lax
from jax.experimental import pallas as pl
from jax.experimental.pallas import tpu as pltpu

HEAD_DIM = 64
N_MAIN_HEADS = 12
N_MEM_HEADS = 4
N_HEADS = N_MAIN_HEADS + N_MEM_HEADS
MAIN_WIDTH = N_MAIN_HEADS * HEAD_DIM
MEM_WIDTH = N_MEM_HEADS * HEAD_DIM
CONV_WIDTH = 3
EPS = 1e-6
Q_SCALE = HEAD_DIM ** -0.5
LOG2_E = 1.4426950408889634

LANES = 128
MXU_WIDTH = 256
SUBLANES = 8
BF16_ROWS = 16
PAIR_WIDTH = 2 * HEAD_DIM
GATE_TERMS = 3
VMEM_LIMIT = 56 * 1024 * 1024

MATMUL_ROWS = 256
PREFIX_ROWS = 512

F32 = jnp.float32
BF16 = jnp.bfloat16


def _params(*sem):
    return pltpu.CompilerParams(dimension_semantics=sem, vmem_limit_bytes=VMEM_LIMIT)


def _dot(a, b):
    return jnp.dot(a, b, preferred_element_type=F32)


def _rms_normalize(x):
    return x * lax.rsqrt(jnp.mean(x * x, axis=-1, keepdims=True) + EPS)


def _log_sigmoid(x):
    return jnp.minimum(x, 0.0) - jnp.log1p(jnp.exp(-jnp.abs(x)))


def _split3(x):
    hi = x.astype(BF16)
    r1 = x - hi.astype(F32)
    mid = r1.astype(BF16)
    return hi, mid, (r1 - mid.astype(F32)).astype(BF16)


def _store_chunks(h, w_ref, w_lo, out_ref, width, scale, chunk=MXU_WIDTH):
    rows = min(MATMUL_ROWS, h.shape[0])
    for r0 in range(0, h.shape[0], rows):
        for c in range(0, width, chunk):
            r = _dot(h[r0:r0 + rows], w_ref[:, w_lo + c:w_lo + c + chunk])
            if scale is not None:
                r = r * scale
            out_ref[r0:r0 + rows, c:c + chunk] = r.astype(out_ref.dtype)


def _store_transposed(h, w_ref, w_lo, out_ref, width, tk, scale=None, chunk=MXU_WIDTH):
    for r in range(h.shape[0] // tk):
        for c in range(0, width, chunk):
            v = _dot(h[r * tk:(r + 1) * tk], w_ref[:, w_lo + c:w_lo + c + chunk])
            if scale is not None:
                v = v * scale
            out_ref[0, r, c:c + chunk, :] = v.T.astype(out_ref.dtype)


def _proj_a_kernel(x_ref, g_ref, wqkv_ref, wqm_ref, wf_ref, bf_ref,
                   q_ref, k_ref, vt_ref, qm_ref, cp_ref, carry_ref, *, tiles_per_seq, tk):
    tm = x_ref.shape[0]

    @pl.when(pl.program_id(0) % tiles_per_seq == 0)
    def _():
        carry_ref[...] = jnp.zeros_like(carry_ref)

    h = (_rms_normalize(x_ref[...]) * g_ref[...]).astype(BF16)
    pr = min(PREFIX_ROWS, tm)
    f_logits = [_dot(h[r:r + pr], wf_ref[...]) for r in range(0, tm, pr)]
    _store_transposed(h, wqkv_ref, 0, q_ref, MAIN_WIDTH, tk, Q_SCALE)

    row = lax.broadcasted_iota(jnp.int32, (pr, pr), 0)
    col = lax.broadcasted_iota(jnp.int32, (pr, pr), 1)
    tri = jnp.where(row >= col, 1.0, 0.0).astype(BF16)
    sums = [_dot(tri, jnp.concatenate(_split3(_log_sigmoid(f + bf_ref[...])), axis=1)) for f in f_logits]
    _store_chunks(h, wqkv_ref, MAIN_WIDTH, k_ref, MAIN_WIDTH, None)

    rr = lax.broadcasted_iota(jnp.int32, (GATE_TERMS * LANES, LANES), 0)
    cc = lax.broadcasted_iota(jnp.int32, (GATE_TERMS * LANES, LANES), 1)
    head, term = rr & (LANES - 1), rr >> (LANES.bit_length() - 1)
    place = jnp.where((cc == GATE_TERMS * head + term) & (head < N_MAIN_HEADS), 1.0, 0.0).astype(BF16)
    carry = carry_ref[0:1, :]
    cps = []
    for s in sums:
        c = (s[:, :LANES] + s[:, LANES:2 * LANES]) + s[:, 2 * LANES:] + carry
        carry = c[pr - 1:pr, :]
        cps.append(_dot(jnp.concatenate(_split3(-c), axis=1), place))
    carry_ref[0:1, :] = carry
    _store_transposed(h, wqkv_ref, 2 * MAIN_WIDTH, vt_ref, MAIN_WIDTH, tk)
    _store_transposed(h, wqm_ref, 0, qm_ref, MEM_WIDTH, tk, Q_SCALE)
    cp_ref[...] = jnp.concatenate(cps, axis=0).astype(cp_ref.dtype)


def _proj_b_kernel(x_ref, gmix_ref, gkv_ref, win_ref, wkv_ref, q_ref, qm_ref, k_ref, vt_ref, *, tk):
    xn = _rms_normalize(x_ref[...])
    h = (xn * gmix_ref[...]).astype(BF16)
    _store_transposed(h, win_ref, 0, q_ref, MAIN_WIDTH, tk, Q_SCALE)
    _store_transposed(h, win_ref, MAIN_WIDTH, qm_ref, MEM_WIDTH, tk, Q_SCALE)
    hk = (xn * gkv_ref[...]).astype(BF16)
    _store_chunks(hk, wkv_ref, 0, k_ref, MAIN_WIDTH, None)
    _store_transposed(hk, wkv_ref, MAIN_WIDTH, vt_ref, MAIN_WIDTH, tk)


def _mem_proj_kernel(x_ref, g_ref, w_ref, mk_ref, mvt_ref):
    n_mem = x_ref.shape[0]
    xn = _rms_normalize(x_ref[...])
    for layer in range(g_ref.shape[0]):
        h = (xn * g_ref[layer]).astype(BF16)
        _store_chunks(h, w_ref.at[layer], 0, mk_ref.at[layer], MEM_WIDTH, None)
        _store_transposed(h, w_ref.at[layer], MEM_WIDTH, mvt_ref.at[layer], MEM_WIDTH, n_mem)


def _row_spec(tm, width):
    return pl.BlockSpec((tm, width), lambda i: (i, 0))


def _full_spec(shape):
    return pl.BlockSpec(shape, lambda *_: (0,) * len(shape))


def _transposed_out(t, seq, tm, tk, width):
    tiles = seq // tm
    shape = (t // seq, seq // tk, width, tk)
    spec = pl.BlockSpec((1, tm // tk, width, tk), lambda i: (i // tiles, i % tiles, 0, 0))
    return spec, jax.ShapeDtypeStruct(shape, BF16)


def _proj_a(x2, g, w_in, b_f, seq, tm, tk):
    t, d = x2.shape
    n_f = N_MAIN_HEADS
    wqkv = w_in[:, :3 * MAIN_WIDTH].astype(BF16)
    wf = jnp.pad(w_in[:, 3 * MAIN_WIDTH:3 * MAIN_WIDTH + n_f], ((0, 0), (0, LANES - n_f))).astype(BF16)
    wqm = w_in[:, 3 * MAIN_WIDTH + n_f:].astype(BF16)
    bf = jnp.pad(b_f, (0, LANES - n_f)).reshape(1, LANES)
    vt_spec, vt_shape = _transposed_out(t, seq, tm, tk, MAIN_WIDTH)
    qmt_spec, qmt_shape = _transposed_out(t, seq, tm, tk, MEM_WIDTH)
    kernel = functools.partial(_proj_a_kernel, tiles_per_seq=seq // tm, tk=tk)
    return pl.pallas_call(
        kernel,
        grid=(t // tm,),
        in_specs=[_row_spec(tm, d), _full_spec((1, d)), _full_spec(wqkv.shape), _full_spec(wqm.shape),
                  _full_spec(wf.shape), _full_spec((1, LANES))],
        out_specs=[vt_spec, _row_spec(tm, MAIN_WIDTH), vt_spec, qmt_spec, _row_spec(tm, LANES)],
        out_shape=[vt_shape, jax.ShapeDtypeStruct((t, MAIN_WIDTH), BF16), vt_shape, qmt_shape,
                   jax.ShapeDtypeStruct((t, LANES), BF16)],
        scratch_shapes=[pltpu.VMEM((SUBLANES, LANES), F32)],
        compiler_params=_params("arbitrary"),
        name="proj_a",
    )(x2, g.reshape(1, d), wqkv, wqm, wf, bf)


def _proj_b(x2, g_mix, g_kv, w_in, w_kv, seq, tm, tk):
    t, d = x2.shape
    win = w_in.astype(BF16)
    wkv = w_kv.astype(BF16)
    vt_spec, vt_shape = _transposed_out(t, seq, tm, tk, MAIN_WIDTH)
    qmt_spec, qmt_shape = _transposed_out(t, seq, tm, tk, MEM_WIDTH)
    return pl.pallas_call(
        functools.partial(_proj_b_kernel, tk=tk),
        grid=(t // tm,),
        in_specs=[_row_spec(tm, d), _full_spec((1, d)), _full_spec((1, d)), _full_spec(win.shape),
                  _full_spec(wkv.shape)],
        out_specs=[vt_spec, qmt_spec, _row_spec(tm, MAIN_WIDTH), vt_spec],
        out_shape=[vt_shape, qmt_shape, jax.ShapeDtypeStruct((t, MAIN_WIDTH), BF16), vt_shape],
        compiler_params=_params("arbitrary"),
        name="proj_b",
    )(x2, g_mix.reshape(1, d), g_kv.reshape(1, d), win, wkv)


def _mem_proj(mem2, g, w, batch, n_mem):
    depth, d = g.shape
    wb = w.astype(BF16)
    return pl.pallas_call(
        _mem_proj_kernel,
        grid=(batch,),
        in_specs=[_row_spec(n_mem, d), _full_spec((depth, 1, d)), _full_spec(wb.shape)],
        out_specs=[pl.BlockSpec((depth, n_mem, MEM_WIDTH), lambda i: (0, i, 0)),
                   pl.BlockSpec((depth, 1, 1, MEM_WIDTH, n_mem), lambda i: (0, i, 0, 0, 0))],
        out_shape=[jax.ShapeDtypeStruct((depth, batch * n_mem, MEM_WIDTH), BF16),
                   jax.ShapeDtypeStruct((depth, batch, 1, MEM_WIDTH, n_mem), BF16)],
        compiler_params=_params("arbitrary"),
        name="mem_proj",
    )(mem2, g.reshape(depth, 1, d), wb)


ACC_ROWS = HEAD_DIM + BF16_ROWS
RUN_DEAD = 105.0
LOOKAHEAD = 12
SCORE_AHEAD = 3
SUFFIX_AHEAD = 3


def _split_heads_into(qt_ref, dst_ref, first, n_heads):
    row = lax.broadcasted_iota(jnp.int32, (PAIR_WIDTH, 1), 0)
    low = row < HEAD_DIM
    for hp in range(n_heads // 2):
        qt = qt_ref[0, 0, hp * PAIR_WIDTH:(hp + 1) * PAIR_WIDTH, :]
        zero = jnp.zeros_like(qt)
        for sub in range(2):
            keep = low if sub == 0 else jnp.logical_not(low)
            dst_ref[first + 2 * hp + sub, 0:PAIR_WIDTH, :] = jnp.where(keep, qt, zero)


def _write_gate_rows(dst_ref):
    tq = dst_ref.shape[2]
    row = lax.broadcasted_iota(jnp.int32, (PAIR_WIDTH, 1), 0)
    for head in range(N_MAIN_HEADS):
        pick = (row >= GATE_TERMS * head) & (row < GATE_TERMS * (head + 1))
        dst_ref[head, PAIR_WIDTH:, :] = jnp.broadcast_to(
            jnp.where(pick, 1.0, 0.0).astype(dst_ref.dtype), (PAIR_WIDTH, tq))


def _values_with_ones(vt2, sub):
    vt = vt2[sub * HEAD_DIM:(sub + 1) * HEAD_DIM]
    return jnp.concatenate([vt, jnp.ones((BF16_ROWS, vt.shape[1]), vt.dtype)], axis=0)


def _softmax_update(s, head, vt_h, m_ref, acc_ref):
    m_prev = m_ref[head][0:1]
    m_new = jnp.maximum(m_prev, jnp.max(s, axis=0, keepdims=True))
    p = jnp.exp((s - m_new).astype(BF16))
    acc_ref[head] = jnp.exp(m_prev - m_new) * acc_ref[head] + _dot(vt_h, p)
    m_ref[head] = jnp.broadcast_to(m_new, m_ref.shape[1:])


def _softmax_finish(acc_ref, first, n_heads, o_ref, col0):
    for hp in range(n_heads // 2):
        outs = []
        for sub in range(2):
            acc = acc_ref[first + 2 * hp + sub]
            outs.append(acc[:HEAD_DIM] / acc[HEAD_DIM:HEAD_DIM + 1])
        o_ref[:, col0 + hp * PAIR_WIDTH:col0 + (hp + 1) * PAIR_WIDTH] = (
            jnp.concatenate(outs, axis=0).T.astype(o_ref.dtype))


def _init_softmax_state(m_ref, acc_ref):
    m_ref[...] = jnp.full_like(m_ref, -jnp.inf)
    acc_ref[...] = jnp.zeros_like(acc_ref)


def _pair_cols(head):
    return slice((head // 2) * PAIR_WIDTH, (head // 2 + 1) * PAIR_WIDTH)


def _softmax_pipeline(items, m_ref, acc_ref, lookahead=LOOKAHEAD):
    pending = [item[0]() for item in items[:lookahead]]
    for n, (_, values_fn, head) in enumerate(items):
        s = pending.pop(0)
        if n + lookahead < len(items):
            pending.append(items[n + lookahead][0]())
        _softmax_update(s, head, values_fn(), m_ref, acc_ref)


def _memory_items(qs_ref, mk_ref, mvt_ref):
    def score(i):
        return _dot(mk_ref[:, _pair_cols(i)], qs_ref[N_MAIN_HEADS + i, 0:PAIR_WIDTH, :])

    def values(i):
        return _values_with_ones(mvt_ref[_pair_cols(i), :], i % 2)

    return [(functools.partial(score, i), functools.partial(values, i), N_MAIN_HEADS + i)
            for i in range(N_MEM_HEADS)]


def _fox_attn_kernel(q_ref, k_ref, cp_ref, vt_ref, qm_ref, mk_ref, mvt_ref, o_ref, qs_ref, m_ref, acc_ref, *, tq):
    qi = pl.program_id(1)
    tk = tq
    _write_gate_rows(qs_ref)
    _split_heads_into(q_ref, qs_ref, 0, N_MAIN_HEADS)
    _split_heads_into(qm_ref, qs_ref, N_MAIN_HEADS, N_MEM_HEADS)
    _init_softmax_state(m_ref, acc_ref)

    def block_items(kb, causal):
        ks = pl.multiple_of(kb * tk, tk)

        def score(head):
            kk = jnp.concatenate([k_ref[pl.ds(ks, tk), _pair_cols(head)], cp_ref[pl.ds(ks, tk), :]], axis=1)
            s = _dot(kk, qs_ref[head])
            return s if causal is None else jnp.where(causal, s, -jnp.inf)

        def values(head):
            return _values_with_ones(vt_ref[0, kb, _pair_cols(head), :], head % 2)

        return [(functools.partial(score, h), functools.partial(values, h), h) for h in range(N_MAIN_HEADS)]

    def two_blocks(it, carry):
        _softmax_pipeline(block_items(2 * it, None) + block_items(2 * it + 1, None), m_ref, acc_ref)
        return carry

    lax.fori_loop(0, qi >> 1, two_blocks, 0)
    key = lax.broadcasted_iota(jnp.int32, (tk, tq), 0)
    qry = lax.broadcasted_iota(jnp.int32, (tk, tq), 1)
    tail = block_items(qi, key <= qry) + _memory_items(qs_ref, mk_ref, mvt_ref)

    def finish(items):
        _softmax_pipeline(items, m_ref, acc_ref)
        _softmax_finish(acc_ref, 0, N_MAIN_HEADS, o_ref, 0)
        _softmax_finish(acc_ref, N_MAIN_HEADS, N_MEM_HEADS, o_ref, MAIN_WIDTH)

    @pl.when((qi & 1) == 1)
    def _():
        finish(block_items(qi - 1, None) + tail)

    @pl.when((qi & 1) == 0)
    def _():
        finish(tail)


def _sb_attn_kernel(q_ref, k_ref, vt_ref, qm_ref, mk_ref, mvt_ref, o_ref, qs_ref, m_ref, acc_ref, run_ref, *, tq):
    qi = pl.program_id(1)
    tk = tq
    _split_heads_into(q_ref, qs_ref, 0, N_MAIN_HEADS)
    _split_heads_into(qm_ref, qs_ref, N_MAIN_HEADS, N_MEM_HEADS)
    _init_softmax_state(m_ref, acc_ref)
    run_ref[...] = jnp.zeros_like(run_ref)

    ss = lax.broadcasted_iota(jnp.int32, (LANES + BF16_ROWS, 2 * LANES), 0)
    jj = lax.broadcasted_iota(jnp.int32, (LANES + BF16_ROWS, 2 * LANES), 1) & (LANES - 1)
    suffix = jnp.where((jj >= ss) | (ss >= LANES), 1.0, 0.0).astype(BF16)

    n_chunks = tk // LANES

    def stage_pipeline(items):
        def score_stage(kb, causal, head):
            ks = pl.multiple_of(kb * tk, tk)
            z = _dot(k_ref[pl.ds(ks, tk), _pair_cols(head)], qs_ref[head])
            logits, split = [], []
            for c in range(n_chunks):
                zc = z[c * LANES:(c + 1) * LANES]
                if causal is not None:
                    zc = jnp.where(causal[c], zc, -jnp.inf)
                nl = jnp.maximum(zc, 0.0) + jnp.log(1.0 + jnp.exp2(jnp.abs(zc) * -LOG2_E))
                hi = nl.astype(BF16)
                logits.append(zc)
                split.append(jnp.concatenate([hi, (nl - hi.astype(F32)).astype(BF16)], axis=0))
            return logits, split

        def suffix_stage(state):
            logits, split = state
            return logits, [_dot(suffix, sp) for sp in split]

        def value_stage(kb, causal, head, state):
            logits, sums = state
            run = run_ref[head][0:1]
            a_chunks = [None] * n_chunks
            for c in reversed(range(n_chunks)):
                a_chunks[c] = jnp.exp((logits[c] - (sums[c][:LANES] + run)).astype(BF16))
                run = run + sums[c][LANES:LANES + 1]
            run_ref[head] = jnp.broadcast_to(run, run_ref.shape[1:])
            vt = vt_ref[0, kb, _pair_cols(head), :][(head % 2) * HEAD_DIM:(head % 2 + 1) * HEAD_DIM]
            acc_ref[head, 0:HEAD_DIM] += _dot(vt, jnp.concatenate(a_chunks, axis=0))

        scored, summed = {}, {}
        for step in range(len(items) + SCORE_AHEAD + SUFFIX_AHEAD):
            if step < len(items):
                scored[step] = score_stage(*items[step])
            i2 = step - SCORE_AHEAD
            if 0 <= i2 < len(items):
                summed[i2] = suffix_stage(scored.pop(i2))
            i3 = i2 - SUFFIX_AHEAD
            if 0 <= i3 < len(items):
                value_stage(*items[i3], summed.pop(i3))

    def block_items(kb, causal):
        return [(kb, causal, head) for head in range(N_MAIN_HEADS)]

    key = lax.broadcasted_iota(jnp.int32, (LANES, tq), 0)
    qry = lax.broadcasted_iota(jnp.int32, (LANES, tq), 1)
    diagonal = block_items(qi, [(key + c * LANES) < qry for c in range(n_chunks)])
    memory = _memory_items(qs_ref, mk_ref, mvt_ref)

    @pl.when(qi > 0)
    def _():
        stage_pipeline(diagonal + block_items(qi - 1, None))
        _softmax_pipeline(memory, m_ref, acc_ref)

    @pl.when(qi == 0)
    def _():
        stage_pipeline(diagonal)
        _softmax_pipeline(memory, m_ref, acc_ref)

    def any_weight_left():
        least = functools.reduce(jnp.minimum, [run_ref[h] for h in range(N_MAIN_HEADS)])
        return (jnp.min(least) < RUN_DEAD).astype(jnp.int32)

    def cond(carry):
        it, live = carry
        return (it < qi) & (live > 0)

    def body(carry):
        it, _ = carry
        stage_pipeline(block_items(qi - 1 - it, None))
        return it + 1, any_weight_left()

    lax.while_loop(cond, body, (jnp.int32(1), any_weight_left()))
    for hp in range(N_MAIN_HEADS // 2):
        pair = jnp.concatenate([acc_ref[2 * hp, 0:HEAD_DIM], acc_ref[2 * hp + 1, 0:HEAD_DIM]], axis=0)
        o_ref[:, hp * PAIR_WIDTH:(hp + 1) * PAIR_WIDTH] = pair.T.astype(o_ref.dtype)
    _softmax_finish(acc_ref, N_MAIN_HEADS, N_MEM_HEADS, o_ref, MAIN_WIDTH)


def _attention(kind, layer, q, k, cp, vt, qm, mk, mvt, batch, seq, n_mem, tq):
    t = k.shape[0]
    nq = seq // tq
    q_spec = pl.BlockSpec((1, 1, MAIN_WIDTH, tq), lambda b, i: (b, i, 0, 0))
    k_spec = pl.BlockSpec((seq, MAIN_WIDTH), lambda b, i: (b, 0))
    vt_spec = pl.BlockSpec((1,) + vt.shape[1:], lambda b, i: (b, 0, 0, 0))
    qm_spec = pl.BlockSpec((1, 1, MEM_WIDTH, tq), lambda b, i: (b, i, 0, 0))
    mk_spec = pl.BlockSpec((None, n_mem, MEM_WIDTH), lambda b, i: (layer, b, 0))
    mvt_spec = pl.BlockSpec((None, None, None, MEM_WIDTH, n_mem), lambda b, i: (layer, b, 0, 0, 0))
    o_spec = pl.BlockSpec((tq, MAIN_WIDTH + MEM_WIDTH), lambda b, i: (b * nq + i, 0))
    q_rows = 2 * PAIR_WIDTH if kind == "fox" else PAIR_WIDTH
    scratch = [pltpu.VMEM((N_HEADS, q_rows, tq), BF16),
               pltpu.VMEM((N_HEADS, SUBLANES, tq), F32),
               pltpu.VMEM((N_HEADS, ACC_ROWS, tq), F32)]
    if kind == "fox":
        cp_spec = pl.BlockSpec((seq, LANES), lambda b, i: (b, 0))
        kernel = functools.partial(_fox_attn_kernel, tq=tq)
        in_specs = [q_spec, k_spec, cp_spec, vt_spec, qm_spec, mk_spec, mvt_spec]
        args = (q, k, cp, vt, qm, mk, mvt)
    else:
        kernel = functools.partial(_sb_attn_kernel, tq=tq)
        in_specs = [q_spec, k_spec, vt_spec, qm_spec, mk_spec, mvt_spec]
        args = (q, k, vt, qm, mk, mvt)
        scratch.append(pltpu.VMEM((N_MAIN_HEADS, SUBLANES, tq), F32))
    return pl.pallas_call(
        kernel,
        grid=(batch, nq),
        in_specs=in_specs,
        out_specs=o_spec,
        out_shape=jax.ShapeDtypeStruct((t, MAIN_WIDTH + MEM_WIDTH), BF16),
        scratch_shapes=scratch,
        compiler_params=_params("arbitrary", "arbitrary"),
        name=kind + "_attn",
    )(*args)


def _causal_conv(u, prev, cw, cb):
    r8 = lax.broadcasted_iota(jnp.int32, (SUBLANES, 1), 0)
    out = cb
    for j in range(CONV_WIDTH - 1):
        shift = CONV_WIDTH - 1 - j
        rolled = pltpu.roll(u, shift, axis=0)
        top = jnp.where(r8 < shift, pltpu.roll(prev, shift, axis=0), rolled[:SUBLANES])
        shifted = jnp.concatenate([top, rolled[SUBLANES:]], axis=0)
        out = out + cw[j:j + 1, :] * shifted
    return out + cw[CONV_WIDTH - 1:CONV_WIDTH, :] * u


def _out_ffn_kernel(x_ref, o_ref, wout_ref, g_ref, wup_ref, cw_ref, cb_ref, wd_ref, fg_ref, out_ref,
                    u_ref, act_ref, carry_ref, *, tiles_per_seq, tf, final_norm):
    tm = x_ref.shape[0]
    d_ff = wd_ref.shape[0]

    @pl.when(pl.program_id(0) % tiles_per_seq == 0)
    def _():
        carry_ref[...] = jnp.zeros_like(carry_ref)

    x1 = x_ref[...] + _dot(o_ref[...], wout_ref[...])
    h = (_rms_normalize(x1) * g_ref[...]).astype(BF16)
    rows = min(MATMUL_ROWS, tm)
    for r in range(0, tm, rows):
        u_ref[r:r + rows] = _dot(h[r:r + rows], wup_ref[...])
    for c in range(0, d_ff, tf):
        halves = []
        for half in range(2):
            cols = slice(half * d_ff + c, half * d_ff + c + tf)
            u = u_ref[:, cols]
            prev = carry_ref[:, cols]
            carry_ref[:, cols] = u[tm - SUBLANES:]
            halves.append(_causal_conv(u, prev, cw_ref[:, cols], cb_ref[:, cols]))
        gate, val = halves
        act_ref[:, c:c + tf] = (gate * (1.0 / (1.0 + jnp.exp(-gate))) * val).astype(BF16)
    y = x1 + _dot(act_ref[...], wd_ref[...])
    if final_norm:
        y = _rms_normalize(y) * fg_ref[...]
    out_ref[...] = y


def _layer_spec(stacked, layer):
    return pl.BlockSpec((None,) + stacked.shape[1:], lambda *_: (layer, 0, 0), pipeline_mode=pl.Buffered(1))


def _out_ffn(x2, o, layer, w_out, g, w_up, conv_w, conv_b, w_down, final_g, seq, tm, tf, final_norm):
    t, d = x2.shape
    d_ff = w_down.shape[1]
    stacked = (w_out, g, w_up, conv_w, conv_b, w_down)
    kernel = functools.partial(_out_ffn_kernel, tiles_per_seq=seq // tm, tf=tf, final_norm=final_norm)
    return pl.pallas_call(
        kernel,
        grid=(t // tm,),
        in_specs=[_row_spec(tm, d), _row_spec(tm, o.shape[1])] + [_layer_spec(w, layer) for w in stacked]
        + [_full_spec((1, d))],
        out_specs=_row_spec(tm, d),
        out_shape=jax.ShapeDtypeStruct((t, d), F32),
        scratch_shapes=[pltpu.VMEM((tm, 2 * d_ff), F32), pltpu.VMEM((tm, d_ff), BF16),
                        pltpu.VMEM((SUBLANES, 2 * d_ff), F32)],
        compiler_params=_params("arbitrary"),
        name="out_ffn",
    )(x2, o, *stacked, final_g.reshape(1, d))


def kernel(x, mem, ln_mix_g, w_in_a, b_f_a, w_in_b, ln_kv_g, w_kv, ln_mem_g, w_memkv, w_out, ln_ffn_g,
           w_up, conv_w, conv_b, w_down, final_g):
    batch, seq, d = x.shape
    n_mem = mem.shape[1]
    depth = ln_mix_g.shape[0]
    n_a = w_in_a.shape[0]
    assert w_in_b.shape[0] == 1, "one stick-breaking layer reads the shared K/V"
    tm_proj = min(1024, seq)
    tq_attn = min(MXU_WIDTH, seq)
    tm_ffn, tf = min(512, seq), MXU_WIDTH

    x2 = x.reshape(batch * seq, d)
    mem2 = mem.reshape(batch * n_mem, d)
    ffn_weights = (w_out.astype(BF16), ln_ffn_g.reshape(depth, 1, d), w_up.astype(BF16), conv_w,
                   conv_b.reshape(depth, 1, -1), w_down.astype(BF16))
    mk, mvt = _mem_proj(mem2, ln_mem_g, w_memkv, batch, n_mem)
    for layer in range(depth):
        if layer < n_a:
            q, k, vt, qm, cp = _proj_a(x2, ln_mix_g[layer], w_in_a[layer], b_f_a[layer], seq, tm_proj, tq_attn)
            o = _attention("fox", layer, q, k, cp, vt, qm, mk, mvt, batch, seq, n_mem, tq_attn)
        else:
            q, qm, k_sh, vt_sh = _proj_b(x2, ln_mix_g[layer], ln_kv_g, w_in_b[0], w_kv, seq, tm_proj, tq_attn)
            o = _attention("sb", layer, q, k_sh, None, vt_sh, qm, mk, mvt, batch, seq, n_mem, tq_attn)
        x2 = _out_ffn(x2, o, layer, *ffn_weights, final_g, seq, tm_ffn, tf, final_norm=(layer == depth - 1))
    return x2.reshape(batch, seq, d)
```

```python
import functools

import jax
import jax.numpy as jnp
from jax import lax
from jax.experimental import pallas as pl
from jax.experimental.pallas import tpu as pltpu

HEAD_DIM = 64
N_MAIN_HEADS = 12
N_MEM_HEADS = 4
N_HEADS = N_MAIN_HEADS + N_MEM_HEADS
MAIN_WIDTH = N_MAIN_HEADS * HEAD_DIM
MEM_WIDTH = N_MEM_HEADS * HEAD_DIM
CONV_WIDTH = 3
EPS = 1e-6
Q_SCALE = HEAD_DIM ** -0.5
LOG2_E = 1.4426950408889634

LANES = 128
MXU_WIDTH = 256
SUBLANES = 8
BF16_ROWS = 16
PAIR_WIDTH = 2 * HEAD_DIM
GATE_TERMS = 3
VMEM_LIMIT = 56 * 1024 * 1024

MATMUL_ROWS = 256
PREFIX_ROWS = 512

F32 = jnp.float32
BF16 = jnp.bfloat16


def _params(*sem):
    return pltpu.CompilerParams(dimension_semantics=sem, vmem_limit_bytes=VMEM_LIMIT)


def _dot(a, b):
    return jnp.dot(a, b, preferred_element_type=F32)


def _rms_normalize(x):
    return x * lax.rsqrt(jnp.mean(x * x, axis=-1, keepdims=True) + EPS)


def _log_sigmoid(x):
    return jnp.minimum(x, 0.0) - jnp.log1p(jnp.exp(-jnp.abs(x)))


def _split3(x):
    hi = x.astype(BF16)
    r1 = x - hi.astype(F32)
    mid = r1.astype(BF16)
    return hi, mid, (r1 - mid.astype(F32)).astype(BF16)


def _store_chunks(h, w_ref, w_lo, out_ref, width, scale, chunk=MXU_WIDTH):
    rows = min(MATMUL_ROWS, h.shape[0])
    for r0 in range(0, h.shape[0], rows):
        for c in range(0, width, chunk):
            r = _dot(h[r0:r0 + rows], w_ref[:, w_lo + c:w_lo + c + chunk])
            if scale is not None:
                r = r * scale
            out_ref[r0:r0 + rows, c:c + chunk] = r.astype(out_ref.dtype)


def _store_transposed(h, w_ref, w_lo, out_ref, width, tk, scale=None, chunk=MXU_WIDTH):
    for r in range(h.shape[0] // tk):
        for c in range(0, width, chunk):
            v = _dot(h[r * tk:(r + 1) * tk], w_ref[:, w_lo + c:w_lo + c + chunk])
            if scale is not None:
                v = v * scale
            out_ref[0, r, c:c + chunk, :] = v.T.astype(out_ref.dtype)


def _proj_a_kernel(x_ref, g_ref, wqkv_ref, wqm_ref, wf_ref, bf_ref,
                   q_ref, k_ref, vt_ref, qm_ref, cp_ref, carry_ref, *, tiles_per_seq, tk):
    tm = x_ref.shape[0]

    @pl.when(pl.program_id(0) % tiles_per_seq == 0)
    def _():
        carry_ref[...] = jnp.zeros_like(carry_ref)

    h = (_rms_normalize(x_ref[...]) * g_ref[...]).astype(BF16)
    pr = min(PREFIX_ROWS, tm)
    f_logits = [_dot(h[r:r + pr], wf_ref[...]) for r in range(0, tm, pr)]
    _store_transposed(h, wqkv_ref, 0, q_ref, MAIN_WIDTH, tk, Q_SCALE)

    row = lax.broadcasted_iota(jnp.int32, (pr, pr), 0)
    col = lax.broadcasted_iota(jnp.int32, (pr, pr), 1)
    tri = jnp.where(row >= col, 1.0, 0.0).astype(BF16)
    sums = [_dot(tri, jnp.concatenate(_split3(_log_sigmoid(f + bf_ref[...])), axis=1)) for f in f_logits]
    _store_chunks(h, wqkv_ref, MAIN_WIDTH, k_ref, MAIN_WIDTH, None)

    rr = lax.broadcasted_iota(jnp.int32, (GATE_TERMS * LANES, LANES), 0)
    cc = lax.broadcasted_iota(jnp.int32, (GATE_TERMS * LANES, LANES), 1)
    head, term = rr & (LANES - 1), rr >> (LANES.bit_length() - 1)
    place = jnp.where((cc == GATE_TERMS * head + term) & (head < N_MAIN_HEADS), 1.0, 0.0).astype(BF16)
    carry = carry_ref[0:1, :]
    cps = []
    for s in sums:
        c = (s[:, :LANES] + s[:, LANES:2 * LANES]) + s[:, 2 * LANES:] + carry
        carry = c[pr - 1:pr, :]
        cps.append(_dot(jnp.concatenate(_split3(-c), axis=1), place))
    carry_ref[0:1, :] = carry
    _store_transposed(h, wqkv_ref, 2 * MAIN_WIDTH, vt_ref, MAIN_WIDTH, tk)
    _store_transposed(h, wqm_ref, 0, qm_ref, MEM_WIDTH, tk, Q_SCALE)
    cp_ref[...] = jnp.concatenate(cps, axis=0).astype(cp_ref.dtype)


def _proj_b_kernel(x_ref, gmix_ref, gkv_ref, win_ref, wkv_ref, q_ref, qm_ref, k_ref, vt_ref, *, tk):
    xn = _rms_normalize(x_ref[...])
    h = (xn * gmix_ref[...]).astype(BF16)
    _store_transposed(h, win_ref, 0, q_ref, MAIN_WIDTH, tk, Q_SCALE)
    _store_transposed(h, win_ref, MAIN_WIDTH, qm_ref, MEM_WIDTH, tk, Q_SCALE)
    hk = (xn * gkv_ref[...]).astype(BF16)
    _store_chunks(hk, wkv_ref, 0, k_ref, MAIN_WIDTH, None)
    _store_transposed(hk, wkv_ref, MAIN_WIDTH, vt_ref, MAIN_WIDTH, tk)


def _mem_proj_kernel(x_ref, g_ref, w_ref, mk_ref, mvt_ref):
    n_mem = x_ref.shape[0]
    xn = _rms_normalize(x_ref[...])
    for layer in range(g_ref.shape[0]):
        h = (xn * g_ref[layer]).astype(BF16)
        _store_chunks(h, w_ref.at[layer], 0, mk_ref.at[layer], MEM_WIDTH, None)
        _store_transposed(h, w_ref.at[layer], MEM_WIDTH, mvt_ref.at[layer], MEM_WIDTH, n_mem)


def _row_spec(tm, width):
    return pl.BlockSpec((tm, width), lambda i: (i, 0))


def _full_spec(shape):
    return pl.BlockSpec(shape, lambda *_: (0,) * len(shape))


def _transposed_out(t, seq, tm, tk, width):
    tiles = seq // tm
    shape = (t // seq, seq // tk, width, tk)
    spec = pl.BlockSpec((1, tm // tk, width, tk), lambda i: (i // tiles, i % tiles, 0, 0))
    return spec, jax.ShapeDtypeStruct(shape, BF16)


def _proj_a(x2, g, w_in, b_f, seq, tm, tk):
    t, d = x2.shape
    n_f = N_MAIN_HEADS
    wqkv = w_in[:, :3 * MAIN_WIDTH].astype(BF16)
    wf = jnp.pad(w_in[:, 3 * MAIN_WIDTH:3 * MAIN_WIDTH + n_f], ((0, 0), (0, LANES - n_f))).astype(BF16)
    wqm = w_in[:, 3 * MAIN_WIDTH + n_f:].astype(BF16)
    bf = jnp.pad(b_f, (0, LANES - n_f)).reshape(1, LANES)
    vt_spec, vt_shape = _transposed_out(t, seq, tm, tk, MAIN_WIDTH)
    qmt_spec, qmt_shape = _transposed_out(t, seq, tm, tk, MEM_WIDTH)
    kernel = functools.partial(_proj_a_kernel, tiles_per_seq=seq // tm, tk=tk)
    return pl.pallas_call(
        kernel,
        grid=(t // tm,),
        in_specs=[_row_spec(tm, d), _full_spec((1, d)), _full_spec(wqkv.shape), _full_spec(wqm.shape),
                  _full_spec(wf.shape), _full_spec((1, LANES))],
        out_specs=[vt_spec, _row_spec(tm, MAIN_WIDTH), vt_spec, qmt_spec, _row_spec(tm, LANES)],
        out_shape=[vt_shape, jax.ShapeDtypeStruct((t, MAIN_WIDTH), BF16), vt_shape, qmt_shape,
                   jax.ShapeDtypeStruct((t, LANES), BF16)],
        scratch_shapes=[pltpu.VMEM((SUBLANES, LANES), F32)],
        compiler_params=_params("arbitrary"),
        name="proj_a",
    )(x2, g.reshape(1, d), wqkv, wqm, wf, bf)


def _proj_b(x2, g_mix, g_kv, w_in, w_kv, seq, tm, tk):
    t, d = x2.shape
    win = w_in.astype(BF16)
    wkv = w_kv.astype(BF16)
    vt_spec, vt_shape = _transposed_out(t, seq, tm, tk, MAIN_WIDTH)
    qmt_spec, qmt_shape = _transposed_out(t, seq, tm, tk, MEM_WIDTH)
    return pl.pallas_call(
        functools.partial(_proj_b_kernel, tk=tk),
        grid=(t // tm,),
        in_specs=[_row_spec(tm, d), _full_spec((1, d)), _full_spec((1, d)), _full_spec(win.shape),
                  _full_spec(wkv.shape)],
        out_specs=[vt_spec, qmt_spec, _row_spec(tm, MAIN_WIDTH), vt_spec],
        out_shape=[vt_shape, qmt_shape, jax.ShapeDtypeStruct((t, MAIN_WIDTH), BF16), vt_shape],
        compiler_params=_params("arbitrary"),
        name="proj_b",
    )(x2, g_mix.reshape(1, d), g_kv.reshape(1, d), win, wkv)


def _mem_proj(mem2, g, w, batch, n_mem):
    depth, d = g.shape
    wb = w.astype(BF16)
    return pl.pallas_call(
        _mem_proj_kernel,
        grid=(batch,),
        in_specs=[_row_spec(n_mem, d), _full_spec((depth, 1, d)), _full_spec(wb.shape)],
        out_specs=[pl.BlockSpec((depth, n_mem, MEM_WIDTH), lambda i: (0, i, 0)),
                   pl.BlockSpec((depth, 1, 1, MEM_WIDTH, n_mem), lambda i: (0, i, 0, 0, 0))],
        out_shape=[jax.ShapeDtypeStruct((depth, batch * n_mem, MEM_WIDTH), BF16),
                   jax.ShapeDtypeStruct((depth, batch, 1, MEM_WIDTH, n_mem), BF16)],
        compiler_params=_params("arbitrary"),
        name="mem_proj",
    )(mem2, g.reshape(depth, 1, d), wb)


ACC_ROWS = HEAD_DIM + BF16_ROWS
RUN_DEAD = 105.0
LOOKAHEAD = 12
SCORE_AHEAD = 3
SUFFIX_AHEAD = 3


def _split_heads_into(qt_ref, dst_ref, first, n_heads):
    row = lax.broadcasted_iota(jnp.int32, (PAIR_WIDTH, 1), 0)
    low = row < HEAD_DIM
    for hp in range(n_heads // 2):
        qt = qt_ref[0, 0, hp * PAIR_WIDTH:(hp + 1) * PAIR_WIDTH, :]
        zero = jnp.zeros_like(qt)
        for sub in range(2):
            keep = low if sub == 0 else jnp.logical_not(low)
            dst_ref[first + 2 * hp + sub, 0:PAIR_WIDTH, :] = jnp.where(keep, qt, zero)


def _write_gate_rows(dst_ref):
    tq = dst_ref.shape[2]
    row = lax.broadcasted_iota(jnp.int32, (PAIR_WIDTH, 1), 0)
    for head in range(N_MAIN_HEADS):
        pick = (row >= GATE_TERMS * head) & (row < GATE_TERMS * (head + 1))
        dst_ref[head, PAIR_WIDTH:, :] = jnp.broadcast_to(
            jnp.where(pick, 1.0, 0.0).astype(dst_ref.dtype), (PAIR_WIDTH, tq))


def _values_with_ones(vt2, sub):
    vt = vt2[sub * HEAD_DIM:(sub + 1) * HEAD_DIM]
    return jnp.concatenate([vt, jnp.ones((BF16_ROWS, vt.shape[1]), vt.dtype)], axis=0)


def _softmax_update(s, head, vt_h, m_ref, acc_ref):
    m_prev = m_ref[head][0:1]
    m_new = jnp.maximum(m_prev, jnp.max(s, axis=0, keepdims=True))
    p = jnp.exp((s - m_new).astype(BF16))
    acc_ref[head] = jnp.exp(m_prev - m_new) * acc_ref[head] + _dot(vt_h, p)
    m_ref[head] = jnp.broadcast_to(m_new, m_ref.shape[1:])


def _softmax_finish(acc_ref, first, n_heads, o_ref, col0):
    for hp in range(n_heads // 2):
        outs = []
        for sub in range(2):
            acc = acc_ref[first + 2 * hp + sub]
            outs.append(acc[:HEAD_DIM] / acc[HEAD_DIM:HEAD_DIM + 1])
        o_ref[:, col0 + hp * PAIR_WIDTH:col0 + (hp + 1) * PAIR_WIDTH] = (
            jnp.concatenate(outs, axis=0).T.astype(o_ref.dtype))


def _init_softmax_state(m_ref, acc_ref):
    m_ref[...] = jnp.full_like(m_ref, -jnp.inf)
    acc_ref[...] = jnp.zeros_like(acc_ref)


def _pair_cols(head):
    return slice((head // 2) * PAIR_WIDTH, (head // 2 + 1) * PAIR_WIDTH)


def _softmax_pipeline(items, m_ref, acc_ref, lookahead=LOOKAHEAD):
    pending = [item[0]() for item in items[:lookahead]]
    for n, (_, values_fn, head) in enumerate(items):
        s = pending.pop(0)
        if n + lookahead < len(items):
            pending.append(items[n + lookahead][0]())
        _softmax_update(s, head, values_fn(), m_ref, acc_ref)


def _memory_items(qs_ref, mk_ref, mvt_ref):
    def score(i):
        return _dot(mk_ref[:, _pair_cols(i)], qs_ref[N_MAIN_HEADS + i, 0:PAIR_WIDTH, :])

    def values(i):
        return _values_with_ones(mvt_ref[_pair_cols(i), :], i % 2)

    return [(functools.partial(score, i), functools.partial(values, i), N_MAIN_HEADS + i)
            for i in range(N_MEM_HEADS)]


def _fox_attn_kernel(q_ref, k_ref, cp_ref, vt_ref, qm_ref, mk_ref, mvt_ref, o_ref, qs_ref, m_ref, acc_ref, *, tq):
    qi = pl.program_id(1)
    tk = tq
    _write_gate_rows(qs_ref)
    _split_heads_into(q_ref, qs_ref, 0, N_MAIN_HEADS)
    _split_heads_into(qm_ref, qs_ref, N_MAIN_HEADS, N_MEM_HEADS)
    _init_softmax_state(m_ref, acc_ref)

    def block_items(kb, causal):
        ks = pl.multiple_of(kb * tk, tk)

        def score(head):
            kk = jnp.concatenate([k_ref[pl.ds(ks, tk), _pair_cols(head)], cp_ref[pl.ds(ks, tk), :]], axis=1)
            s = _dot(kk, qs_ref[head])
            return s if causal is None else jnp.where(causal, s, -jnp.inf)

        def values(head):
            return _values_with_ones(vt_ref[0, kb, _pair_cols(head), :], head % 2)

        return [(functools.partial(score, h), functools.partial(values, h), h) for h in range(N_MAIN_HEADS)]

    def two_blocks(it, carry):
        _softmax_pipeline(block_items(2 * it, None) + block_items(2 * it + 1, None), m_ref, acc_ref)
        return carry

    lax.fori_loop(0, qi >> 1, two_blocks, 0)
    key = lax.broadcasted_iota(jnp.int32, (tk, tq), 0)
    qry = lax.broadcasted_iota(jnp.int32, (tk, tq), 1)
    tail = block_items(qi, key <= qry) + _memory_items(qs_ref, mk_ref, mvt_ref)

    def finish(items):
        _softmax_pipeline(items, m_ref, acc_ref)
        _softmax_finish(acc_ref, 0, N_MAIN_HEADS, o_ref, 0)
        _softmax_finish(acc_ref, N_MAIN_HEADS, N_MEM_HEADS, o_ref, MAIN_WIDTH)

    @pl.when((qi & 1) == 1)
    def _():
        finish(block_items(qi - 1, None) + tail)

    @pl.when((qi & 1) == 0)
    def _():
        finish(tail)


def _sb_attn_kernel(q_ref, k_ref, vt_ref, qm_ref, mk_ref, mvt_ref, o_ref, qs_ref, m_ref, acc_ref, run_ref, *, tq):
    qi = pl.program_id(1)
    tk = tq
    _split_heads_into(q_ref, qs_ref, 0, N_MAIN_HEADS)
    _split_heads_into(qm_ref, qs_ref, N_MAIN_HEADS, N_MEM_HEADS)
    _init_softmax_state(m_ref, acc_ref)
    run_ref[...] = jnp.zeros_like(run_ref)

    ss = lax.broadcasted_iota(jnp.int32, (LANES + BF16_ROWS, 2 * LANES), 0)
    jj = lax.broadcasted_iota(jnp.int32, (LANES + BF16_ROWS, 2 * LANES), 1) & (LANES - 1)
    suffix = jnp.where((jj >= ss) | (ss >= LANES), 1.0, 0.0).astype(BF16)

    n_chunks = tk // LANES

    def stage_pipeline(items):
        def score_stage(kb, causal, head):
            ks = pl.multiple_of(kb * tk, tk)
            z = _dot(k_ref[pl.ds(ks, tk), _pair_cols(head)], qs_ref[head])
            logits, split = [], []
            for c in range(n_chunks):
                zc = z[c * LANES:(c + 1) * LANES]
                if causal is not None:
                    zc = jnp.where(causal[c], zc, -jnp.inf)
                nl = jnp.maximum(zc, 0.0) + jnp.log(1.0 + jnp.exp2(jnp.abs(zc) * -LOG2_E))
                hi = nl.astype(BF16)
                logits.append(zc)
                split.append(jnp.concatenate([hi, (nl - hi.astype(F32)).astype(BF16)], axis=0))
            return logits, split

        def suffix_stage(state):
            logits, split = state
            return logits, [_dot(suffix, sp) for sp in split]

        def value_stage(kb, causal, head, state):
            logits, sums = state
            run = run_ref[head][0:1]
            a_chunks = [None] * n_chunks
            for c in reversed(range(n_chunks)):
                a_chunks[c] = jnp.exp((logits[c] - (sums[c][:LANES] + run)).astype(BF16))
                run = run + sums[c][LANES:LANES + 1]
            run_ref[head] = jnp.broadcast_to(run, run_ref.shape[1:])
            vt = vt_ref[0, kb, _pair_cols(head), :][(head % 2) * HEAD_DIM:(head % 2 + 1) * HEAD_DIM]
            acc_ref[head, 0:HEAD_DIM] += _dot(vt, jnp.concatenate(a_chunks, axis=0))

        scored, summed = {}, {}
        for step in range(len(items) + SCORE_AHEAD + SUFFIX_AHEAD):
            if step < len(items):
                scored[step] = score_stage(*items[step])
            i2 = step - SCORE_AHEAD
            if 0 <= i2 < len(items):
                summed[i2] = suffix_stage(scored.pop(i2))
            i3 = i2 - SUFFIX_AHEAD
            if 0 <= i3 < len(items):
                value_stage(*items[i3], summed.pop(i3))

    def block_items(kb, causal):
        return [(kb, causal, head) for head in range(N_MAIN_HEADS)]

    key = lax.broadcasted_iota(jnp.int32, (LANES, tq), 0)
    qry = lax.broadcasted_iota(jnp.int32, (LANES, tq), 1)
    diagonal = block_items(qi, [(key + c * LANES) < qry for c in range(n_chunks)])
    memory = _memory_items(qs_ref, mk_ref, mvt_ref)

    @pl.when(qi > 0)
    def _():
        stage_pipeline(diagonal + block_items(qi - 1, None))
        _softmax_pipeline(memory, m_ref, acc_ref)

    @pl.when(qi == 0)
    def _():
        stage_pipeline(diagonal)
        _softmax_pipeline(memory, m_ref, acc_ref)

    def any_weight_left():
        least = functools.reduce(jnp.minimum, [run_ref[h] for h in range(N_MAIN_HEADS)])
        return (jnp.min(least) < RUN_DEAD).astype(jnp.int32)

    def cond(carry):
        it, live = carry
        return (it < qi) & (live > 0)

    def body(carry):
        it, _ = carry
        stage_pipeline(block_items(qi - 1 - it, None))
        return it + 1, any_weight_left()

    lax.while_loop(cond, body, (jnp.int32(1), any_weight_left()))
    for hp in range(N_MAIN_HEADS // 2):
        pair = jnp.concatenate([acc_ref[2 * hp, 0:HEAD_DIM], acc_ref[2 * hp + 1, 0:HEAD_DIM]], axis=0)
        o_ref[:, hp * PAIR_WIDTH:(hp + 1) * PAIR_WIDTH] = pair.T.astype(o_ref.dtype)
    _softmax_finish(acc_ref, N_MAIN_HEADS, N_MEM_HEADS, o_ref, MAIN_WIDTH)


def _attention(kind, layer, q, k, cp, vt, qm, mk, mvt, batch, seq, n_mem, tq):
    t = k.shape[0]
    nq = seq // tq
    q_spec = pl.BlockSpec((1, 1, MAIN_WIDTH, tq), lambda b, i: (b, i, 0, 0))
    k_spec = pl.BlockSpec((seq, MAIN_WIDTH), lambda b, i: (b, 0))
    vt_spec = pl.BlockSpec((1,) + vt.shape[1:], lambda b, i: (b, 0, 0, 0))
    qm_spec = pl.BlockSpec((1, 1, MEM_WIDTH, tq), lambda b, i: (b, i, 0, 0))
    mk_spec = pl.BlockSpec((None, n_mem, MEM_WIDTH), lambda b, i: (layer, b, 0))
    mvt_spec = pl.BlockSpec((None, None, None, MEM_WIDTH, n_mem), lambda b, i: (layer, b, 0, 0, 0))
    o_spec = pl.BlockSpec((tq, MAIN_WIDTH + MEM_WIDTH), lambda b, i: (b * nq + i, 0))
    q_rows = 2 * PAIR_WIDTH if kind == "fox" else PAIR_WIDTH
    scratch = [pltpu.VMEM((N_HEADS, q_rows, tq), BF16),
               pltpu.VMEM((N_HEADS, SUBLANES, tq), F32),
               pltpu.VMEM((N_HEADS, ACC_ROWS, tq), F32)]
    if kind == "fox":
        cp_spec = pl.BlockSpec((seq, LANES), lambda b, i: (b, 0))
        kernel = functools.partial(_fox_attn_kernel, tq=tq)
        in_specs = [q_spec, k_spec, cp_spec, vt_spec, qm_spec, mk_spec, mvt_spec]
        args = (q, k, cp, vt, qm, mk, mvt)
    else:
        kernel = functools.partial(_sb_attn_kernel, tq=tq)
        in_specs = [q_spec, k_spec, vt_spec, qm_spec, mk_spec, mvt_spec]
        args = (q, k, vt, qm, mk, mvt)
        scratch.append(pltpu.VMEM((N_MAIN_HEADS, SUBLANES, tq), F32))
    return pl.pallas_call(
        kernel,
        grid=(batch, nq),
        in_specs=in_specs,
        out_specs=o_spec,
        out_shape=jax.ShapeDtypeStruct((t, MAIN_WIDTH + MEM_WIDTH), BF16),
        scratch_shapes=scratch,
        compiler_params=_params("arbitrary", "arbitrary"),
        name=kind + "_attn",
    )(*args)


def _causal_conv(u, prev, cw, cb):
    r8 = lax.broadcasted_iota(jnp.int32, (SUBLANES, 1), 0)
    out = cb
    for j in range(CONV_WIDTH - 1):
        shift = CONV_WIDTH - 1 - j
        rolled = pltpu.roll(u, shift, axis=0)
        top = jnp.where(r8 < shift, pltpu.roll(prev, shift, axis=0), rolled[:SUBLANES])
        shifted = jnp.concatenate([top, rolled[SUBLANES:]], axis=0)
        out = out + cw[j:j + 1, :] * shifted
    return out + cw[CONV_WIDTH - 1:CONV_WIDTH, :] * u


def _out_ffn_kernel(x_ref, o_ref, wout_ref, g_ref, wup_ref, cw_ref, cb_ref, wd_ref, fg_ref, out_ref,
                    u_ref, act_ref, carry_ref, *, tiles_per_seq, tf, final_norm):
    tm = x_ref.shape[0]
    d_ff = wd_ref.shape[0]

    @pl.when(pl.program_id(0) % tiles_per_seq == 0)
    def _():
        carry_ref[...] = jnp.zeros_like(carry_ref)

    rows = min(MATMUL_ROWS, tm)
    x1_parts = [x_ref[r:r + rows] + _dot(o_ref[r:r + rows], wout_ref[...]) for r in range(0, tm, rows)]
    for i, x1r in enumerate(x1_parts):
        h = (_rms_normalize(x1r) * g_ref[...]).astype(BF16)
        u_ref[i * rows:(i + 1) * rows] = _dot(h, wup_ref[...])
    x1 = jnp.concatenate(x1_parts, axis=0)
    for c in range(0, d_ff, tf):
        halves = []
        for half in range(2):
            cols = slice(half * d_ff + c, half * d_ff + c + tf)
            u = u_ref[:, cols]
            prev = carry_ref[:, cols]
            carry_ref[:, cols] = u[tm - SUBLANES:]
            halves.append(_causal_conv(u, prev, cw_ref[:, cols], cb_ref[:, cols]))
        gate, val = halves
        act_ref[:, c:c + tf] = (gate * (1.0 / (1.0 + jnp.exp(-gate))) * val).astype(BF16)
    y = x1 + _dot(act_ref[...], wd_ref[...])
    if final_norm:
        y = _rms_normalize(y) * fg_ref[...]
    out_ref[...] = y


def _layer_spec(stacked, layer):
    return pl.BlockSpec((None,) + stacked.shape[1:], lambda *_: (layer, 0, 0), pipeline_mode=pl.Buffered(1))


def _out_ffn(x2, o, layer, w_out, g, w_up, conv_w, conv_b, w_down, final_g, seq, tm, tf, final_norm):
    t, d = x2.shape
    d_ff = w_down.shape[1]
    stacked = (w_out, g, w_up, conv_w, conv_b, w_down)
    kernel = functools.partial(_out_ffn_kernel, tiles_per_seq=seq // tm, tf=tf, final_norm=final_norm)
    return pl.pallas_call(
        kernel,
        grid=(t // tm,),
        in_specs=[_row_spec(tm, d), _row_spec(tm, o.shape[1])] + [_layer_spec(w, layer) for w in stacked]
        + [_full_spec((1, d))],
        out_specs=_row_spec(tm, d),
        out_shape=jax.ShapeDtypeStruct((t, d), F32),
        scratch_shapes=[pltpu.VMEM((tm, 2 * d_ff), F32), pltpu.VMEM((tm, d_ff), BF16),
                        pltpu.VMEM((SUBLANES, 2 * d_ff), F32)],
        compiler_params=_params("arbitrary"),
        name="out_ffn",
    )(x2, o, *stacked, final_g.reshape(1, d))


def kernel(x, mem, ln_mix_g, w_in_a, b_f_a, w_in_b, ln_kv_g, w_kv, ln_mem_g, w_memkv, w_out, ln_ffn_g,
           w_up, conv_w, conv_b, w_down, final_g):
    batch, seq, d = x.shape
    n_mem = mem.shape[1]
    depth = ln_mix_g.shape[0]
    n_a = w_in_a.shape[0]
    assert w_in_b.shape[0] == 1, "one stick-breaking layer reads the shared K/V"
    tm_proj = min(1024, seq)
    tq_attn = min(MXU_WIDTH, seq)
    tm_ffn, tf = min(512, seq), MXU_WIDTH

    x2 = x.reshape(batch * seq, d)
    mem2 = mem.reshape(batch * n_mem, d)
    ffn_weights = (w_out.astype(BF16), ln_ffn_g.reshape(depth, 1, d), w_up.astype(BF16), conv_w,
                   conv_b.reshape(depth, 1, -1), w_down.astype(BF16))
    mk, mvt = _mem_proj(mem2, ln_mem_g, w_memkv, batch, n_mem)
    for layer in range(depth):
        if layer < n_a:
            q, k, vt, qm, cp = _proj_a(x2, ln_mix_g[layer], w_in_a[layer], b_f_a[layer], seq, tm_proj, tq_attn)
            o = _attention("fox", layer, q, k, cp, vt, qm, mk, mvt, batch, seq, n_mem, tq_attn)
        else:
            q, qm, k_sh, vt_sh = _proj_b(x2, ln_mix_g[layer], ln_kv_g, w_in_b[0], w_kv, seq, tm_proj, tq_attn)
            o = _attention("sb", layer, q, k_sh, None, vt_sh, qm, mk, mvt, batch, seq, n_mem, tq_attn)
        x2 = _out_ffn(x2, o, layer, *ffn_weights, final_g, seq, tm_ffn, tf, final_norm=(layer == depth - 1))
    return x2.reshape(batch, seq, d)
```
